```python
import math
import jax, jax.numpy as jnp
from jax import lax
import numpy as np

D_MODEL = 1024
BATCH = 2
SEQ = 16384
DEPTH = 4

N_A = DEPTH // 2
N_B = DEPTH - N_A

GLA_HEADS = 4
GLA_KD = D_MODEL // 2
GLA_VD = D_MODEL
GLA_DK = GLA_KD // GLA_HEADS
GLA_DV = GLA_VD // GLA_HEADS
GLA_RANK = 16
GLA_TAU = 16.0
GLA_CHUNK = 64

FOX_HEADS = 16
FOX_HEAD_DIM = 64
FOX_WIDTH = FOX_HEADS * FOX_HEAD_DIM
FOX_BLOCK = 128

D_FF = 2816
PLE_DIM = 256
LN_EPS = 1e-5
DEEPNORM_ALPHA = (2 * DEPTH) ** 0.25
DEEPNORM_BETA = (8 * DEPTH) ** -0.25

kernel_name = "yoco_gla_fox_macaron_deepnorm"


def layer_norm(x, g, b):
    xf = x.astype(jnp.float32)
    mu = jnp.mean(xf, axis=-1, keepdims=True)
    var = jnp.mean(jnp.square(xf - mu), axis=-1, keepdims=True)
    return ((xf - mu) * lax.rsqrt(var + LN_EPS) * g + b).astype(x.dtype)


def swiglu(x, w_in, w_out):
    g, u = jnp.split(x @ w_in, 2, axis=-1)
    return (jax.nn.silu(g) * u) @ w_out


def _to_chunks(t, hd):
    b, s, _ = t.shape
    return t.reshape(b, s // GLA_CHUNK, GLA_CHUNK, GLA_HEADS, hd).transpose(0, 3, 1, 2, 4)


def gla_mixer(x, w_in, w_a2, b_a, gn_g, gn_b, w_o):
    bsz, s, _ = x.shape
    proj = x @ w_in
    q, k, v, r, a_lr = jnp.split(
        proj, [GLA_KD, 2 * GLA_KD, 2 * GLA_KD + GLA_VD, 2 * GLA_KD + 2 * GLA_VD], axis=-1)
    log_a = jax.nn.log_sigmoid((a_lr @ w_a2).astype(jnp.float32) + b_a) / GLA_TAU

    q = _to_chunks(q * (GLA_DK ** -0.5), GLA_DK)
    k = _to_chunks(k, GLA_DK)
    v = _to_chunks(v, GLA_DV)
    bcum = jnp.cumsum(_to_chunks(log_a, GLA_DK), axis=3)
    b_last = bcum[..., -1:, :]

    q_dec = q * jnp.exp(bcum)
    k_intra = k * jnp.exp(-bcum)
    k_state = k * jnp.exp(b_last - bcum)

    causal = jnp.tril(jnp.ones((GLA_CHUNK, GLA_CHUNK), dtype=bool))
    attn = jnp.einsum('bhncd,bhnsd->bhncs', q_dec, k_intra)
    attn = jnp.where(causal, attn, 0.0)
    o_intra = jnp.einsum('bhncs,bhnsv->bhncv', attn, v.astype(attn.dtype))

    def step(state, inp):
        q_c, k_c, v_c, dec_c = inp
        o_c = jnp.einsum('bhcd,bhdv->bhcv', q_c, state)
        state = state * dec_c[..., None] + jnp.einsum('bhcd,bhcv->bhdv', k_c, v_c.astype(state.dtype))
        return state, o_c

    state0 = jnp.zeros((bsz, GLA_HEADS, GLA_DK, GLA_DV), dtype=q_dec.dtype)
    xs = (jnp.moveaxis(q_dec, 2, 0), jnp.moveaxis(k_state, 2, 0),
          jnp.moveaxis(v, 2, 0), jnp.moveaxis(jnp.exp(b_last[..., 0, :]), 2, 0))
    _, o_inter = lax.scan(step, state0, xs)
    o = o_intra + jnp.moveaxis(o_inter, 0, 2)

    o = o.transpose(0, 2, 3, 1, 4).reshape(bsz, s, GLA_HEADS, GLA_DV)
    o = layer_norm(o, gn_g.reshape(GLA_HEADS, GLA_DV), gn_b.reshape(GLA_HEADS, GLA_DV))
    o = o.reshape(bsz, s, GLA_VD).astype(x.dtype) * jax.nn.silu(r)
    return o @ w_o


def fox_shared_kv(h, w_kvf, b_f):
    bsz, s, _ = h.shape
    kvf = h @ w_kvf
    k, v, f_logit = jnp.split(kvf, [FOX_WIDTH, 2 * FOX_WIDTH], axis=-1)
    log_f = jax.nn.log_sigmoid(f_logit.astype(jnp.float32) + b_f)
    c = jnp.cumsum(log_f, axis=1).transpose(0, 2, 1)
    k = k.reshape(bsz, s, FOX_HEADS, FOX_HEAD_DIM).transpose(0, 2, 1, 3)
    v = v.reshape(bsz, s, FOX_HEADS, FOX_HEAD_DIM).transpose(0, 2, 1, 3)
    return k, v, c


def fox_mixer(x, k, v, c, w_in, w_o):
    bsz, s, _ = x.shape
    q, g = jnp.split(x @ w_in, 2, axis=-1)
    q = (q * (FOX_HEAD_DIM ** -0.5)).reshape(bsz, s, FOX_HEADS, FOX_HEAD_DIM).transpose(0, 2, 1, 3)
    nb = s // FOX_BLOCK
    q_blocks = q.reshape(bsz, FOX_HEADS, nb, FOX_BLOCK, FOX_HEAD_DIM).transpose(2, 0, 1, 3, 4)
    c_blocks = c.reshape(bsz, FOX_HEADS, nb, FOX_BLOCK).transpose(2, 0, 1, 3)
    k_pos = jnp.arange(s)

    def one_block(args):
        qb, cb, start = args
        logits = jnp.einsum('bhqd,bhkd->bhqk', qb, k).astype(jnp.float32)
        logits = logits + cb[..., :, None] - c[:, :, None, :]
        q_pos = start + jnp.arange(FOX_BLOCK)
        logits = jnp.where(k_pos[None, :] <= q_pos[:, None], logits, -jnp.inf)
        probs = jax.nn.softmax(logits, axis=-1)
        return jnp.einsum('bhqk,bhkd->bhqd', probs.astype(v.dtype), v)

    o = lax.map(one_block, (q_blocks, c_blocks, jnp.arange(nb) * FOX_BLOCK))
    o = o.transpose(1, 0, 3, 2, 4).reshape(bsz, s, FOX_WIDTH)
    o = o * jax.nn.sigmoid(g)
    return o @ w_o


def setup_inputs(seed: int = 0) -> dict:
    key = jax.random.key(seed)
    ks = jax.random.split(key, 32)
    f32 = jnp.float32

    def nrm(k, shape, fan_in, scale=1.0):
        return jax.random.normal(k, shape, f32) * (scale * fan_in ** -0.5)

    def gain(k, shape):
        return 1.0 + 0.02 * jax.random.normal(k, shape, f32)

    def bias(k, shape):
        return 0.02 * jax.random.normal(k, shape, f32)

    gla_in_width = 2 * GLA_KD + 2 * GLA_VD + GLA_RANK
    return {
        "x": jax.random.normal(ks[0], (BATCH, SEQ, D_MODEL), f32),
        "p": jax.random.normal(ks[1], (DEPTH, BATCH, SEQ, PLE_DIM), f32),
        "ffn1_w_in": nrm(ks[2], (DEPTH, D_MODEL, 2 * D_FF), D_MODEL),
        "ffn1_w_out": nrm(ks[3], (DEPTH, D_FF, D_MODEL), D_FF, DEEPNORM_BETA),
        "ln1_g": gain(ks[4], (DEPTH, D_MODEL)),
        "ln1_b": bias(ks[5], (DEPTH, D_MODEL)),
        "gla_w_in": nrm(ks[6], (N_A, D_MODEL, gla_in_width), D_MODEL),
        "gla_w_a2": nrm(ks[7], (N_A, GLA_RANK, GLA_KD), GLA_RANK),
        "gla_b_a": bias(ks[8], (N_A, GLA_KD)),
        "gla_gn_g": gain(ks[9], (N_A, GLA_VD)),
        "gla_gn_b": bias(ks[10], (N_A, GLA_VD)),
        "gla_w_o": nrm(ks[11], (N_A, GLA_VD, D_MODEL), GLA_VD, DEEPNORM_BETA),
        "fox_w_kvf": nrm(ks[12], (D_MODEL, 2 * FOX_WIDTH + FOX_HEADS), D_MODEL),
        "fox_b_f": 2.0 + 2.0 * jax.random.uniform(ks[13], (FOX_HEADS,), f32),
        "fox_w_in": nrm(ks[14], (N_B, D_MODEL, 2 * FOX_WIDTH), D_MODEL),
        "fox_w_o": nrm(ks[15], (N_B, FOX_WIDTH, D_MODEL), FOX_WIDTH, DEEPNORM_BETA),
        "ln2_g": gain(ks[16], (DEPTH, D_MODEL)),
        "ln2_b": bias(ks[17], (DEPTH, D_MODEL)),
        "ffn2_w_in": nrm(ks[18], (DEPTH, D_MODEL, 2 * D_FF), D_MODEL),
        "ffn2_w_out": nrm(ks[19], (DEPTH, D_FF, D_MODEL), D_FF, DEEPNORM_BETA),
        "ple_w_gate": nrm(ks[20], (DEPTH, D_MODEL, D_MODEL), D_MODEL),
        "ple_w_proj": nrm(ks[21], (DEPTH, PLE_DIM, D_MODEL), PLE_DIM, DEEPNORM_BETA),
        "ln3_g": gain(ks[22], (DEPTH, D_MODEL)),
        "ln3_b": bias(ks[23], (DEPTH, D_MODEL)),
    }


def reference(x, p, ffn1_w_in, ffn1_w_out, ln1_g, ln1_b, gla_w_in, gla_w_a2, gla_b_a, gla_gn_g,
              gla_gn_b, gla_w_o, fox_w_kvf, fox_b_f, fox_w_in, fox_w_o, ln2_g, ln2_b,
              ffn2_w_in, ffn2_w_out, ple_w_gate, ple_w_proj, ln3_g, ln3_b):
    k_sh = v_sh = c_sh = None
    for i in range(DEPTH):
        if i == N_A:
            k_sh, v_sh, c_sh = fox_shared_kv(x, fox_w_kvf, fox_b_f)
        x = layer_norm(DEEPNORM_ALPHA * x + 0.5 * swiglu(x, ffn1_w_in[i], ffn1_w_out[i]),
                       ln1_g[i], ln1_b[i])
        if i < N_A:
            mix = gla_mixer(x, gla_w_in[i], gla_w_a2[i], gla_b_a[i], gla_gn_g[i], gla_gn_b[i], gla_w_o[i])
        else:
            j = i - N_A
            mix = fox_mixer(x, k_sh, v_sh, c_sh, fox_w_in[j], fox_w_o[j])
        x = layer_norm(DEEPNORM_ALPHA * x + mix, ln2_g[i], ln2_b[i])
        ple = jax.nn.sigmoid(x @ ple_w_gate[i]) * (p[i] @ ple_w_proj[i])
        x = layer_norm(DEEPNORM_ALPHA * x + 0.5 * swiglu(x, ffn2_w_in[i], ffn2_w_out[i]) + ple,
                       ln3_g[i], ln3_b[i])
    return x
```

```python
import functools

import jax
import jax.numpy as jnp
from jax import lax
from jax.experimental import pallas as pl
from jax.experimental.pallas import tpu as pltpu

F32 = jnp.float32
BF16 = jnp.bfloat16

GLA_HEADS = 4
GLA_TAU = 16.0
GLA_CHUNK = 64
FOX_HEADS = 16
FOX_HEAD_DIM = 64
LN_EPS = 1e-5

V7X_LANES = 128
V7X_VMEM_LIMIT_BYTES = 56 * 1024 * 1024

TOKEN_TILE = 512
FFN_TOKEN_TILE = 256
FOX_BLOCK = 512
FOX_SLOT = 128
TRI_BLOCK = 256


def _params(*sems):
    return pltpu.CompilerParams(dimension_semantics=sems, vmem_limit_bytes=V7X_VMEM_LIMIT_BYTES)


def _resident(shape):
    zeros = (0,) * len(shape)
    return pl.BlockSpec(shape, lambda *_: zeros, pipeline_mode=pl.Buffered(1))


def _dot(a, b):
    return jnp.dot(a, b, preferred_element_type=F32)


def _dot_nt(a, b):
    return lax.dot_general(a, b, (((1,), (1,)), ((), ())), preferred_element_type=F32)


def _split3(a):
    hi = a.astype(BF16)
    r1 = a - hi.astype(F32)
    mid = r1.astype(BF16)
    lo = (r1 - mid.astype(F32)).astype(BF16)
    return hi, mid, lo


def _tri_cumsum(tri, a):
    hi, mid, lo = _split3(a)
    return _dot(tri, hi) + _dot(tri, mid) + _dot(tri, lo)


def _log_sigmoid(z):
    return jnp.minimum(z, 0.0) - jnp.log1p(jnp.exp(-jnp.abs(z)))


def _sigmoid(z):
    return 1.0 / (1.0 + jnp.exp(-z))


def _layer_norm(z, g, b):
    mu = jnp.mean(z, axis=-1, keepdims=True)
    zc = z - mu
    var = jnp.mean(zc * zc, axis=-1, keepdims=True)
    return zc * lax.rsqrt(var + LN_EPS) * g + b


def _ffn_kernel(*refs, alpha, d_ff, has_ple):
    if has_ple:
        x_ref, win_ref, wout_ref, g_ref, b_ref, p_ref, wgate_ref, wproj_ref, o_ref = refs
    else:
        x_ref, win_ref, wout_ref, g_ref, b_ref, o_ref = refs
    x = x_ref[...]
    xb = x.astype(BF16)
    gate = _dot(xb, win_ref[:, :d_ff])
    up = _dot(xb, win_ref[:, d_ff:])
    h = (gate * _sigmoid(gate) * up).astype(BF16)
    z = alpha * x + 0.5 * _dot(h, wout_ref[...])
    if has_ple:
        ple_gate = _sigmoid(_dot(xb, wgate_ref[...]))
        z = z + ple_gate * _dot(p_ref[...].astype(BF16), wproj_ref[...])
    o_ref[...] = _layer_norm(z, g_ref[...], b_ref[...])


def _ffn(x, w_in, w_out, ln_g, ln_b, alpha, ple=None):
    t, d = x.shape
    d_ff = w_out.shape[0]
    tm = FFN_TOKEN_TILE
    row = lambda i: (i, 0)
    in_specs = [pl.BlockSpec((tm, d), row), _resident(w_in.shape), _resident(w_out.shape),
                _resident(ln_g.shape), _resident(ln_b.shape)]
    args = [x, w_in, w_out, ln_g, ln_b]
    if ple is not None:
        p, w_gate, w_proj = ple
        in_specs += [pl.BlockSpec((tm, p.shape[1]), row), _resident(w_gate.shape), _resident(w_proj.shape)]
        args += [p, w_gate, w_proj]
    return pl.pallas_call(
        functools.partial(_ffn_kernel, alpha=alpha, d_ff=d_ff, has_ple=ple is not None),
        grid=(t // tm,),
        in_specs=in_specs,
        out_specs=pl.BlockSpec((tm, d), row),
        out_shape=jax.ShapeDtypeStruct((t, d), F32),
        compiler_params=_params("parallel"),
        name="ffn_ple" if ple is not None else "ffn",
    )(*args)


def _out_proj_kernel(y_ref, x_ref, w_ref, g_ref, b_ref, o_ref, *, alpha):
    z = alpha * x_ref[...] + _dot(y_ref[...], w_ref[...])
    o_ref[...] = _layer_norm(z, g_ref[...], b_ref[...])


def _out_proj(y, x, w_o, ln_g, ln_b, alpha):
    t, d = x.shape
    tm = TOKEN_TILE
    row = lambda i: (i, 0)
    return pl.pallas_call(
        functools.partial(_out_proj_kernel, alpha=alpha),
        grid=(t // tm,),
        in_specs=[pl.BlockSpec((tm, y.shape[1]), row), pl.BlockSpec((tm, d), row), _resident(w_o.shape),
                  _resident(ln_g.shape), _resident(ln_b.shape)],
        out_specs=pl.BlockSpec((tm, d), row),
        out_shape=jax.ShapeDtypeStruct((t, d), F32),
        compiler_params=_params("parallel"),
        name="out_proj",
    )(y, x, w_o, ln_g, ln_b)


def _gla_proj_kernel(x_ref, win_ref, wa2_ref, ba_ref, tri_ref,
                     qd_ref, ki_ref, ks_ref, v_ref, dec_ref, gate_ref, *, kd, vd):
    tm = x_ref.shape[0]
    xb = x_ref[...].astype(BF16)
    proj = _dot(xb, win_ref[...])
    q = proj[:, :kd] * ((kd // GLA_HEADS) ** -0.5)
    k = proj[:, kd:2 * kd]
    v = proj[:, 2 * kd:2 * kd + vd]
    r = proj[:, 2 * kd + vd:2 * kd + 2 * vd]
    a_lr = proj[:, 2 * kd + 2 * vd:]
    log_a = _log_sigmoid(_dot(a_lr.astype(BF16), wa2_ref[...]) + ba_ref[...]) * (1.0 / GLA_TAU)
    tri = tri_ref[...]
    bcum = jnp.concatenate(
        [_tri_cumsum(tri, log_a[s:s + TRI_BLOCK]) for s in range(0, tm, TRI_BLOCK)], axis=0)
    nc = tm // GLA_CHUNK
    b3 = bcum.reshape(nc, GLA_CHUNK, kd)
    b_last = b3[:, GLA_CHUNK - 1:GLA_CHUNK, :]
    qd_ref[...] = (q * jnp.exp(bcum)).astype(BF16)
    ki_ref[...] = (k * jnp.exp(-bcum)).astype(BF16)
    ks_ref[...] = (k.reshape(nc, GLA_CHUNK, kd) * jnp.exp(b_last - b3)).reshape(tm, kd).astype(BF16)
    dec_ref[...] = jnp.exp(b_last)
    v_ref[...] = v.astype(BF16)
    gate_ref[...] = r * _sigmoid(r)


def _gla_proj(x, w_in, w_a2, b_a, tri):
    t, d = x.shape
    kd = w_a2.shape[1]
    vd = (w_in.shape[1] - 2 * kd - w_a2.shape[0]) // 2
    tm = TOKEN_TILE
    row = lambda i: (i, 0)
    return pl.pallas_call(
        functools.partial(_gla_proj_kernel, kd=kd, vd=vd),
        grid=(t // tm,),
        in_specs=[pl.BlockSpec((tm, d), row), _resident(w_in.shape), _resident(w_a2.shape),
                  _resident(b_a.shape), _resident(tri.shape)],
        out_specs=[pl.BlockSpec((tm, kd), row), pl.BlockSpec((tm, kd), row), pl.BlockSpec((tm, kd), row),
                   pl.BlockSpec((tm, vd), row), pl.BlockSpec((tm // GLA_CHUNK, 1, kd), lambda i: (i, 0, 0)),
                   pl.BlockSpec((tm, vd), row)],
        out_shape=[jax.ShapeDtypeStruct((t, kd), BF16)] * 3 + [
            jax.ShapeDtypeStruct((t, vd), BF16),
            jax.ShapeDtypeStruct((t // GLA_CHUNK, 1, kd), F32),
            jax.ShapeDtypeStruct((t, vd), F32)],
        compiler_params=_params("parallel"),
        name="gla_proj",
    )(x, w_in, w_a2, b_a, tri)


def _gla_rec_kernel(qd_ref, ki_ref, ks_ref, v_ref, dec_ref, gate_ref, gng_ref, gnb_ref, o_ref, st_ref):
    tm = qd_ref.shape[1]
    dk = qd_ref.shape[2] // GLA_HEADS
    dv = v_ref.shape[2] // GLA_HEADS

    @pl.when(pl.program_id(1) == 0)
    def _():
        st_ref[...] = jnp.zeros_like(st_ref)

    rows = lax.broadcasted_iota(jnp.int32, (GLA_CHUNK, GLA_CHUNK), 0)
    cols = lax.broadcasted_iota(jnp.int32, (GLA_CHUNK, GLA_CHUNK), 1)
    causal = cols <= rows

    def chunk(c, carry):
        r0 = pl.multiple_of(c * GLA_CHUNK, GLA_CHUNK)
        rs = pl.ds(r0, GLA_CHUNK)
        for h in range(GLA_HEADS):
            ksl = slice(h * dk, (h + 1) * dk)
            vsl = slice(h * dv, (h + 1) * dv)
            qd = qd_ref[0, rs, ksl]
            ki = ki_ref[0, rs, ksl]
            ks = ks_ref[0, rs, ksl]
            vv = v_ref[0, rs, vsl]
            attn = jnp.where(causal, _dot_nt(qd, ki), 0.0).astype(BF16)
            st = st_ref[h]
            o = _dot(attn, vv) + _dot_nt(qd, st.astype(BF16))
            vt = jnp.transpose(vv.astype(F32)).astype(BF16)
            st_ref[h] = st * dec_ref[0, c, :, ksl] + _dot(vt, ks)
            on = _layer_norm(o, gng_ref[:, vsl], gnb_ref[:, vsl])
            o_ref[0, rs, vsl] = (on * gate_ref[0, rs, vsl]).astype(BF16)
        return carry

    lax.fori_loop(0, tm // GLA_CHUNK, chunk, 0)


def _gla_rec(qd, ki, ks, v, dec, gate, gn_g, gn_b):
    b, s, kd = qd.shape
    vd = v.shape[2]
    tm = TOKEN_TILE
    blk = lambda bi, i: (bi, i, 0)
    return pl.pallas_call(
        _gla_rec_kernel,
        grid=(b, s // tm),
        in_specs=[pl.BlockSpec((1, tm, kd), blk), pl.BlockSpec((1, tm, kd), blk), pl.BlockSpec((1, tm, kd), blk),
                  pl.BlockSpec((1, tm, vd), blk),
                  pl.BlockSpec((1, tm // GLA_CHUNK, 1, kd), lambda bi, i: (bi, i, 0, 0)),
                  pl.BlockSpec((1, tm, vd), blk), _resident(gn_g.shape), _resident(gn_b.shape)],
        out_specs=pl.BlockSpec((1, tm, vd), blk),
        out_shape=jax.ShapeDtypeStruct((b, s, vd), BF16),
        scratch_shapes=[pltpu.VMEM((GLA_HEADS, vd // GLA_HEADS, kd // GLA_HEADS), F32)],
        compiler_params=_params("arbitrary", "arbitrary"),
        name="gla_rec",
    )(qd, ki, ks, v, dec, gate, gn_g, gn_b)


def _fox_kv_kernel(x_ref, wk_ref, wvt_ref, wf_ref, bf_ref, tri_ref, spread_ref,
                   k_ref, vt_ref, cblk_ref, carry_ref):
    @pl.when(pl.program_id(1) == 0)
    def _():
        carry_ref[...] = jnp.zeros_like(carry_ref)

    tk = x_ref.shape[1]
    xb = x_ref[0].astype(BF16)
    log_f = _log_sigmoid(_dot(xb, wf_ref[...]) + bf_ref[...])
    c_rel = _tri_cumsum(tri_ref[...], log_f)
    hi, mid, lo = _split3(-c_rel)
    bias = _dot(jnp.concatenate([hi, mid, lo], axis=1), spread_ref[...])
    k_ref[0, 0] = (_dot(xb, wk_ref[...]) + bias).astype(BF16)
    vt_ref[0, 0] = _dot_nt(wvt_ref[...], xb).astype(BF16)
    cblk_ref[0, 0] = carry_ref[...]
    carry_ref[...] = carry_ref[...] + c_rel[tk - 1:tk, :]


def _fox_kv(x, wk_aug, wvt, wf, b_f, tri, spread):
    b, s, d = x.shape
    tk = FOX_BLOCK
    nkv = s // tk
    hw = wk_aug.shape[1]
    vw = wvt.shape[0]
    nh = wf.shape[1]
    return pl.pallas_call(
        _fox_kv_kernel,
        grid=(b, nkv),
        in_specs=[pl.BlockSpec((1, tk, d), lambda bi, j: (bi, j, 0)), _resident(wk_aug.shape),
                  _resident(wvt.shape), _resident(wf.shape), _resident(b_f.shape), _resident(tri.shape),
                  _resident(spread.shape)],
        out_specs=[pl.BlockSpec((1, 1, tk, hw), lambda bi, j: (bi, j, 0, 0)),
                   pl.BlockSpec((1, 1, vw, tk), lambda bi, j: (bi, j, 0, 0)),
                   pl.BlockSpec((1, 1, 1, nh), lambda bi, j: (bi, j, 0, 0))],
        out_shape=[jax.ShapeDtypeStruct((b, nkv, tk, hw), BF16),
                   jax.ShapeDtypeStruct((b, nkv, vw, tk), BF16),
                   jax.ShapeDtypeStruct((b, nkv, 1, nh), F32)],
        scratch_shapes=[pltpu.VMEM((1, nh), F32)],
        compiler_params=_params("arbitrary", "arbitrary"),
        name="fox_kv",
    )(x, wk_aug, wvt, wf, b_f, tri, spread)


def _fox_q_kernel(x_ref, wqt_ref, wg_ref, qt_ref, g_ref):
    xb = x_ref[0].astype(BF16)
    qt = _dot_nt(wqt_ref[...], xb) * (FOX_HEAD_DIM ** -0.5)
    slot_row = lax.broadcasted_iota(jnp.int32, qt.shape, 0) % FOX_SLOT
    is_bias_row = (slot_row >= FOX_HEAD_DIM) & (slot_row < FOX_HEAD_DIM + 3)
    qt_ref[0] = jnp.where(is_bias_row, 1.0, qt).astype(BF16)
    g_ref[0] = _dot(xb, wg_ref[...])


def _fox_q(x, wqt_aug, wg):
    b, s, d = x.shape
    tm = TOKEN_TILE
    hw = wqt_aug.shape[0]
    return pl.pallas_call(
        _fox_q_kernel,
        grid=(b, s // tm),
        in_specs=[pl.BlockSpec((1, tm, d), lambda bi, i: (bi, i, 0)), _resident(wqt_aug.shape),
                  _resident(wg.shape)],
        out_specs=[pl.BlockSpec((1, hw, tm), lambda bi, i: (bi, 0, i)),
                   pl.BlockSpec((1, tm, wg.shape[1]), lambda bi, i: (bi, i, 0))],
        out_shape=[jax.ShapeDtypeStruct((b, hw, s), BF16), jax.ShapeDtypeStruct((b, s, wg.shape[1]), F32)],
        compiler_params=_params("parallel", "parallel"),
        name="fox_q",
    )(x, wqt_aug, wg)


FOX_HEADS_PER_STEP = 2


def _fox_attn_kernel(cblk_ref, qt_ref, k_ref, vt_ref, g_ref, o_ref, acc_ref, *, nkv):
    bi = pl.program_id(0)
    hp = pl.program_id(1)
    i = pl.program_id(2)
    tq = qt_ref.shape[2]
    tk = k_ref.shape[2]
    dh = FOX_HEAD_DIM
    nhp = FOX_HEADS_PER_STEP

    def head_state(hh):
        return (bi * FOX_HEADS + hp * nhp + hh) * nkv

    def block(hh, j, masked):
        kj = k_ref[0, j, :, hh * FOX_SLOT:(hh + 1) * FOX_SLOT]
        s = _dot(kj, qt_ref[0, hh * FOX_SLOT:(hh + 1) * FOX_SLOT, :])
        if masked:
            key_pos = lax.broadcasted_iota(jnp.int32, (tk, tq), 0)
            qry_pos = lax.broadcasted_iota(jnp.int32, (tk, tq), 1)
            s = jnp.where(key_pos <= qry_pos, s, -jnp.inf)
        return s, cblk_ref[head_state(hh) + j], vt_ref[0, j, hh * dh:(hh + 1) * dh, :]

    carry = []
    for hh in range(nhp):
        s, cb, vj = block(hh, i, True)
        m = jnp.max(s, axis=0, keepdims=True) - cb
        p = jnp.exp(s - (m + cb))
        carry += [m, jnp.sum(p, axis=0, keepdims=True)]
        acc_ref[hh * dh:(hh + 1) * dh, :] = _dot(vj, p.astype(BF16))

    def step(jj, carry):
        j = i - 1 - jj
        out = []
        for hh in range(nhp):
            m, l = carry[2 * hh], carry[2 * hh + 1]
            s, cb, vj = block(hh, j, False)
            m_new = jnp.maximum(m, jnp.max(s, axis=0, keepdims=True) - cb)
            alpha = jnp.exp(m - m_new)
            p = jnp.exp(s - (m_new + cb))
            out += [m_new, alpha * l + jnp.sum(p, axis=0, keepdims=True)]
            rows = slice(hh * dh, (hh + 1) * dh)
            acc_ref[rows, :] = alpha * acc_ref[rows, :] + _dot(vj, p.astype(BF16))
        return tuple(out)

    carry = lax.fori_loop(0, i, step, tuple(carry))

    inv = jnp.concatenate([jnp.broadcast_to(1.0 / carry[2 * hh + 1], (dh, tq)) for hh in range(nhp)], axis=0)
    o = jnp.transpose(acc_ref[...] * inv)
    o_ref[0] = (o * _sigmoid(g_ref[0])).astype(BF16)


def _fox_attn(cblk, qt, k, vt, g):
    b, nkv, tk, hw = k.shape
    s = nkv * tk
    tq = FOX_BLOCK
    nhp = FOX_HEADS_PER_STEP
    out_w = nhp * FOX_HEAD_DIM
    return pl.pallas_call(
        functools.partial(_fox_attn_kernel, nkv=nkv),
        grid=(b, FOX_HEADS // nhp, s // tq),
        in_specs=[pl.BlockSpec(memory_space=pltpu.SMEM),
                  pl.BlockSpec((1, nhp * FOX_SLOT, tq), lambda bi, hp, i: (bi, hp, i)),
                  pl.BlockSpec((1, nkv, tk, nhp * FOX_SLOT), lambda bi, hp, i: (bi, 0, 0, hp)),
                  pl.BlockSpec((1, nkv, out_w, tk), lambda bi, hp, i: (bi, 0, hp, 0)),
                  pl.BlockSpec((1, tq, out_w), lambda bi, hp, i: (bi, i, hp))],
        out_specs=pl.BlockSpec((1, tq, out_w), lambda bi, hp, i: (bi, i, hp)),
        out_shape=jax.ShapeDtypeStruct((b, s, FOX_HEADS * FOX_HEAD_DIM), BF16),
        scratch_shapes=[pltpu.VMEM((out_w, tq), F32)],
        compiler_params=_params("arbitrary", "arbitrary", "arbitrary"),
        name="fox_attn",
    )(cblk, qt, k, vt, g)


def _block_tri(n, block):
    r = jnp.arange(n)
    return ((r[:, None] >= r[None, :]) & (r[:, None] // block == r[None, :] // block)).astype(BF16)


def _spread_heads(w):
    d = w.shape[0]
    w = w.reshape(d, FOX_HEADS, FOX_HEAD_DIM)
    w = jnp.pad(w, ((0, 0), (0, 0), (0, FOX_SLOT - FOX_HEAD_DIM)))
    return w.reshape(d, FOX_HEADS * FOX_SLOT)


def _bias_spread_matrix():
    rows = jnp.arange(3 * FOX_HEADS)
    term, head = rows // FOX_HEADS, rows % FOX_HEADS
    cols = head * FOX_SLOT + FOX_HEAD_DIM + term
    return (jnp.arange(FOX_HEADS * FOX_SLOT)[None, :] == cols[:, None]).astype(BF16)


def kernel(x, p, ffn1_w_in, ffn1_w_out, ln1_g, ln1_b, gla_w_in, gla_w_a2, gla_b_a, gla_gn_g, gla_gn_b, gla_w_o, fox_w_kvf, fox_b_f, fox_w_in, fox_w_o, ln2_g, ln2_b, ffn2_w_in, ffn2_w_out, ple_w_gate, ple_w_proj, ln3_g, ln3_b):
    bsz, seq, d = x.shape
    depth = ffn1_w_in.shape[0]
    n_gla = gla_w_in.shape[0]
    alpha = (2 * depth) ** 0.25
    t = bsz * seq
    fox_w = FOX_HEADS * FOX_HEAD_DIM
    row = lambda a: a.reshape(1, -1)

    tri_chunk = _block_tri(TRI_BLOCK, GLA_CHUNK)
    tri_block = _block_tri(FOX_BLOCK, FOX_BLOCK)
    bias_spread = _bias_spread_matrix()

    xf = x.reshape(t, d)
    pf = p.reshape(depth, t, p.shape[-1])
    fox_k = fox_vt = fox_cblk = None
    for i in range(depth):
        if i == n_gla:
            wk_aug = _spread_heads(fox_w_kvf[:, :fox_w]).astype(BF16)
            wvt = fox_w_kvf[:, fox_w:2 * fox_w].T.astype(BF16)
            wf = fox_w_kvf[:, 2 * fox_w:].astype(BF16)
            fox_k, fox_vt, cblk = _fox_kv(xf.reshape(bsz, seq, d), wk_aug, wvt, wf, row(fox_b_f),
                                          tri_block, bias_spread)
            fox_cblk = jnp.transpose(cblk[:, :, 0, :], (0, 2, 1)).reshape(-1)
        xf = _ffn(xf, ffn1_w_in[i].astype(BF16), ffn1_w_out[i].astype(BF16), row(ln1_g[i]), row(ln1_b[i]), alpha)
        if i < n_gla:
            qd, ki, ks, v, dec, gate = _gla_proj(xf, gla_w_in[i].astype(BF16), gla_w_a2[i].astype(BF16),
                                                 row(gla_b_a[i]), tri_chunk)
            b3 = lambda a: a.reshape(bsz, seq, -1)
            mix = _gla_rec(b3(qd), b3(ki), b3(ks), b3(v), dec.reshape(bsz, seq // GLA_CHUNK, 1, -1), b3(gate),
                           row(gla_gn_g[i]), row(gla_gn_b[i])).reshape(t, -1)
            w_o = gla_w_o[i]
        else:
            j = i - n_gla
            wqt_aug = _spread_heads(fox_w_in[j][:, :fox_w]).T.astype(BF16)
            qt, g = _fox_q(xf.reshape(bsz, seq, d), wqt_aug, fox_w_in[j][:, fox_w:].astype(BF16))
            mix = _fox_attn(fox_cblk, qt, fox_k, fox_vt, g).reshape(t, -1)
            w_o = fox_w_o[j]
        xf = _out_proj(mix, xf, w_o.astype(BF16), row(ln2_g[i]), row(ln2_b[i]), alpha)
        xf = _ffn(xf, ffn2_w_in[i].astype(BF16), ffn2_w_out[i].astype(BF16), row(ln3_g[i]), row(ln3_b[i]), alpha,
                  ple=(pf[i], ple_w_gate[i].astype(BF16), ple_w_proj[i].astype(BF16)))
    return xf.reshape(bsz, seq, d)
```

```python
import functools

import jax
import jax.numpy as jnp
from jax import lax
from jax.experimental import pallas as pl
from jax.experimental.pallas import tpu as pltpu

F32 = jnp.float32
BF16 = jnp.bfloat16

GLA_HEADS = 4
GLA_TAU = 16.0
GLA_CHUNK = 64
FOX_HEADS = 16
FOX_HEAD_DIM = 64
LN_EPS = 1e-5

V7X_LANES = 128
V7X_VMEM_LIMIT_BYTES = 56 * 1024 * 1024

TOKEN_TILE = 512
FFN_TOKEN_TILE = 256
FOX_BLOCK = 512
FOX_SLOT = 128
TRI_BLOCK = 256
EXP_ZERO_GAP = 105.0
NORM_SLACK = 1.01


def _params(*sems):
    return pltpu.CompilerParams(dimension_semantics=sems, vmem_limit_bytes=V7X_VMEM_LIMIT_BYTES)


def _resident(shape):
    zeros = (0,) * len(shape)
    return pl.BlockSpec(shape, lambda *_: zeros, pipeline_mode=pl.Buffered(1))


def _dot(a, b):
    return jnp.dot(a, b, preferred_element_type=F32)


def _dot_nt(a, b):
    return lax.dot_general(a, b, (((1,), (1,)), ((), ())), preferred_element_type=F32)


def _split3(a):
    hi = a.astype(BF16)
    r1 = a - hi.astype(F32)
    mid = r1.astype(BF16)
    lo = (r1 - mid.astype(F32)).astype(BF16)
    return hi, mid, lo


def _tri_cumsum(tri, a):
    hi, mid, lo = _split3(a)
    return _dot(tri, hi) + _dot(tri, mid) + _dot(tri, lo)


def _log_sigmoid(z):
    return jnp.minimum(z, 0.0) - jnp.log1p(jnp.exp(-jnp.abs(z)))


def _sigmoid(z):
    return 1.0 / (1.0 + jnp.exp(-z))


def _layer_norm(z, g, b):
    mu = jnp.mean(z, axis=-1, keepdims=True)
    zc = z - mu
    var = jnp.mean(zc * zc, axis=-1, keepdims=True)
    return zc * lax.rsqrt(var + LN_EPS) * g + b


def _ffn_kernel(*refs, alpha, d_ff, has_ple):
    if has_ple:
        x_ref, win_ref, wout_ref, g_ref, b_ref, p_ref, wgate_ref, wproj_ref, o_ref = refs
    else:
        x_ref, win_ref, wout_ref, g_ref, b_ref, o_ref = refs
    x = x_ref[...]
    xb = x.astype(BF16)
    gate = _dot(xb, win_ref[:, :d_ff])
    up = _dot(xb, win_ref[:, d_ff:])
    h = (gate * _sigmoid(gate) * up).astype(BF16)
    z = alpha * x + 0.5 * _dot(h, wout_ref[...])
    if has_ple:
        ple_gate = _sigmoid(_dot(xb, wgate_ref[...]))
        z = z + ple_gate * _dot(p_ref[...].astype(BF16), wproj_ref[...])
    o_ref[...] = _layer_norm(z, g_ref[...], b_ref[...])


def _ffn(x, w_in, w_out, ln_g, ln_b, alpha, ple=None):
    t, d = x.shape
    d_ff = w_out.shape[0]
    tm = FFN_TOKEN_TILE
    row = lambda i: (i, 0)
    in_specs = [pl.BlockSpec((tm, d), row), _resident(w_in.shape), _resident(w_out.shape),
                _resident(ln_g.shape), _resident(ln_b.shape)]
    args = [x, w_in, w_out, ln_g, ln_b]
    if ple is not None:
        p, w_gate, w_proj = ple
        in_specs += [pl.BlockSpec((tm, p.shape[1]), row), _resident(w_gate.shape), _resident(w_proj.shape)]
        args += [p, w_gate, w_proj]
    return pl.pallas_call(
        functools.partial(_ffn_kernel, alpha=alpha, d_ff=d_ff, has_ple=ple is not None),
        grid=(t // tm,),
        in_specs=in_specs,
        out_specs=pl.BlockSpec((tm, d), row),
        out_shape=jax.ShapeDtypeStruct((t, d), F32),
        compiler_params=_params("parallel"),
        name="ffn_ple" if ple is not None else "ffn",
    )(*args)


def _out_proj_kernel(y_ref, x_ref, w_ref, g_ref, b_ref, o_ref, *, alpha):
    z = alpha * x_ref[...] + _dot(y_ref[...], w_ref[...])
    o_ref[...] = _layer_norm(z, g_ref[...], b_ref[...])


def _out_proj(y, x, w_o, ln_g, ln_b, alpha):
    t, d = x.shape
    tm = TOKEN_TILE
    row = lambda i: (i, 0)
    return pl.pallas_call(
        functools.partial(_out_proj_kernel, alpha=alpha),
        grid=(t // tm,),
        in_specs=[pl.BlockSpec((tm, y.shape[1]), row), pl.BlockSpec((tm, d), row), _resident(w_o.shape),
                  _resident(ln_g.shape), _resident(ln_b.shape)],
        out_specs=pl.BlockSpec((tm, d), row),
        out_shape=jax.ShapeDtypeStruct((t, d), F32),
        compiler_params=_params("parallel"),
        name="out_proj",
    )(y, x, w_o, ln_g, ln_b)


def _gla_proj_kernel(x_ref, win_ref, wa2_ref, ba_ref, tri_ref,
                     qd_ref, ki_ref, ks_ref, v_ref, dec_ref, gate_ref, *, kd, vd):
    tm = x_ref.shape[0]
    xb = x_ref[...].astype(BF16)
    proj = _dot(xb, win_ref[...])
    q = proj[:, :kd] * ((kd // GLA_HEADS) ** -0.5)
    k = proj[:, kd:2 * kd]
    v = proj[:, 2 * kd:2 * kd + vd]
    r = proj[:, 2 * kd + vd:2 * kd + 2 * vd]
    a_lr = proj[:, 2 * kd + 2 * vd:]
    log_a = _log_sigmoid(_dot(a_lr.astype(BF16), wa2_ref[...]) + ba_ref[...]) * (1.0 / GLA_TAU)
    tri = tri_ref[...]
    bcum = jnp.concatenate(
        [_tri_cumsum(tri, log_a[s:s + TRI_BLOCK]) for s in range(0, tm, TRI_BLOCK)], axis=0)
    nc = tm // GLA_CHUNK
    b3 = bcum.reshape(nc, GLA_CHUNK, kd)
    b_last = b3[:, GLA_CHUNK - 1:GLA_CHUNK, :]
    qd_ref[...] = (q * jnp.exp(bcum)).astype(BF16)
    ki_ref[...] = (k * jnp.exp(-bcum)).astype(BF16)
    ks_ref[...] = (k.reshape(nc, GLA_CHUNK, kd) * jnp.exp(b_last - b3)).reshape(tm, kd).astype(BF16)
    dec_ref[...] = jnp.exp(b_last)
    v_ref[...] = v.astype(BF16)
    gate_ref[...] = r * _sigmoid(r)


def _gla_proj(x, w_in, w_a2, b_a, tri):
    t, d = x.shape
    kd = w_a2.shape[1]
    vd = (w_in.shape[1] - 2 * kd - w_a2.shape[0]) // 2
    tm = TOKEN_TILE
    row = lambda i: (i, 0)
    return pl.pallas_call(
        functools.partial(_gla_proj_kernel, kd=kd, vd=vd),
        grid=(t // tm,),
        in_specs=[pl.BlockSpec((tm, d), row), _resident(w_in.shape), _resident(w_a2.shape),
                  _resident(b_a.shape), _resident(tri.shape)],
        out_specs=[pl.BlockSpec((tm, kd), row), pl.BlockSpec((tm, kd), row), pl.BlockSpec((tm, kd), row),
                   pl.BlockSpec((tm, vd), row), pl.BlockSpec((tm // GLA_CHUNK, 1, kd), lambda i: (i, 0, 0)),
                   pl.BlockSpec((tm, vd), row)],
        out_shape=[jax.ShapeDtypeStruct((t, kd), BF16)] * 3 + [
            jax.ShapeDtypeStruct((t, vd), BF16),
            jax.ShapeDtypeStruct((t // GLA_CHUNK, 1, kd), F32),
            jax.ShapeDtypeStruct((t, vd), F32)],
        compiler_params=_params("parallel"),
        name="gla_proj",
    )(x, w_in, w_a2, b_a, tri)


def _gla_rec_kernel(qd_ref, ki_ref, ks_ref, v_ref, dec_ref, gate_ref, gng_ref, gnb_ref, o_ref, st_ref):
    tm = qd_ref.shape[1]
    dk = qd_ref.shape[2] // GLA_HEADS
    dv = v_ref.shape[2] // GLA_HEADS

    @pl.when(pl.program_id(1) == 0)
    def _():
        st_ref[...] = jnp.zeros_like(st_ref)

    rows = lax.broadcasted_iota(jnp.int32, (GLA_CHUNK, GLA_CHUNK), 0)
    cols = lax.broadcasted_iota(jnp.int32, (GLA_CHUNK, GLA_CHUNK), 1)
    causal = cols <= rows

    def chunk(c, carry):
        r0 = pl.multiple_of(c * GLA_CHUNK, GLA_CHUNK)
        rs = pl.ds(r0, GLA_CHUNK)
        for h in range(GLA_HEADS):
            ksl = slice(h * dk, (h + 1) * dk)
            vsl = slice(h * dv, (h + 1) * dv)
            qd = qd_ref[0, rs, ksl]
            ki = ki_ref[0, rs, ksl]
            ks = ks_ref[0, rs, ksl]
            vv = v_ref[0, rs, vsl]
            attn = jnp.where(causal, _dot_nt(qd, ki), 0.0).astype(BF16)
            st = st_ref[h]
            o = _dot(attn, vv) + _dot_nt(qd, st.astype(BF16))
            vt = jnp.transpose(vv.astype(F32)).astype(BF16)
            st_ref[h] = st * dec_ref[0, c, :, ksl] + _dot(vt, ks)
            on = _layer_norm(o, gng_ref[:, vsl], gnb_ref[:, vsl])
            o_ref[0, rs, vsl] = (on * gate_ref[0, rs, vsl]).astype(BF16)
        return carry

    lax.fori_loop(0, tm // GLA_CHUNK, chunk, 0)


def _gla_rec(qd, ki, ks, v, dec, gate, gn_g, gn_b):
    b, s, kd = qd.shape
    vd = v.shape[2]
    tm = TOKEN_TILE
    blk = lambda bi, i: (bi, i, 0)
    return pl.pallas_call(
        _gla_rec_kernel,
        grid=(b, s // tm),
        in_specs=[pl.BlockSpec((1, tm, kd), blk), pl.BlockSpec((1, tm, kd), blk), pl.BlockSpec((1, tm, kd), blk),
                  pl.BlockSpec((1, tm, vd), blk),
                  pl.BlockSpec((1, tm // GLA_CHUNK, 1, kd), lambda bi, i: (bi, i, 0, 0)),
                  pl.BlockSpec((1, tm, vd), blk), _resident(gn_g.shape), _resident(gn_b.shape)],
        out_specs=pl.BlockSpec((1, tm, vd), blk),
        out_shape=jax.ShapeDtypeStruct((b, s, vd), BF16),
        scratch_shapes=[pltpu.VMEM((GLA_HEADS, vd // GLA_HEADS, kd // GLA_HEADS), F32)],
        compiler_params=_params("arbitrary", "arbitrary"),
        name="gla_rec",
    )(qd, ki, ks, v, dec, gate, gn_g, gn_b)


def _fox_kv_kernel(x_ref, wk_ref, wvt_ref, wf_ref, bf_ref, tri_ref, spread_ref, group_ref,
                   k_ref, vt_ref, cblk_ref, kpm_ref, carry_ref, kmax_ref):
    @pl.when(pl.program_id(1) == 0)
    def _():
        carry_ref[...] = jnp.zeros_like(carry_ref)
        kmax_ref[...] = jnp.zeros_like(kmax_ref)

    tk = x_ref.shape[1]
    xb = x_ref[0].astype(BF16)
    log_f = _log_sigmoid(_dot(xb, wf_ref[...]) + bf_ref[...])
    c_rel = _tri_cumsum(tri_ref[...], log_f)
    hi, mid, lo = _split3(-c_rel)
    bias = _dot(jnp.concatenate([hi, mid, lo], axis=1), spread_ref[...])
    k = _dot(xb, wk_ref[...])
    k_ref[0, 0] = (k + bias).astype(BF16)
    vt_ref[0, 0] = _dot_nt(wvt_ref[...], xb).astype(BF16)
    cblk_ref[0, 0] = carry_ref[...]
    carry_ref[...] = carry_ref[...] + c_rel[tk - 1:tk, :]
    kr = k.astype(BF16).astype(F32)
    norm2 = _dot((kr * kr).astype(BF16), group_ref[...]) * NORM_SLACK
    kmax_ref[...] = jnp.maximum(kmax_ref[...], jnp.sqrt(jnp.max(norm2, axis=0, keepdims=True)))
    kpm_ref[0, 0] = kmax_ref[...]


def _fox_kv(x, wk_aug, wvt, wf, b_f, tri, spread, group):
    b, s, d = x.shape
    tk = FOX_BLOCK
    nkv = s // tk
    hw = wk_aug.shape[1]
    vw = wvt.shape[0]
    nh = wf.shape[1]
    return pl.pallas_call(
        _fox_kv_kernel,
        grid=(b, nkv),
        in_specs=[pl.BlockSpec((1, tk, d), lambda bi, j: (bi, j, 0)), _resident(wk_aug.shape),
                  _resident(wvt.shape), _resident(wf.shape), _resident(b_f.shape), _resident(tri.shape),
                  _resident(spread.shape), _resident(group.shape)],
        out_specs=[pl.BlockSpec((1, 1, tk, hw), lambda bi, j: (bi, j, 0, 0)),
                   pl.BlockSpec((1, 1, vw, tk), lambda bi, j: (bi, j, 0, 0)),
                   pl.BlockSpec((1, 1, 1, nh), lambda bi, j: (bi, j, 0, 0)),
                   pl.BlockSpec((1, 1, 1, nh), lambda bi, j: (bi, j, 0, 0))],
        out_shape=[jax.ShapeDtypeStruct((b, nkv, tk, hw), BF16),
                   jax.ShapeDtypeStruct((b, nkv, vw, tk), BF16),
                   jax.ShapeDtypeStruct((b, nkv, 1, nh), F32),
                   jax.ShapeDtypeStruct((b, nkv, 1, nh), F32)],
        scratch_shapes=[pltpu.VMEM((1, nh), F32), pltpu.VMEM((1, nh), F32)],
        compiler_params=_params("arbitrary", "arbitrary"),
        name="fox_kv",
    )(x, wk_aug, wvt, wf, b_f, tri, spread, group)


def _fox_q_kernel(x_ref, wqt_ref, wg_ref, qt_ref, g_ref, qn_ref):
    xb = x_ref[0].astype(BF16)
    qt = _dot_nt(wqt_ref[...], xb) * (FOX_HEAD_DIM ** -0.5)
    qr = qt.astype(BF16).astype(F32)
    norm2 = jnp.sum((qr * qr).reshape(FOX_HEADS, FOX_SLOT, qt.shape[1]), axis=1)
    qn_ref[0, 0] = jnp.sqrt(jnp.max(norm2, axis=1, keepdims=True))
    slot_row = lax.broadcasted_iota(jnp.int32, qt.shape, 0) % FOX_SLOT
    is_bias_row = (slot_row >= FOX_HEAD_DIM) & (slot_row < FOX_HEAD_DIM + 3)
    qt_ref[0] = jnp.where(is_bias_row, 1.0, qt).astype(BF16)
    g_ref[0] = _dot(xb, wg_ref[...])


def _fox_q(x, wqt_aug, wg):
    b, s, d = x.shape
    tm = TOKEN_TILE
    hw = wqt_aug.shape[0]
    return pl.pallas_call(
        _fox_q_kernel,
        grid=(b, s // tm),
        in_specs=[pl.BlockSpec((1, tm, d), lambda bi, i: (bi, i, 0)), _resident(wqt_aug.shape),
                  _resident(wg.shape)],
        out_specs=[pl.BlockSpec((1, hw, tm), lambda bi, i: (bi, 0, i)),
                   pl.BlockSpec((1, tm, wg.shape[1]), lambda bi, i: (bi, i, 0)),
                   pl.BlockSpec((1, 1, FOX_HEADS, 1), lambda bi, i: (bi, i, 0, 0))],
        out_shape=[jax.ShapeDtypeStruct((b, hw, s), BF16), jax.ShapeDtypeStruct((b, s, wg.shape[1]), F32),
                   jax.ShapeDtypeStruct((b, s // tm, FOX_HEADS, 1), F32)],
        compiler_params=_params("parallel", "parallel"),
        name="fox_q",
    )(x, wqt_aug, wg)


FOX_HEADS_PER_STEP = 2


def _fox_attn_kernel(cblk_ref, kpm_ref, qn_ref, qt_ref, k_ref, vt_ref, g_ref, o_ref, acc_ref, *, nkv):
    bi = pl.program_id(0)
    hp = pl.program_id(1)
    i = pl.program_id(2)
    tq = qt_ref.shape[2]
    tk = k_ref.shape[2]
    dh = FOX_HEAD_DIM
    nhp = FOX_HEADS_PER_STEP

    def head_state(hh):
        return (bi * FOX_HEADS + hp * nhp + hh) * nkv

    def blocks_needed(hh):
        base = head_state(hh)
        thr = cblk_ref[base + i] + EXP_ZERO_GAP + 2.0 * NORM_SLACK * qn_ref[base + i] * kpm_ref[base + i]

        def count(j, n):
            return n + jnp.where(cblk_ref[base + j + 1] > thr, 0, 1)

        return lax.fori_loop(0, i, count, jnp.int32(0))

    def block(hh, j, masked):
        kj = k_ref[0, j, :, hh * FOX_SLOT:(hh + 1) * FOX_SLOT]
        s = _dot(kj, qt_ref[0, hh * FOX_SLOT:(hh + 1) * FOX_SLOT, :])
        if masked:
            key_pos = lax.broadcasted_iota(jnp.int32, (tk, tq), 0)
            qry_pos = lax.broadcasted_iota(jnp.int32, (tk, tq), 1)
            s = jnp.where(key_pos <= qry_pos, s, -jnp.inf)
        return s, cblk_ref[head_state(hh) + j], vt_ref[0, j, hh * dh:(hh + 1) * dh, :]

    carry = []
    for hh in range(nhp):
        s, cb, vj = block(hh, i, True)
        m = jnp.max(s, axis=0, keepdims=True) - cb
        p = jnp.exp(s - (m + cb))
        carry += [m, jnp.sum(p, axis=0, keepdims=True)]
        acc_ref[hh * dh:(hh + 1) * dh, :] = _dot(vj, p.astype(BF16))

    def step(jj, carry):
        j = i - 1 - jj
        out = []
        for hh in range(nhp):
            m, l = carry[2 * hh], carry[2 * hh + 1]
            s, cb, vj = block(hh, j, False)
            m_new = jnp.maximum(m, jnp.max(s, axis=0, keepdims=True) - cb)
            alpha = jnp.exp(m - m_new)
            p = jnp.exp(s - (m_new + cb))
            out += [m_new, alpha * l + jnp.sum(p, axis=0, keepdims=True)]
            rows = slice(hh * dh, (hh + 1) * dh)
            acc_ref[rows, :] = alpha * acc_ref[rows, :] + _dot(vj, p.astype(BF16))
        return tuple(out)

    n_blocks = blocks_needed(0)
    for hh in range(1, nhp):
        n_blocks = jnp.maximum(n_blocks, blocks_needed(hh))
    carry = lax.fori_loop(0, n_blocks, step, tuple(carry))

    inv = jnp.concatenate([jnp.broadcast_to(1.0 / carry[2 * hh + 1], (dh, tq)) for hh in range(nhp)], axis=0)
    o = jnp.transpose(acc_ref[...] * inv)
    o_ref[0] = (o * _sigmoid(g_ref[0])).astype(BF16)


def _fox_attn(cblk, kpm, qn, qt, k, vt, g):
    b, nkv, tk, hw = k.shape
    s = nkv * tk
    tq = FOX_BLOCK
    nhp = FOX_HEADS_PER_STEP
    out_w = nhp * FOX_HEAD_DIM
    return pl.pallas_call(
        functools.partial(_fox_attn_kernel, nkv=nkv),
        grid=(b, FOX_HEADS // nhp, s // tq),
        in_specs=[pl.BlockSpec(memory_space=pltpu.SMEM), pl.BlockSpec(memory_space=pltpu.SMEM),
                  pl.BlockSpec(memory_space=pltpu.SMEM),
                  pl.BlockSpec((1, nhp * FOX_SLOT, tq), lambda bi, hp, i: (bi, hp, i)),
                  pl.BlockSpec((1, nkv, tk, nhp * FOX_SLOT), lambda bi, hp, i: (bi, 0, 0, hp)),
                  pl.BlockSpec((1, nkv, out_w, tk), lambda bi, hp, i: (bi, 0, hp, 0)),
                  pl.BlockSpec((1, tq, out_w), lambda bi, hp, i: (bi, i, hp))],
        out_specs=pl.BlockSpec((1, tq, out_w), lambda bi, hp, i: (bi, i, hp)),
        out_shape=jax.ShapeDtypeStruct((b, s, FOX_HEADS * FOX_HEAD_DIM), BF16),
        scratch_shapes=[pltpu.VMEM((out_w, tq), F32)],
        compiler_params=_params("arbitrary", "arbitrary", "arbitrary"),
        name="fox_attn",
    )(cblk, kpm, qn, qt, k, vt, g)


def _block_tri(n, block):
    r = jnp.arange(n)
    return ((r[:, None] >= r[None, :]) & (r[:, None] // block == r[None, :] // block)).astype(BF16)


def _spread_heads(w):
    d = w.shape[0]
    w = w.reshape(d, FOX_HEADS, FOX_HEAD_DIM)
    w = jnp.pad(w, ((0, 0), (0, 0), (0, FOX_SLOT - FOX_HEAD_DIM)))
    return w.reshape(d, FOX_HEADS * FOX_SLOT)


def _head_group_matrix():
    return (jnp.arange(FOX_HEADS * FOX_SLOT)[:, None] // FOX_SLOT == jnp.arange(FOX_HEADS)[None, :]).astype(BF16)


def _bias_spread_matrix():
    rows = jnp.arange(3 * FOX_HEADS)
    term, head = rows // FOX_HEADS, rows % FOX_HEADS
    cols = head * FOX_SLOT + FOX_HEAD_DIM + term
    return (jnp.arange(FOX_HEADS * FOX_SLOT)[None, :] == cols[:, None]).astype(BF16)


def kernel(x, p, ffn1_w_in, ffn1_w_out, ln1_g, ln1_b, gla_w_in, gla_w_a2, gla_b_a, gla_gn_g, gla_gn_b, gla_w_o, fox_w_kvf, fox_b_f, fox_w_in, fox_w_o, ln2_g, ln2_b, ffn2_w_in, ffn2_w_out, ple_w_gate, ple_w_proj, ln3_g, ln3_b):
    bsz, seq, d = x.shape
    depth = ffn1_w_in.shape[0]
    n_gla = gla_w_in.shape[0]
    alpha = (2 * depth) ** 0.25
    t = bsz * seq
    fox_w = FOX_HEADS * FOX_HEAD_DIM
    row = lambda a: a.reshape(1, -1)

    tri_chunk = _block_tri(TRI_BLOCK, GLA_CHUNK)
    tri_block = _block_tri(FOX_BLOCK, FOX_BLOCK)
    bias_spread = _bias_spread_matrix()

    xf = x.reshape(t, d)
    pf = p.reshape(depth, t, p.shape[-1])
    per_head = lambda a: jnp.transpose(a[:, :, 0, :], (0, 2, 1)).reshape(-1)
    fox_k = fox_vt = fox_cblk = fox_kpm = None
    for i in range(depth):
        if i == n_gla:
            wk_aug = _spread_heads(fox_w_kvf[:, :fox_w]).astype(BF16)
            wvt = fox_w_kvf[:, fox_w:2 * fox_w].T.astype(BF16)
            wf = fox_w_kvf[:, 2 * fox_w:].astype(BF16)
            fox_k, fox_vt, cblk, kpm = _fox_kv(xf.reshape(bsz, seq, d), wk_aug, wvt, wf, row(fox_b_f),
                                               tri_block, bias_spread, _head_group_matrix())
            fox_cblk, fox_kpm = per_head(cblk), per_head(kpm)
        xf = _ffn(xf, ffn1_w_in[i].astype(BF16), ffn1_w_out[i].astype(BF16), row(ln1_g[i]), row(ln1_b[i]), alpha)
        if i < n_gla:
            qd, ki, ks, v, dec, gate = _gla_proj(xf, gla_w_in[i].astype(BF16), gla_w_a2[i].astype(BF16),
                                                 row(gla_b_a[i]), tri_chunk)
            b3 = lambda a: a.reshape(bsz, seq, -1)
            mix = _gla_rec(b3(qd), b3(ki), b3(ks), b3(v), dec.reshape(bsz, seq // GLA_CHUNK, 1, -1), b3(gate),
                           row(gla_gn_g[i]), row(gla_gn_b[i])).reshape(t, -1)
            w_o = gla_w_o[i]
        else:
            j = i - n_gla
            wqt_aug = _spread_heads(fox_w_in[j][:, :fox_w]).T.astype(BF16)
            qt, g, qn = _fox_q(xf.reshape(bsz, seq, d), wqt_aug, fox_w_in[j][:, fox_w:].astype(BF16))
            mix = _fox_attn(fox_cblk, fox_kpm, jnp.transpose(qn[:, :, :, 0], (0, 2, 1)).reshape(-1),
                            qt, fox_k, fox_vt, g).reshape(t, -1)
            w_o = fox_w_o[j]
        xf = _out_proj(mix, xf, w_o.astype(BF16), row(ln2_g[i]), row(ln2_b[i]), alpha)
        xf = _ffn(xf, ffn2_w_in[i].astype(BF16), ffn2_w_out[i].astype(BF16), row(ln3_g[i]), row(ln3_b[i]), alpha,
                  ple=(pf[i], ple_w_gate[i].astype(BF16), ple_w_proj[i].astype(BF16)))
    return xf.reshape(bsz, seq, d)
```

```python
import functools
import math

import jax
import jax.numpy as jnp
from jax import lax
from jax.experimental import pallas as pl
from jax.experimental.pallas import tpu as pltpu

F32 = jnp.float32
BF16 = jnp.bfloat16

GLA_HEADS = 4
GLA_TAU = 16.0
GLA_CHUNK = 64
FOX_HEADS = 16
FOX_HEAD_DIM = 64
LN_EPS = 1e-5

V7X_VMEM_LIMIT_BYTES = 56 * 1024 * 1024

TOKEN_TILE = 512
FFN_TOKEN_TILE = 256
FOX_BLOCK = 512
FOX_SLOT = 128
FOX_HEADS_PER_STEP = 2
TRI_BLOCK = 256
LOG2E = math.log2(math.e)
EXP2_ZERO_GAP = 152.0
NORM_SLACK = 1.01


def _params(*sems):
    return pltpu.CompilerParams(dimension_semantics=sems, vmem_limit_bytes=V7X_VMEM_LIMIT_BYTES)


def _resident(shape):
    zeros = (0,) * len(shape)
    return pl.BlockSpec(shape, lambda *_: zeros, pipeline_mode=pl.Buffered(1))


def _dot(a, b):
    return jnp.dot(a, b, preferred_element_type=F32)


def _dot_nt(a, b):
    return lax.dot_general(a, b, (((1,), (1,)), ((), ())), preferred_element_type=F32)


def _split3(a):
    hi = a.astype(BF16)
    r1 = a - hi.astype(F32)
    mid = r1.astype(BF16)
    lo = (r1 - mid.astype(F32)).astype(BF16)
    return hi, mid, lo


def _tri_cumsum(tri, a):
    hi, mid, lo = _split3(a)
    return _dot(tri, hi) + _dot(tri, mid) + _dot(tri, lo)


def _log_sigmoid(z):
    return jnp.minimum(z, 0.0) - jnp.log1p(jnp.exp(-jnp.abs(z)))


def _sigmoid(z):
    return 1.0 / (1.0 + jnp.exp(-z))


def _layer_norm(z, g, b):
    mu = jnp.mean(z, axis=-1, keepdims=True)
    zc = z - mu
    var = jnp.mean(zc * zc, axis=-1, keepdims=True)
    return zc * lax.rsqrt(var + LN_EPS) * g + b


def _ffn_kernel(*refs, alpha, d_ff, has_ple):
    if has_ple:
        x_ref, win_ref, wout_ref, g_ref, b_ref, p_ref, wgate_ref, wproj_ref, o_ref = refs
    else:
        x_ref, win_ref, wout_ref, g_ref, b_ref, o_ref = refs
    x = x_ref[...]
    xb = x.astype(BF16)
    gate = _dot(xb, win_ref[:, :d_ff])
    up = _dot(xb, win_ref[:, d_ff:])
    h = (gate * _sigmoid(gate) * up).astype(BF16)
    z = alpha * x + 0.5 * _dot(h, wout_ref[...])
    if has_ple:
        ple_gate = _sigmoid(_dot(xb, wgate_ref[...]))
        z = z + ple_gate * _dot(p_ref[...].astype(BF16), wproj_ref[...])
    o_ref[...] = _layer_norm(z, g_ref[...], b_ref[...])


def _ffn(x, w_in, w_out, ln_g, ln_b, alpha, ple=None):
    t, d = x.shape
    d_ff = w_out.shape[0]
    tm = FFN_TOKEN_TILE
    row = lambda i: (i, 0)
    in_specs = [pl.BlockSpec((tm, d), row), _resident(w_in.shape), _resident(w_out.shape),
                _resident(ln_g.shape), _resident(ln_b.shape)]
    args = [x, w_in, w_out, ln_g, ln_b]
    if ple is not None:
        p, w_gate, w_proj = ple
        in_specs += [pl.BlockSpec((tm, p.shape[1]), row), _resident(w_gate.shape), _resident(w_proj.shape)]
        args += [p, w_gate, w_proj]
    return pl.pallas_call(
        functools.partial(_ffn_kernel, alpha=alpha, d_ff=d_ff, has_ple=ple is not None),
        grid=(t // tm,),
        in_specs=in_specs,
        out_specs=pl.BlockSpec((tm, d), row),
        out_shape=jax.ShapeDtypeStruct((t, d), F32),
        compiler_params=_params("parallel"),
        name="ffn_ple" if ple is not None else "ffn",
    )(*args)


def _out_proj_kernel(y_ref, x_ref, w_ref, g_ref, b_ref, o_ref, *, alpha):
    z = alpha * x_ref[...] + _dot(y_ref[...], w_ref[...])
    o_ref[...] = _layer_norm(z, g_ref[...], b_ref[...])


def _out_proj(y, x, w_o, ln_g, ln_b, alpha):
    t, d = x.shape
    tm = TOKEN_TILE
    row = lambda i: (i, 0)
    return pl.pallas_call(
        functools.partial(_out_proj_kernel, alpha=alpha),
        grid=(t // tm,),
        in_specs=[pl.BlockSpec((tm, y.shape[1]), row), pl.BlockSpec((tm, d), row), _resident(w_o.shape),
                  _resident(ln_g.shape), _resident(ln_b.shape)],
        out_specs=pl.BlockSpec((tm, d), row),
        out_shape=jax.ShapeDtypeStruct((t, d), F32),
        compiler_params=_params("parallel"),
        name="out_proj",
    )(y, x, w_o, ln_g, ln_b)


def _gla_proj_kernel(x_ref, win_ref, wa2_ref, ba_ref, tri_ref,
                     qd_ref, ki_ref, ks_ref, v_ref, dec_ref, gate_ref, *, kd, vd):
    tm = x_ref.shape[0]
    xb = x_ref[...].astype(BF16)
    proj = _dot(xb, win_ref[...])
    q = proj[:, :kd] * ((kd // GLA_HEADS) ** -0.5)
    k = proj[:, kd:2 * kd]
    v = proj[:, 2 * kd:2 * kd + vd]
    r = proj[:, 2 * kd + vd:2 * kd + 2 * vd]
    a_lr = proj[:, 2 * kd + 2 * vd:]
    log_a = _log_sigmoid(_dot(a_lr.astype(BF16), wa2_ref[...]) + ba_ref[...]) * (1.0 / GLA_TAU)
    tri = tri_ref[...]
    bcum = jnp.concatenate(
        [_tri_cumsum(tri, log_a[s:s + TRI_BLOCK]) for s in range(0, tm, TRI_BLOCK)], axis=0)
    nc = tm // GLA_CHUNK
    b3 = bcum.reshape(nc, GLA_CHUNK, kd)
    b_last = b3[:, GLA_CHUNK - 1:GLA_CHUNK, :]
    qd_ref[...] = (q * jnp.exp(bcum)).astype(BF16)
    ki_ref[...] = (k * jnp.exp(-bcum)).astype(BF16)
    ks_ref[...] = (k.reshape(nc, GLA_CHUNK, kd) * jnp.exp(b_last - b3)).reshape(tm, kd).astype(BF16)
    dec_ref[...] = jnp.exp(b_last)
    v_ref[...] = v.astype(BF16)
    gate_ref[...] = r * _sigmoid(r)


def _gla_proj(x, w_in, w_a2, b_a, tri):
    t, d = x.shape
    kd = w_a2.shape[1]
    vd = (w_in.shape[1] - 2 * kd - w_a2.shape[0]) // 2
    tm = TOKEN_TILE
    row = lambda i: (i, 0)
    return pl.pallas_call(
        functools.partial(_gla_proj_kernel, kd=kd, vd=vd),
        grid=(t // tm,),
        in_specs=[pl.BlockSpec((tm, d), row), _resident(w_in.shape), _resident(w_a2.shape),
                  _resident(b_a.shape), _resident(tri.shape)],
        out_specs=[pl.BlockSpec((tm, kd), row), pl.BlockSpec((tm, kd), row), pl.BlockSpec((tm, kd), row),
                   pl.BlockSpec((tm, vd), row), pl.BlockSpec((tm // GLA_CHUNK, 1, kd), lambda i: (i, 0, 0)),
                   pl.BlockSpec((tm, vd), row)],
        out_shape=[jax.ShapeDtypeStruct((t, kd), BF16)] * 3 + [
            jax.ShapeDtypeStruct((t, vd), BF16),
            jax.ShapeDtypeStruct((t // GLA_CHUNK, 1, kd), F32),
            jax.ShapeDtypeStruct((t, vd), F32)],
        compiler_params=_params("parallel"),
        name="gla_proj",
    )(x, w_in, w_a2, b_a, tri)


def _gla_rec_kernel(qd_ref, ki_ref, ks_ref, v_ref, dec_ref, gate_ref, gng_ref, gnb_ref, o_ref, st_ref):
    tm = qd_ref.shape[1]
    dk = qd_ref.shape[2] // GLA_HEADS
    dv = v_ref.shape[2] // GLA_HEADS

    @pl.when(pl.program_id(1) == 0)
    def _():
        st_ref[...] = jnp.zeros_like(st_ref)

    rows = lax.broadcasted_iota(jnp.int32, (GLA_CHUNK, GLA_CHUNK), 0)
    cols = lax.broadcasted_iota(jnp.int32, (GLA_CHUNK, GLA_CHUNK), 1)
    causal = cols <= rows

    def chunk(c, carry):
        r0 = pl.multiple_of(c * GLA_CHUNK, GLA_CHUNK)
        rs = pl.ds(r0, GLA_CHUNK)
        for h in range(GLA_HEADS):
            ksl = slice(h * dk, (h + 1) * dk)
            vsl = slice(h * dv, (h + 1) * dv)
            qd = qd_ref[0, rs, ksl]
            ki = ki_ref[0, rs, ksl]
            ks = ks_ref[0, rs, ksl]
            vv = v_ref[0, rs, vsl]
            attn = jnp.where(causal, _dot_nt(qd, ki), 0.0).astype(BF16)
            st = st_ref[h]
            o = _dot(attn, vv) + _dot_nt(qd, st.astype(BF16))
            vt = jnp.transpose(vv.astype(F32)).astype(BF16)
            st_ref[h] = st * dec_ref[0, c, :, ksl] + _dot(vt, ks)
            on = _layer_norm(o, gng_ref[:, vsl], gnb_ref[:, vsl])
            o_ref[0, rs, vsl] = (on * gate_ref[0, rs, vsl]).astype(BF16)
        return carry

    lax.fori_loop(0, tm // GLA_CHUNK, chunk, 0)


def _gla_rec(qd, ki, ks, v, dec, gate, gn_g, gn_b):
    b, s, kd = qd.shape
    vd = v.shape[2]
    tm = TOKEN_TILE
    blk = lambda bi, i: (bi, i, 0)
    return pl.pallas_call(
        _gla_rec_kernel,
        grid=(b, s // tm),
        in_specs=[pl.BlockSpec((1, tm, kd), blk), pl.BlockSpec((1, tm, kd), blk), pl.BlockSpec((1, tm, kd), blk),
                  pl.BlockSpec((1, tm, vd), blk),
                  pl.BlockSpec((1, tm // GLA_CHUNK, 1, kd), lambda bi, i: (bi, i, 0, 0)),
                  pl.BlockSpec((1, tm, vd), blk), _resident(gn_g.shape), _resident(gn_b.shape)],
        out_specs=pl.BlockSpec((1, tm, vd), blk),
        out_shape=jax.ShapeDtypeStruct((b, s, vd), BF16),
        scratch_shapes=[pltpu.VMEM((GLA_HEADS, vd // GLA_HEADS, kd // GLA_HEADS), F32)],
        compiler_params=_params("arbitrary", "arbitrary"),
        name="gla_rec",
    )(qd, ki, ks, v, dec, gate, gn_g, gn_b)


def _fox_kv_kernel(x_ref, wk_ref, wvt_ref, wf_ref, bf_ref, tri_ref, spread_ref, group_ref,
                   k_ref, vt_ref, cblk_ref, cend_ref, kpm_ref, carry_ref, kmax_ref):
    @pl.when(pl.program_id(1) == 0)
    def _():
        carry_ref[...] = jnp.zeros_like(carry_ref)
        kmax_ref[...] = jnp.zeros_like(kmax_ref)

    tk = x_ref.shape[1]
    xb = x_ref[0].astype(BF16)
    log_f = _log_sigmoid(_dot(xb, wf_ref[...]) + bf_ref[...])
    c_rel = _tri_cumsum(tri_ref[...], log_f) * LOG2E
    hi, mid, lo = _split3(-c_rel)
    bias = _dot(jnp.concatenate([hi, mid, lo], axis=1), spread_ref[...])
    k = _dot(xb, wk_ref[...])
    k_aug = (k + bias).astype(BF16)
    for h in range(FOX_HEADS):
        k_ref[0, 0, h] = k_aug[:, h * FOX_SLOT:(h + 1) * FOX_SLOT]
    vt = _dot_nt(wvt_ref[...], xb).astype(BF16)
    vt_ref[0, 0] = vt.reshape(FOX_HEADS, FOX_HEAD_DIM, tk)
    cblk_ref[0, 0] = carry_ref[...]
    carry_ref[...] = carry_ref[...] + c_rel[tk - 1:tk, :]
    cend_ref[0, 0] = carry_ref[...]
    kr = k.astype(BF16).astype(F32)
    norm2 = _dot((kr * kr).astype(BF16), group_ref[...]) * NORM_SLACK
    kmax_ref[...] = jnp.maximum(kmax_ref[...], jnp.sqrt(jnp.max(norm2, axis=0, keepdims=True)))
    kpm_ref[0, 0] = kmax_ref[...]


def _fox_kv(x, wk_aug, wvt, wf, b_f, tri, spread, group):
    b, s, d = x.shape
    tk = FOX_BLOCK
    nkv = s // tk
    nh = wf.shape[1]
    blk4 = lambda bi, j: (bi, j, 0, 0)
    blk5 = lambda bi, j: (bi, j, 0, 0, 0)
    small = jax.ShapeDtypeStruct((b, nkv, 1, nh), F32)
    return pl.pallas_call(
        _fox_kv_kernel,
        grid=(b, nkv),
        in_specs=[pl.BlockSpec((1, tk, d), lambda bi, j: (bi, j, 0)), _resident(wk_aug.shape),
                  _resident(wvt.shape), _resident(wf.shape), _resident(b_f.shape), _resident(tri.shape),
                  _resident(spread.shape), _resident(group.shape)],
        out_specs=[pl.BlockSpec((1, 1, nh, tk, FOX_SLOT), blk5),
                   pl.BlockSpec((1, 1, nh, FOX_HEAD_DIM, tk), blk5),
                   pl.BlockSpec((1, 1, 1, nh), blk4), pl.BlockSpec((1, 1, 1, nh), blk4),
                   pl.BlockSpec((1, 1, 1, nh), blk4)],
        out_shape=[jax.ShapeDtypeStruct((b, nkv, nh, tk, FOX_SLOT), BF16),
                   jax.ShapeDtypeStruct((b, nkv, nh, FOX_HEAD_DIM, tk), BF16),
                   small, small, small],
        scratch_shapes=[pltpu.VMEM((1, nh), F32), pltpu.VMEM((1, nh), F32)],
        compiler_params=_params("arbitrary", "arbitrary"),
        name="fox_kv",
    )(x, wk_aug, wvt, wf, b_f, tri, spread, group)


def _fox_q_kernel(x_ref, wqt_ref, wg_ref, cblk_ref, cend_ref, kpm_ref, qt_ref, g_ref, need_ref):
    i = pl.program_id(1)
    tq = x_ref.shape[1]
    xb = x_ref[0].astype(BF16)
    qt = _dot_nt(wqt_ref[...], xb) * (FOX_HEAD_DIM ** -0.5 * LOG2E)
    qt3 = qt.reshape(FOX_HEADS, FOX_SLOT, tq)
    slot_row = lax.broadcasted_iota(jnp.int32, qt3.shape, 1)
    is_bias_row = (slot_row >= FOX_HEAD_DIM) & (slot_row < FOX_HEAD_DIM + 3)
    qt_ref[0] = jnp.where(is_bias_row, 1.0, qt3).astype(BF16)
    g_ref[0] = _dot(xb, wg_ref[...])

    qr = qt.astype(BF16).astype(F32).reshape(FOX_HEADS, FOX_SLOT, tq)
    qn = jnp.sqrt(jnp.max(jnp.sum(qr * qr, axis=1), axis=1, keepdims=True))
    blk = lax.broadcasted_iota(jnp.int32, cblk_ref.shape[1:], 1)
    at_i = blk == i
    c_before = jnp.sum(jnp.where(at_i, cblk_ref[0], 0.0), axis=1, keepdims=True)
    k_norm = jnp.sum(jnp.where(at_i, kpm_ref[0], 0.0), axis=1, keepdims=True)
    thr = c_before + EXP2_ZERO_GAP + 2.0 * NORM_SLACK * qn * k_norm
    needed = (blk < i) & jnp.logical_not(cend_ref[0] > thr)
    need_ref[0, 0] = jnp.sum(jnp.where(needed, 1.0, 0.0), axis=1, keepdims=True).astype(jnp.int32)


def _fox_q(x, wqt_aug, wg, cblk_t, cend_t, kpm_t):
    b, s, d = x.shape
    tm = FOX_BLOCK
    nkv = cblk_t.shape[2]
    per_batch = pl.BlockSpec((1, FOX_HEADS, nkv), lambda bi, i: (bi, 0, 0))
    return pl.pallas_call(
        _fox_q_kernel,
        grid=(b, s // tm),
        in_specs=[pl.BlockSpec((1, tm, d), lambda bi, i: (bi, i, 0)), _resident(wqt_aug.shape),
                  _resident(wg.shape), per_batch, per_batch, per_batch],
        out_specs=[pl.BlockSpec((1, FOX_HEADS, FOX_SLOT, tm), lambda bi, i: (bi, 0, 0, i)),
                   pl.BlockSpec((1, tm, wg.shape[1]), lambda bi, i: (bi, i, 0)),
                   pl.BlockSpec((1, 1, FOX_HEADS, 1), lambda bi, i: (bi, i, 0, 0))],
        out_shape=[jax.ShapeDtypeStruct((b, FOX_HEADS, FOX_SLOT, s), BF16),
                   jax.ShapeDtypeStruct((b, s, wg.shape[1]), F32),
                   jax.ShapeDtypeStruct((b, s // tm, FOX_HEADS, 1), jnp.int32)],
        compiler_params=_params("parallel", "parallel"),
        name="fox_q",
    )(x, wqt_aug, wg, cblk_t, cend_t, kpm_t)


def _fox_attn_kernel(cblk_ref, need_ref, qt_ref, k_ref, vt_ref, g_ref, o_ref,
                     acc_ref, m_ref, l_ref, s0_ref, s1_ref, cmax0_ref, cmax1_ref,
                     p0_ref, p1_ref, alpha0_ref, alpha1_ref, *, nkv):
    bi = pl.program_id(0)
    hp = pl.program_id(1)
    i = pl.program_id(2)
    tq = qt_ref.shape[3]
    tk = k_ref.shape[3]
    nhp = FOX_HEADS_PER_STEP
    assert nhp == 2

    def head_base(hh):
        return (bi * FOX_HEADS + hp * nhp + hh) * nkv

    key_pos = lax.broadcasted_iota(jnp.int32, (tk, tq), 0)
    qry_pos = lax.broadcasted_iota(jnp.int32, (tk, tq), 1)
    for hh in range(nhp):
        s = jnp.where(key_pos <= qry_pos, _dot(k_ref[0, i, hh], qt_ref[0, hh]), -jnp.inf)
        cb = cblk_ref[head_base(hh) + i]
        m = jnp.max(s, axis=0, keepdims=True) - cb
        p = jnp.exp2(s - (m + cb))
        m_ref[hh] = m
        l_ref[hh] = jnp.sum(p, axis=0, keepdims=True)
        acc_ref[hh] = _dot(vt_ref[0, i, hh], p.astype(BF16))

    n0 = need_ref[head_base(0) + i]
    n1 = need_ref[head_base(1) + i]
    odd = (n0 + n1) % 2
    grow0 = odd * (n0 < i).astype(jnp.int32)
    n0 = n0 + grow0
    n1 = n1 + odd - grow0
    total = n0 + n1

    def item(e):
        hh = (e >= n0).astype(jnp.int32)
        return hh, i - 1 - (e - hh * n0)

    def scores(e, s_ref, cmax_ref):
        hh, j = item(e)
        s = _dot(k_ref[0, j, hh], qt_ref[0, hh])
        s_ref[...] = s
        cmax_ref[...] = jnp.max(s, axis=0, keepdims=True) - cblk_ref[head_base(hh) + j]

    def softmax(e, s_ref, cmax_ref, p_ref, alpha_ref):
        hh, j = item(e)
        m = m_ref[hh]
        m_new = jnp.maximum(m, cmax_ref[...])
        alpha = jnp.exp2(m - m_new)
        p = jnp.exp2(s_ref[...] - (m_new + cblk_ref[head_base(hh) + j]))
        m_ref[hh] = m_new
        l_ref[hh] = alpha * l_ref[hh] + jnp.sum(p, axis=0, keepdims=True)
        p_ref[...] = p.astype(BF16)
        alpha_ref[...] = alpha

    def accumulate(e, p_ref, alpha_ref):
        hh, j = item(e)
        acc_ref[hh] = alpha_ref[...] * acc_ref[hh] + _dot(vt_ref[0, j, hh], p_ref[...])

    even = (s0_ref, cmax0_ref, p0_ref, alpha0_ref)
    odd_ = (s1_ref, cmax1_ref, p1_ref, alpha1_ref)

    @pl.when(total > 0)
    def _():
        scores(0, *even[:2])
        scores(1, *odd_[:2])
        softmax(0, *even)

        def pair(t, carry):
            e = 2 * t
            scores(e, *even[:2])
            softmax(e - 1, *odd_)
            accumulate(e - 2, *even[2:])
            scores(e + 1, *odd_[:2])
            softmax(e, *even)
            accumulate(e - 1, *odd_[2:])
            return carry

        lax.fori_loop(1, total // 2, pair, 0)
        softmax(total - 1, *odd_)
        accumulate(total - 2, *even[2:])
        accumulate(total - 1, *odd_[2:])

    inv = jnp.concatenate([jnp.broadcast_to(1.0 / l_ref[hh], acc_ref.shape[1:]) for hh in range(nhp)], axis=0)
    o = jnp.transpose(acc_ref[...].reshape(inv.shape) * inv)
    o_ref[0] = (o * _sigmoid(g_ref[0])).astype(BF16)


def _fox_attn(cblk, need, qt, k, vt, g):
    b, nkv, nh, tk, slot = k.shape
    dh = vt.shape[3]
    s = nkv * tk
    tq = FOX_BLOCK
    nhp = FOX_HEADS_PER_STEP
    out_w = nhp * dh
    vec = pltpu.VMEM((1, tq), F32)
    return pl.pallas_call(
        functools.partial(_fox_attn_kernel, nkv=nkv),
        grid=(b, nh // nhp, s // tq),
        in_specs=[pl.BlockSpec(memory_space=pltpu.SMEM), pl.BlockSpec(memory_space=pltpu.SMEM),
                  pl.BlockSpec((1, nhp, slot, tq), lambda bi, hp, i: (bi, hp, 0, i)),
                  pl.BlockSpec((1, nkv, nhp, tk, slot), lambda bi, hp, i: (bi, 0, hp, 0, 0)),
                  pl.BlockSpec((1, nkv, nhp, dh, tk), lambda bi, hp, i: (bi, 0, hp, 0, 0)),
                  pl.BlockSpec((1, tq, out_w), lambda bi, hp, i: (bi, i, hp))],
        out_specs=pl.BlockSpec((1, tq, out_w), lambda bi, hp, i: (bi, i, hp)),
        out_shape=jax.ShapeDtypeStruct((b, s, nh * dh), BF16),
        scratch_shapes=[pltpu.VMEM((nhp, dh, tq), F32), pltpu.VMEM((nhp, 1, tq), F32),
                        pltpu.VMEM((nhp, 1, tq), F32),
                        pltpu.VMEM((tk, tq), F32), pltpu.VMEM((tk, tq), F32), vec, vec,
                        pltpu.VMEM((tk, tq), BF16), pltpu.VMEM((tk, tq), BF16), vec, vec],
        compiler_params=_params("arbitrary", "arbitrary", "arbitrary"),
        name="fox_attn",
    )(cblk, need, qt, k, vt, g)


def _block_tri(n, block):
    r = jnp.arange(n)
    return ((r[:, None] >= r[None, :]) & (r[:, None] // block == r[None, :] // block)).astype(BF16)


def _spread_heads(w):
    d = w.shape[0]
    w = w.reshape(d, FOX_HEADS, FOX_HEAD_DIM)
    w = jnp.pad(w, ((0, 0), (0, 0), (0, FOX_SLOT - FOX_HEAD_DIM)))
    return w.reshape(d, FOX_HEADS * FOX_SLOT)


def _head_group_matrix():
    return (jnp.arange(FOX_HEADS * FOX_SLOT)[:, None] // FOX_SLOT == jnp.arange(FOX_HEADS)[None, :]).astype(BF16)


def _bias_spread_matrix():
    rows = jnp.arange(3 * FOX_HEADS)
    term, head = rows // FOX_HEADS, rows % FOX_HEADS
    cols = head * FOX_SLOT + FOX_HEAD_DIM + term
    return (jnp.arange(FOX_HEADS * FOX_SLOT)[None, :] == cols[:, None]).astype(BF16)


def kernel(x, p, ffn1_w_in, ffn1_w_out, ln1_g, ln1_b, gla_w_in, gla_w_a2, gla_b_a, gla_gn_g, gla_gn_b, gla_w_o, fox_w_kvf, fox_b_f, fox_w_in, fox_w_o, ln2_g, ln2_b, ffn2_w_in, ffn2_w_out, ple_w_gate, ple_w_proj, ln3_g, ln3_b):
    bsz, seq, d = x.shape
    depth = ffn1_w_in.shape[0]
    n_gla = gla_w_in.shape[0]
    alpha = (2 * depth) ** 0.25
    t = bsz * seq
    fox_w = FOX_HEADS * FOX_HEAD_DIM
    row = lambda a: a.reshape(1, -1)

    tri_chunk = _block_tri(TRI_BLOCK, GLA_CHUNK)
    tri_block = _block_tri(FOX_BLOCK, FOX_BLOCK)
    bias_spread = _bias_spread_matrix()

    xf = x.reshape(t, d)
    pf = p.reshape(depth, t, p.shape[-1])
    heads_major = lambda a: jnp.transpose(a[:, :, 0, :], (0, 2, 1))
    fox_k = fox_vt = cblk_t = cend_t = kpm_t = None
    for i in range(depth):
        if i == n_gla:
            wk_aug = _spread_heads(fox_w_kvf[:, :fox_w]).astype(BF16)
            wvt = fox_w_kvf[:, fox_w:2 * fox_w].T.astype(BF16)
            wf = fox_w_kvf[:, 2 * fox_w:].astype(BF16)
            fox_k, fox_vt, cblk, cend, kpm = _fox_kv(xf.reshape(bsz, seq, d), wk_aug, wvt, wf, row(fox_b_f),
                                                     tri_block, bias_spread, _head_group_matrix())
            cblk_t, cend_t, kpm_t = heads_major(cblk), heads_major(cend), heads_major(kpm)
        xf = _ffn(xf, ffn1_w_in[i].astype(BF16), ffn1_w_out[i].astype(BF16), row(ln1_g[i]), row(ln1_b[i]), alpha)
        if i < n_gla:
            qd, ki, ks, v, dec, gate = _gla_proj(xf, gla_w_in[i].astype(BF16), gla_w_a2[i].astype(BF16),
                                                 row(gla_b_a[i]), tri_chunk)
            b3 = lambda a: a.reshape(bsz, seq, -1)
            mix = _gla_rec(b3(qd), b3(ki), b3(ks), b3(v), dec.reshape(bsz, seq // GLA_CHUNK, 1, -1), b3(gate),
                           row(gla_gn_g[i]), row(gla_gn_b[i])).reshape(t, -1)
            w_o = gla_w_o[i]
        else:
            j = i - n_gla
            wqt_aug = _spread_heads(fox_w_in[j][:, :fox_w]).T.astype(BF16)
            qt, g, need = _fox_q(xf.reshape(bsz, seq, d), wqt_aug, fox_w_in[j][:, fox_w:].astype(BF16),
                                 cblk_t, cend_t, kpm_t)
            need = jnp.transpose(need[:, :, :, 0], (0, 2, 1)).reshape(-1)
            mix = _fox_attn(cblk_t.reshape(-1), need, qt, fox_k, fox_vt, g).reshape(t, -1)
            w_o = fox_w_o[j]
        xf = _out_proj(mix, xf, w_o.astype(BF16), row(ln2_g[i]), row(ln2_b[i]), alpha)
        xf = _ffn(xf, ffn2_w_in[i].astype(BF16), ffn2_w_out[i].astype(BF16), row(ln3_g[i]), row(ln3_b[i]), alpha,
                  ple=(pf[i], ple_w_gate[i].astype(BF16), ple_w_proj[i].astype(BF16)))
    return xf.reshape(bsz, seq, d)
```

```python
import functools
import math

import jax
import jax.numpy as jnp
from jax import lax
from jax.experimental import pallas as pl
from jax.experimental.pallas import tpu as pltpu

F32 = jnp.float32
BF16 = jnp.bfloat16

GLA_HEADS = 4
GLA_TAU = 16.0
GLA_CHUNK = 64
FOX_HEADS = 16
FOX_HEAD_DIM = 64
LN_EPS = 1e-5

V7X_VMEM_LIMIT_BYTES = 56 * 1024 * 1024

TOKEN_TILE = 512
FFN_TOKEN_TILE = 256
FOX_BLOCK = 512
FOX_SLOT = 128
FOX_HEADS_PER_STEP = 2
TRI_BLOCK = 256
LOG2E = math.log2(math.e)
EXP2_ZERO_GAP = 152.0
NORM_SLACK = 1.01
FOX_V_ROWS = 80
FOX_ATTN_UNROLL = 2
MAX_FIXED_SHIFT = 50.0


def _params(*sems):
    return pltpu.CompilerParams(dimension_semantics=sems, vmem_limit_bytes=V7X_VMEM_LIMIT_BYTES)


def _resident(shape):
    zeros = (0,) * len(shape)
    return pl.BlockSpec(shape, lambda *_: zeros, pipeline_mode=pl.Buffered(1))


def _dot(a, b):
    return jnp.dot(a, b, preferred_element_type=F32)


def _dot_nt(a, b):
    return lax.dot_general(a, b, (((1,), (1,)), ((), ())), preferred_element_type=F32)


def _split3(a):
    hi = a.astype(BF16)
    r1 = a - hi.astype(F32)
    mid = r1.astype(BF16)
    lo = (r1 - mid.astype(F32)).astype(BF16)
    return hi, mid, lo


def _tri_cumsum(tri, a):
    hi, mid, lo = _split3(a)
    return _dot(tri, hi) + _dot(tri, mid) + _dot(tri, lo)


def _log_sigmoid(z):
    return jnp.minimum(z, 0.0) - jnp.log1p(jnp.exp(-jnp.abs(z)))


def _sigmoid(z):
    return 1.0 / (1.0 + jnp.exp(-z))


def _layer_norm(z, g, b):
    mu = jnp.mean(z, axis=-1, keepdims=True)
    zc = z - mu
    var = jnp.mean(zc * zc, axis=-1, keepdims=True)
    return zc * lax.rsqrt(var + LN_EPS) * g + b


def _ffn_kernel(*refs, alpha, d_ff, has_ple):
    if has_ple:
        x_ref, win_ref, wout_ref, g_ref, b_ref, p_ref, wgate_ref, wproj_ref, o_ref = refs
    else:
        x_ref, win_ref, wout_ref, g_ref, b_ref, o_ref = refs
    x = x_ref[...]
    xb = x.astype(BF16)
    gate = _dot(xb, win_ref[:, :d_ff])
    up = _dot(xb, win_ref[:, d_ff:])
    h = (gate * _sigmoid(gate) * up).astype(BF16)
    z = alpha * x + 0.5 * _dot(h, wout_ref[...])
    if has_ple:
        ple_gate = _sigmoid(_dot(xb, wgate_ref[...]))
        z = z + ple_gate * _dot(p_ref[...].astype(BF16), wproj_ref[...])
    o_ref[...] = _layer_norm(z, g_ref[...], b_ref[...])


def _ffn(x, w_in, w_out, ln_g, ln_b, alpha, ple=None):
    t, d = x.shape
    d_ff = w_out.shape[0]
    tm = FFN_TOKEN_TILE
    row = lambda i: (i, 0)
    in_specs = [pl.BlockSpec((tm, d), row), _resident(w_in.shape), _resident(w_out.shape),
                _resident(ln_g.shape), _resident(ln_b.shape)]
    args = [x, w_in, w_out, ln_g, ln_b]
    if ple is not None:
        p, w_gate, w_proj = ple
        in_specs += [pl.BlockSpec((tm, p.shape[1]), row), _resident(w_gate.shape), _resident(w_proj.shape)]
        args += [p, w_gate, w_proj]
    return pl.pallas_call(
        functools.partial(_ffn_kernel, alpha=alpha, d_ff=d_ff, has_ple=ple is not None),
        grid=(t // tm,),
        in_specs=in_specs,
        out_specs=pl.BlockSpec((tm, d), row),
        out_shape=jax.ShapeDtypeStruct((t, d), F32),
        compiler_params=_params("parallel"),
        name="ffn_ple" if ple is not None else "ffn",
    )(*args)


def _out_proj_kernel(y_ref, x_ref, w_ref, g_ref, b_ref, o_ref, *, alpha):
    z = alpha * x_ref[...] + _dot(y_ref[...], w_ref[...])
    o_ref[...] = _layer_norm(z, g_ref[...], b_ref[...])


def _out_proj(y, x, w_o, ln_g, ln_b, alpha):
    t, d = x.shape
    tm = TOKEN_TILE
    row = lambda i: (i, 0)
    return pl.pallas_call(
        functools.partial(_out_proj_kernel, alpha=alpha),
        grid=(t // tm,),
        in_specs=[pl.BlockSpec((tm, y.shape[1]), row), pl.BlockSpec((tm, d), row), _resident(w_o.shape),
                  _resident(ln_g.shape), _resident(ln_b.shape)],
        out_specs=pl.BlockSpec((tm, d), row),
        out_shape=jax.ShapeDtypeStruct((t, d), F32),
        compiler_params=_params("parallel"),
        name="out_proj",
    )(y, x, w_o, ln_g, ln_b)


def _gla_proj_kernel(x_ref, win_ref, wa2_ref, ba_ref, tri_ref,
                     qd_ref, ki_ref, ks_ref, v_ref, dec_ref, gate_ref, *, kd, vd):
    tm = x_ref.shape[0]
    xb = x_ref[...].astype(BF16)
    proj = _dot(xb, win_ref[...])
    q = proj[:, :kd] * ((kd // GLA_HEADS) ** -0.5)
    k = proj[:, kd:2 * kd]
    v = proj[:, 2 * kd:2 * kd + vd]
    r = proj[:, 2 * kd + vd:2 * kd + 2 * vd]
    a_lr = proj[:, 2 * kd + 2 * vd:]
    log_a = _log_sigmoid(_dot(a_lr.astype(BF16), wa2_ref[...]) + ba_ref[...]) * (1.0 / GLA_TAU)
    tri = tri_ref[...]
    bcum = jnp.concatenate(
        [_tri_cumsum(tri, log_a[s:s + TRI_BLOCK]) for s in range(0, tm, TRI_BLOCK)], axis=0)
    nc = tm // GLA_CHUNK
    b3 = bcum.reshape(nc, GLA_CHUNK, kd)
    b_last = b3[:, GLA_CHUNK - 1:GLA_CHUNK, :]
    qd_ref[...] = (q * jnp.exp(bcum)).astype(BF16)
    ki_ref[...] = (k * jnp.exp(-bcum)).astype(BF16)
    ks_ref[...] = (k.reshape(nc, GLA_CHUNK, kd) * jnp.exp(b_last - b3)).reshape(tm, kd).astype(BF16)
    dec_ref[...] = jnp.exp(b_last)
    v_ref[...] = v.astype(BF16)
    gate_ref[...] = r * _sigmoid(r)


def _gla_proj(x, w_in, w_a2, b_a, tri):
    t, d = x.shape
    kd = w_a2.shape[1]
    vd = (w_in.shape[1] - 2 * kd - w_a2.shape[0]) // 2
    tm = TOKEN_TILE
    row = lambda i: (i, 0)
    return pl.pallas_call(
        functools.partial(_gla_proj_kernel, kd=kd, vd=vd),
        grid=(t // tm,),
        in_specs=[pl.BlockSpec((tm, d), row), _resident(w_in.shape), _resident(w_a2.shape),
                  _resident(b_a.shape), _resident(tri.shape)],
        out_specs=[pl.BlockSpec((tm, kd), row), pl.BlockSpec((tm, kd), row), pl.BlockSpec((tm, kd), row),
                   pl.BlockSpec((tm, vd), row), pl.BlockSpec((tm // GLA_CHUNK, 1, kd), lambda i: (i, 0, 0)),
                   pl.BlockSpec((tm, vd), row)],
        out_shape=[jax.ShapeDtypeStruct((t, kd), BF16)] * 3 + [
            jax.ShapeDtypeStruct((t, vd), BF16),
            jax.ShapeDtypeStruct((t // GLA_CHUNK, 1, kd), F32),
            jax.ShapeDtypeStruct((t, vd), F32)],
        compiler_params=_params("parallel"),
        name="gla_proj",
    )(x, w_in, w_a2, b_a, tri)


def _gla_rec_kernel(qd_ref, ki_ref, ks_ref, v_ref, dec_ref, gate_ref, gng_ref, gnb_ref, o_ref, st_ref):
    tm = qd_ref.shape[1]
    dk = qd_ref.shape[2] // GLA_HEADS
    dv = v_ref.shape[2] // GLA_HEADS

    @pl.when(pl.program_id(1) == 0)
    def _():
        st_ref[...] = jnp.zeros_like(st_ref)

    rows = lax.broadcasted_iota(jnp.int32, (GLA_CHUNK, GLA_CHUNK), 0)
    cols = lax.broadcasted_iota(jnp.int32, (GLA_CHUNK, GLA_CHUNK), 1)
    causal = cols <= rows

    def chunk(c, carry):
        r0 = pl.multiple_of(c * GLA_CHUNK, GLA_CHUNK)
        rs = pl.ds(r0, GLA_CHUNK)
        for h in range(GLA_HEADS):
            ksl = slice(h * dk, (h + 1) * dk)
            vsl = slice(h * dv, (h + 1) * dv)
            qd = qd_ref[0, rs, ksl]
            ki = ki_ref[0, rs, ksl]
            ks = ks_ref[0, rs, ksl]
            vv = v_ref[0, rs, vsl]
            attn = jnp.where(causal, _dot_nt(qd, ki), 0.0).astype(BF16)
            st = st_ref[h]
            o = _dot(attn, vv) + _dot_nt(qd, st.astype(BF16))
            vt = jnp.transpose(vv.astype(F32)).astype(BF16)
            st_ref[h] = st * dec_ref[0, c, :, ksl] + _dot(vt, ks)
            on = _layer_norm(o, gng_ref[:, vsl], gnb_ref[:, vsl])
            o_ref[0, rs, vsl] = (on * gate_ref[0, rs, vsl]).astype(BF16)
        return carry

    lax.fori_loop(0, tm // GLA_CHUNK, chunk, 0)


def _gla_rec(qd, ki, ks, v, dec, gate, gn_g, gn_b):
    b, s, kd = qd.shape
    vd = v.shape[2]
    tm = TOKEN_TILE
    blk = lambda bi, i: (bi, i, 0)
    return pl.pallas_call(
        _gla_rec_kernel,
        grid=(b, s // tm),
        in_specs=[pl.BlockSpec((1, tm, kd), blk), pl.BlockSpec((1, tm, kd), blk), pl.BlockSpec((1, tm, kd), blk),
                  pl.BlockSpec((1, tm, vd), blk),
                  pl.BlockSpec((1, tm // GLA_CHUNK, 1, kd), lambda bi, i: (bi, i, 0, 0)),
                  pl.BlockSpec((1, tm, vd), blk), _resident(gn_g.shape), _resident(gn_b.shape)],
        out_specs=pl.BlockSpec((1, tm, vd), blk),
        out_shape=jax.ShapeDtypeStruct((b, s, vd), BF16),
        scratch_shapes=[pltpu.VMEM((GLA_HEADS, vd // GLA_HEADS, kd // GLA_HEADS), F32)],
        compiler_params=_params("arbitrary", "arbitrary"),
        name="gla_rec",
    )(qd, ki, ks, v, dec, gate, gn_g, gn_b)


def _fox_kv_kernel(x_ref, wk_ref, wvt_ref, wf_ref, bf_ref, tri_ref, spread_ref, group_ref, eye_ref,
                   k_ref, vt_ref, crelt_ref, cblk_ref, cend_ref, kpm_ref, carry_ref, kmax_ref):
    @pl.when(pl.program_id(1) == 0)
    def _():
        carry_ref[...] = jnp.zeros_like(carry_ref)
        kmax_ref[...] = jnp.zeros_like(kmax_ref)

    tk = x_ref.shape[1]
    xb = x_ref[0].astype(BF16)
    log_f = _log_sigmoid(_dot(xb, wf_ref[...]) + bf_ref[...])
    c_rel = _tri_cumsum(tri_ref[...], log_f) * LOG2E
    terms = jnp.concatenate(_split3(-c_rel), axis=1)
    bias = _dot(terms, spread_ref[...])
    crelt_ref[0, 0] = -_dot_nt(eye_ref[...], terms)
    k = _dot(xb, wk_ref[...])
    slot_lane = lax.broadcasted_iota(jnp.int32, k.shape, 1) % FOX_SLOT
    is_one_lane = (slot_lane >= FOX_HEAD_DIM + 3) & (slot_lane < FOX_HEAD_DIM + 6)
    k_aug = jnp.where(is_one_lane, 1.0, k + bias).astype(BF16)
    for h in range(FOX_HEADS):
        k_ref[0, 0, h] = k_aug[:, h * FOX_SLOT:(h + 1) * FOX_SLOT]
    vt = _dot_nt(wvt_ref[...], xb).astype(BF16).reshape(FOX_HEADS, FOX_HEAD_DIM, tk)
    pad_row = lax.broadcasted_iota(jnp.int32, (FOX_HEADS, FOX_V_ROWS - FOX_HEAD_DIM, tk), 1)
    vt_ref[0, 0] = jnp.concatenate([vt, jnp.where(pad_row == 0, 1.0, 0.0).astype(BF16)], axis=1)
    cblk_ref[0, 0] = carry_ref[...]
    carry_ref[...] = carry_ref[...] + c_rel[tk - 1:tk, :]
    cend_ref[0, 0] = carry_ref[...]
    kr = k.astype(BF16).astype(F32)
    norm2 = _dot((kr * kr).astype(BF16), group_ref[...]) * NORM_SLACK
    kmax_ref[...] = jnp.maximum(kmax_ref[...], jnp.sqrt(jnp.max(norm2, axis=0, keepdims=True)))
    kpm_ref[0, 0] = kmax_ref[...]


def _fox_kv(x, wk_aug, wvt, wf, b_f, tri, spread, group, eye):
    b, s, d = x.shape
    tk = FOX_BLOCK
    nkv = s // tk
    nh = wf.shape[1]
    blk4 = lambda bi, j: (bi, j, 0, 0)
    blk5 = lambda bi, j: (bi, j, 0, 0, 0)
    small = jax.ShapeDtypeStruct((b, nkv, 1, nh), F32)
    return pl.pallas_call(
        _fox_kv_kernel,
        grid=(b, nkv),
        in_specs=[pl.BlockSpec((1, tk, d), lambda bi, j: (bi, j, 0)), _resident(wk_aug.shape),
                  _resident(wvt.shape), _resident(wf.shape), _resident(b_f.shape), _resident(tri.shape),
                  _resident(spread.shape), _resident(group.shape), _resident(eye.shape)],
        out_specs=[pl.BlockSpec((1, 1, nh, tk, FOX_SLOT), blk5),
                   pl.BlockSpec((1, 1, nh, FOX_V_ROWS, tk), blk5),
                   pl.BlockSpec((1, 1, 3 * nh, tk), blk4),
                   pl.BlockSpec((1, 1, 1, nh), blk4), pl.BlockSpec((1, 1, 1, nh), blk4),
                   pl.BlockSpec((1, 1, 1, nh), blk4)],
        out_shape=[jax.ShapeDtypeStruct((b, nkv, nh, tk, FOX_SLOT), BF16),
                   jax.ShapeDtypeStruct((b, nkv, nh, FOX_V_ROWS, tk), BF16),
                   jax.ShapeDtypeStruct((b, nkv, 3 * nh, tk), F32),
                   small, small, small],
        scratch_shapes=[pltpu.VMEM((1, nh), F32), pltpu.VMEM((1, nh), F32)],
        compiler_params=_params("arbitrary", "arbitrary"),
        name="fox_kv",
    )(x, wk_aug, wvt, wf, b_f, tri, spread, group, eye)


def _fox_q_kernel(x_ref, wqt_ref, wg_ref, crelt_ref, spreadt_ref, cblk_ref, cend_ref, kpm_ref,
                  qt_ref, g_ref, need_ref, shift_ref):
    i = pl.program_id(1)
    tq = x_ref.shape[1]
    xb = x_ref[0].astype(BF16)
    qt = _dot_nt(wqt_ref[...], xb) * (FOX_HEAD_DIM ** -0.5 * LOG2E)
    c_rows = _dot(spreadt_ref[...], crelt_ref[0, 0].astype(BF16))
    qt3 = (qt + c_rows).reshape(FOX_HEADS, FOX_SLOT, tq)
    slot_row = lax.broadcasted_iota(jnp.int32, qt3.shape, 1)
    is_one_row = (slot_row >= FOX_HEAD_DIM) & (slot_row < FOX_HEAD_DIM + 3)
    qt_ref[0] = jnp.where(is_one_row, 1.0, qt3).astype(BF16)
    g_ref[0] = _dot(xb, wg_ref[...])

    qr = qt.astype(BF16).astype(F32).reshape(FOX_HEADS, FOX_SLOT, tq)
    qn = jnp.sqrt(jnp.max(jnp.sum(qr * qr, axis=1), axis=1, keepdims=True))
    blk = lax.broadcasted_iota(jnp.int32, cblk_ref.shape[1:], 1)
    at_i = blk == i
    c_before = jnp.sum(jnp.where(at_i, cblk_ref[0], 0.0), axis=1, keepdims=True)
    k_norm = jnp.sum(jnp.where(at_i, kpm_ref[0], 0.0), axis=1, keepdims=True)
    qk_bound = NORM_SLACK * qn * k_norm
    thr = c_before + EXP2_ZERO_GAP + 2.0 * qk_bound
    needed = (blk < i) & jnp.logical_not(cend_ref[0] > thr)
    need_ref[0, 0] = jnp.sum(jnp.where(needed, 1.0, 0.0), axis=1, keepdims=True).astype(jnp.int32)
    shift_ref[0, 0] = qk_bound


def _fox_q(x, wqt_aug, wg, crelt, spreadt, cblk_t, cend_t, kpm_t):
    b, s, d = x.shape
    tm = FOX_BLOCK
    nkv = cblk_t.shape[2]
    per_batch = pl.BlockSpec((1, FOX_HEADS, nkv), lambda bi, i: (bi, 0, 0))
    return pl.pallas_call(
        _fox_q_kernel,
        grid=(b, s // tm),
        in_specs=[pl.BlockSpec((1, tm, d), lambda bi, i: (bi, i, 0)), _resident(wqt_aug.shape),
                  _resident(wg.shape),
                  pl.BlockSpec((1, 1, crelt.shape[2], tm), lambda bi, i: (bi, i, 0, 0)),
                  _resident(spreadt.shape), per_batch, per_batch, per_batch],
        out_specs=[pl.BlockSpec((1, FOX_HEADS, FOX_SLOT, tm), lambda bi, i: (bi, 0, 0, i)),
                   pl.BlockSpec((1, tm, wg.shape[1]), lambda bi, i: (bi, i, 0)),
                   pl.BlockSpec((1, 1, FOX_HEADS, 1), lambda bi, i: (bi, i, 0, 0)),
                   pl.BlockSpec((1, 1, FOX_HEADS, 1), lambda bi, i: (bi, i, 0, 0))],
        out_shape=[jax.ShapeDtypeStruct((b, FOX_HEADS, FOX_SLOT, s), BF16),
                   jax.ShapeDtypeStruct((b, s, wg.shape[1]), F32),
                   jax.ShapeDtypeStruct((b, s // tm, FOX_HEADS, 1), jnp.int32),
                   jax.ShapeDtypeStruct((b, s // tm, FOX_HEADS, 1), F32)],
        compiler_params=_params("parallel", "parallel"),
        name="fox_q",
    )(x, wqt_aug, wg, crelt, spreadt, cblk_t, cend_t, kpm_t)


def _fox_attn_kernel(cblk_ref, need_ref, shift_ref, qt_ref, k_ref, vt_ref, g_ref, o_ref, acc_ref, *bufs, nkv):
    bi = pl.program_id(0)
    hp = pl.program_id(1)
    i = pl.program_id(2)
    tq = qt_ref.shape[3]
    tk = k_ref.shape[3]
    nhp = FOX_HEADS_PER_STEP
    assert nhp == 2
    dh = FOX_HEAD_DIM
    unroll = len(bufs)

    def head_base(hh):
        return (bi * FOX_HEADS + hp * nhp + hh) * nkv

    def scores(hh, j):
        return _dot(k_ref[0, j, hh], qt_ref[0, hh])

    key_pos = lax.broadcasted_iota(jnp.int32, (tk, tq), 0)
    qry_pos = lax.broadcasted_iota(jnp.int32, (tk, tq), 1)
    bound = [shift_ref[head_base(hh) + i] for hh in range(nhp)]
    fixed_shift_ok = jnp.maximum(bound[0], bound[1]) <= MAX_FIXED_SHIFT

    @pl.when(fixed_shift_ok)
    def _():
        for hh in range(nhp):
            p = jnp.where(key_pos <= qry_pos, jnp.exp2(scores(hh, i) - bound[hh]), 0.0)
            acc_ref[hh] = _dot(vt_ref[0, i, hh], p.astype(BF16))

        n0 = need_ref[head_base(0) + i]
        total = n0 + need_ref[head_base(1) + i]
        padded = ((total + unroll - 1) // unroll) * unroll

        def item(e):
            hh = (e >= n0).astype(jnp.int32)
            j = jnp.maximum(i - 1 - (e - hh * n0), 0)
            base = head_base(hh)
            shift = cblk_ref[base + i] - cblk_ref[base + j] - shift_ref[base + i]
            return hh, j, jnp.where(e < total, shift, -jnp.inf)

        def probabilities(e, p_ref):
            hh, j, shift = item(e)
            p_ref[...] = jnp.exp2(scores(hh, j) + shift).astype(BF16)

        def accumulate(e, p_ref):
            hh, j, _ = item(e)
            acc_ref[hh] = acc_ref[hh] + _dot(vt_ref[0, j, hh], p_ref[...])

        @pl.when(total > 0)
        def _():
            probabilities(0, bufs[0])

            def body(t, carry):
                e = unroll * t
                for u in range(unroll):
                    probabilities(e + u + 1, bufs[(u + 1) % unroll])
                    accumulate(e + u, bufs[u])
                return carry

            lax.fori_loop(0, padded // unroll - 1, body, 0)
            e = padded - unroll
            for u in range(unroll - 1):
                probabilities(e + u + 1, bufs[u + 1])
                accumulate(e + u, bufs[u])
            accumulate(padded - 1, bufs[unroll - 1])

    @pl.when(jnp.logical_not(fixed_shift_ok))
    def _():
        for hh in range(nhp):
            base = head_base(hh)
            s = jnp.where(key_pos <= qry_pos, scores(hh, i), -jnp.inf)
            m = jnp.max(s, axis=0, keepdims=True)
            acc_ref[hh] = _dot(vt_ref[0, i, hh], jnp.exp2(s - m).astype(BF16))

            def step(jj, m, hh=hh, base=base):
                j = i - 1 - jj
                s = scores(hh, j) + (cblk_ref[base + i] - cblk_ref[base + j])
                m_new = jnp.maximum(m, jnp.max(s, axis=0, keepdims=True))
                acc_ref[hh] = (jnp.exp2(m - m_new) * acc_ref[hh]
                               + _dot(vt_ref[0, j, hh], jnp.exp2(s - m_new).astype(BF16)))
                return m_new

            lax.fori_loop(0, need_ref[base + i], step, m)

    o = jnp.concatenate([acc_ref[hh, :dh, :] / acc_ref[hh, dh:dh + 1, :] for hh in range(nhp)], axis=0)
    o_ref[0] = (jnp.transpose(o) * _sigmoid(g_ref[0])).astype(BF16)


def _fox_attn(cblk, need, shift, qt, k, vt, g):
    b, nkv, nh, tk, slot = k.shape
    v_rows = vt.shape[3]
    s = nkv * tk
    tq = FOX_BLOCK
    nhp = FOX_HEADS_PER_STEP
    out_w = nhp * FOX_HEAD_DIM
    smem = pl.BlockSpec(memory_space=pltpu.SMEM)
    return pl.pallas_call(
        functools.partial(_fox_attn_kernel, nkv=nkv),
        grid=(b, nh // nhp, s // tq),
        in_specs=[smem, smem, smem,
                  pl.BlockSpec((1, nhp, slot, tq), lambda bi, hp, i: (bi, hp, 0, i)),
                  pl.BlockSpec((1, nkv, nhp, tk, slot), lambda bi, hp, i: (bi, 0, hp, 0, 0)),
                  pl.BlockSpec((1, nkv, nhp, v_rows, tk), lambda bi, hp, i: (bi, 0, hp, 0, 0)),
                  pl.BlockSpec((1, tq, out_w), lambda bi, hp, i: (bi, i, hp))],
        out_specs=pl.BlockSpec((1, tq, out_w), lambda bi, hp, i: (bi, i, hp)),
        out_shape=jax.ShapeDtypeStruct((b, s, nh * FOX_HEAD_DIM), BF16),
        scratch_shapes=[pltpu.VMEM((nhp, v_rows, tq), F32)] + [pltpu.VMEM((tk, tq), BF16)] * FOX_ATTN_UNROLL,
        compiler_params=_params("arbitrary", "arbitrary", "arbitrary"),
        name="fox_attn",
    )(cblk, need, shift, qt, k, vt, g)


def _block_tri(n, block):
    r = jnp.arange(n)
    return ((r[:, None] >= r[None, :]) & (r[:, None] // block == r[None, :] // block)).astype(BF16)


def _spread_heads(w):
    d = w.shape[0]
    w = w.reshape(d, FOX_HEADS, FOX_HEAD_DIM)
    w = jnp.pad(w, ((0, 0), (0, 0), (0, FOX_SLOT - FOX_HEAD_DIM)))
    return w.reshape(d, FOX_HEADS * FOX_SLOT)


def _head_group_matrix():
    return (jnp.arange(FOX_HEADS * FOX_SLOT)[:, None] // FOX_SLOT == jnp.arange(FOX_HEADS)[None, :]).astype(BF16)


def _bias_spread_matrix(first_lane):
    rows = jnp.arange(3 * FOX_HEADS)
    term, head = rows // FOX_HEADS, rows % FOX_HEADS
    cols = head * FOX_SLOT + first_lane + term
    return (jnp.arange(FOX_HEADS * FOX_SLOT)[None, :] == cols[:, None]).astype(BF16)


def kernel(x, p, ffn1_w_in, ffn1_w_out, ln1_g, ln1_b, gla_w_in, gla_w_a2, gla_b_a, gla_gn_g, gla_gn_b, gla_w_o, fox_w_kvf, fox_b_f, fox_w_in, fox_w_o, ln2_g, ln2_b, ffn2_w_in, ffn2_w_out, ple_w_gate, ple_w_proj, ln3_g, ln3_b):
    bsz, seq, d = x.shape
    depth = ffn1_w_in.shape[0]
    n_gla = gla_w_in.shape[0]
    alpha = (2 * depth) ** 0.25
    t = bsz * seq
    fox_w = FOX_HEADS * FOX_HEAD_DIM
    row = lambda a: a.reshape(1, -1)

    tri_chunk = _block_tri(TRI_BLOCK, GLA_CHUNK)
    tri_block = _block_tri(FOX_BLOCK, FOX_BLOCK)
    key_bias_spread = _bias_spread_matrix(FOX_HEAD_DIM)
    query_bias_spread_t = _bias_spread_matrix(FOX_HEAD_DIM + 3).T

    xf = x.reshape(t, d)
    pf = p.reshape(depth, t, p.shape[-1])
    heads_major = lambda a: jnp.transpose(a[:, :, 0, :], (0, 2, 1))
    fox_k = fox_vt = fox_crelt = cblk_t = cend_t = kpm_t = None
    for i in range(depth):
        if i == n_gla:
            wk_aug = _spread_heads(fox_w_kvf[:, :fox_w]).astype(BF16)
            wvt = fox_w_kvf[:, fox_w:2 * fox_w].T.astype(BF16)
            wf = fox_w_kvf[:, 2 * fox_w:].astype(BF16)
            fox_k, fox_vt, fox_crelt, cblk, cend, kpm = _fox_kv(
                xf.reshape(bsz, seq, d), wk_aug, wvt, wf, row(fox_b_f), tri_block, key_bias_spread,
                _head_group_matrix(), jnp.eye(3 * FOX_HEADS, dtype=BF16))
            cblk_t, cend_t, kpm_t = heads_major(cblk), heads_major(cend), heads_major(kpm)
        xf = _ffn(xf, ffn1_w_in[i].astype(BF16), ffn1_w_out[i].astype(BF16), row(ln1_g[i]), row(ln1_b[i]), alpha)
        if i < n_gla:
            qd, ki, ks, v, dec, gate = _gla_proj(xf, gla_w_in[i].astype(BF16), gla_w_a2[i].astype(BF16),
                                                 row(gla_b_a[i]), tri_chunk)
            b3 = lambda a: a.reshape(bsz, seq, -1)
            mix = _gla_rec(b3(qd), b3(ki), b3(ks), b3(v), dec.reshape(bsz, seq // GLA_CHUNK, 1, -1), b3(gate),
                           row(gla_gn_g[i]), row(gla_gn_b[i])).reshape(t, -1)
            w_o = gla_w_o[i]
        else:
            j = i - n_gla
            wqt_aug = _spread_heads(fox_w_in[j][:, :fox_w]).T.astype(BF16)
            qt, g, need, shift = _fox_q(xf.reshape(bsz, seq, d), wqt_aug, fox_w_in[j][:, fox_w:].astype(BF16),
                                        fox_crelt, query_bias_spread_t, cblk_t, cend_t, kpm_t)
            flat = lambda a: jnp.transpose(a[:, :, :, 0], (0, 2, 1)).reshape(-1)
            mix = _fox_attn(cblk_t.reshape(-1), flat(need), flat(shift), qt, fox_k, fox_vt, g).reshape(t, -1)
            w_o = fox_w_o[j]
        xf = _out_proj(mix, xf, w_o.astype(BF16), row(ln2_g[i]), row(ln2_b[i]), alpha)
        xf = _ffn(xf, ffn2_w_in[i].astype(BF16), ffn2_w_out[i].astype(BF16), row(ln3_g[i]), row(ln3_b[i]), alpha,
                  ple=(pf[i], ple_w_gate[i].astype(BF16), ple_w_proj[i].astype(BF16)))
    return xf.reshape(bsz, seq, d)
```

```python
import functools
import math

import jax
import jax.numpy as jnp
from jax import lax
from jax.experimental import pallas as pl
from jax.experimental.pallas import tpu as pltpu

F32 = jnp.float32
BF16 = jnp.bfloat16

GLA_HEADS = 4
GLA_TAU = 16.0
GLA_CHUNK = 64
FOX_HEADS = 16
FOX_HEAD_DIM = 64
LN_EPS = 1e-5

V7X_VMEM_LIMIT_BYTES = 56 * 1024 * 1024

TOKEN_TILE = 512
FFN_TOKEN_TILE = 512
FOX_BLOCK = 512
FOX_SLOT = 128
FOX_HEADS_PER_STEP = 2
TRI_BLOCK = 256
LOG2E = math.log2(math.e)
EXP2_ZERO_GAP = 152.0
NORM_SLACK = 1.01
FOX_V_ROWS = 80
FOX_ATTN_UNROLL = 4
MAX_FIXED_SHIFT = 50.0


def _params(*sems):
    return pltpu.CompilerParams(dimension_semantics=sems, vmem_limit_bytes=V7X_VMEM_LIMIT_BYTES)


def _resident(shape):
    zeros = (0,) * len(shape)
    return pl.BlockSpec(shape, lambda *_: zeros, pipeline_mode=pl.Buffered(1))


def _dot(a, b):
    return jnp.dot(a, b, preferred_element_type=F32)


def _dot_nt(a, b):
    return lax.dot_general(a, b, (((1,), (1,)), ((), ())), preferred_element_type=F32)


def _split3(a):
    hi = a.astype(BF16)
    r1 = a - hi.astype(F32)
    mid = r1.astype(BF16)
    lo = (r1 - mid.astype(F32)).astype(BF16)
    return hi, mid, lo


def _tri_cumsum(tri, a):
    hi, mid, lo = _split3(a)
    return _dot(tri, hi) + _dot(tri, mid) + _dot(tri, lo)


def _log_sigmoid(z):
    return jnp.minimum(z, 0.0) - jnp.log1p(jnp.exp(-jnp.abs(z)))


def _sigmoid(z):
    return 1.0 / (1.0 + jnp.exp(-z))


def _layer_norm(z, g, b):
    mu = jnp.mean(z, axis=-1, keepdims=True)
    zc = z - mu
    var = jnp.mean(zc * zc, axis=-1, keepdims=True)
    return zc * lax.rsqrt(var + LN_EPS) * g + b


def _ffn_kernel(*refs, alpha, d_ff, after_mixer):
    if after_mixer:
        (x_ref, mix_ref, wo_ref, g2_ref, b2_ref, win_ref, wout_ref, g_ref, b_ref,
         p_ref, wgate_ref, wproj_ref, o_ref) = refs
        x = _layer_norm(alpha * x_ref[...] + _dot(mix_ref[...], wo_ref[...]), g2_ref[...], b2_ref[...])
    else:
        x_ref, win_ref, wout_ref, g_ref, b_ref, o_ref = refs
        x = x_ref[...]
    xb = x.astype(BF16)
    gate = _dot(xb, win_ref[:, :d_ff])
    up = _dot(xb, win_ref[:, d_ff:])
    h = (gate * _sigmoid(gate) * up).astype(BF16)
    z = alpha * x + 0.5 * _dot(h, wout_ref[...])
    if after_mixer:
        ple_gate = _sigmoid(_dot(xb, wgate_ref[...]))
        z = z + ple_gate * _dot(p_ref[...].astype(BF16), wproj_ref[...])
    o_ref[...] = _layer_norm(z, g_ref[...], b_ref[...])


def _ffn(x, w_in, w_out, ln_g, ln_b, alpha, mixer=None, ple=None):
    assert (mixer is None) == (ple is None)
    t, d = x.shape
    d_ff = w_out.shape[0]
    tm = FFN_TOKEN_TILE
    row = lambda i: (i, 0)
    in_specs = [pl.BlockSpec((tm, d), row)]
    args = [x]
    if mixer is not None:
        mix, w_o, ln2_g, ln2_b = mixer
        in_specs += [pl.BlockSpec((tm, mix.shape[1]), row), _resident(w_o.shape), _resident(ln2_g.shape),
                     _resident(ln2_b.shape)]
        args += [mix, w_o, ln2_g, ln2_b]
    in_specs += [_resident(w_in.shape), _resident(w_out.shape), _resident(ln_g.shape), _resident(ln_b.shape)]
    args += [w_in, w_out, ln_g, ln_b]
    if ple is not None:
        p, w_gate, w_proj = ple
        in_specs += [pl.BlockSpec((tm, p.shape[1]), row), _resident(w_gate.shape), _resident(w_proj.shape)]
        args += [p, w_gate, w_proj]
    return pl.pallas_call(
        functools.partial(_ffn_kernel, alpha=alpha, d_ff=d_ff, after_mixer=mixer is not None),
        grid=(t // tm,),
        in_specs=in_specs,
        out_specs=pl.BlockSpec((tm, d), row),
        out_shape=jax.ShapeDtypeStruct((t, d), F32),
        compiler_params=_params("parallel"),
        name="ffn_after_mixer" if mixer is not None else "ffn",
    )(*args)


def _gla_proj_kernel(x_ref, win_ref, wa2_ref, ba_ref, tri_ref,
                     qd_ref, ki_ref, ks_ref, v_ref, dec_ref, gate_ref, *, kd, vd):
    tm = x_ref.shape[0]
    xb = x_ref[...].astype(BF16)
    proj = _dot(xb, win_ref[...])
    q = proj[:, :kd] * ((kd // GLA_HEADS) ** -0.5)
    k = proj[:, kd:2 * kd]
    v = proj[:, 2 * kd:2 * kd + vd]
    r = proj[:, 2 * kd + vd:2 * kd + 2 * vd]
    a_lr = proj[:, 2 * kd + 2 * vd:]
    log_a = _log_sigmoid(_dot(a_lr.astype(BF16), wa2_ref[...]) + ba_ref[...]) * (1.0 / GLA_TAU)
    tri = tri_ref[...]
    bcum = jnp.concatenate(
        [_tri_cumsum(tri, log_a[s:s + TRI_BLOCK]) for s in range(0, tm, TRI_BLOCK)], axis=0)
    nc = tm // GLA_CHUNK
    b3 = bcum.reshape(nc, GLA_CHUNK, kd)
    b_last = b3[:, GLA_CHUNK - 1:GLA_CHUNK, :]
    qd_ref[...] = (q * jnp.exp(bcum)).astype(BF16)
    ki_ref[...] = (k * jnp.exp(-bcum)).astype(BF16)
    ks_ref[...] = (k.reshape(nc, GLA_CHUNK, kd) * jnp.exp(b_last - b3)).reshape(tm, kd).astype(BF16)
    dec_ref[...] = jnp.exp(b_last)
    v_ref[...] = v.astype(BF16)
    gate_ref[...] = r * _sigmoid(r)


def _gla_proj(x, w_in, w_a2, b_a, tri):
    t, d = x.shape
    kd = w_a2.shape[1]
    vd = (w_in.shape[1] - 2 * kd - w_a2.shape[0]) // 2
    tm = TOKEN_TILE
    row = lambda i: (i, 0)
    return pl.pallas_call(
        functools.partial(_gla_proj_kernel, kd=kd, vd=vd),
        grid=(t // tm,),
        in_specs=[pl.BlockSpec((tm, d), row), _resident(w_in.shape), _resident(w_a2.shape),
                  _resident(b_a.shape), _resident(tri.shape)],
        out_specs=[pl.BlockSpec((tm, kd), row), pl.BlockSpec((tm, kd), row), pl.BlockSpec((tm, kd), row),
                   pl.BlockSpec((tm, vd), row), pl.BlockSpec((tm // GLA_CHUNK, 1, kd), lambda i: (i, 0, 0)),
                   pl.BlockSpec((tm, vd), row)],
        out_shape=[jax.ShapeDtypeStruct((t, kd), BF16)] * 3 + [
            jax.ShapeDtypeStruct((t, vd), BF16),
            jax.ShapeDtypeStruct((t // GLA_CHUNK, 1, kd), F32),
            jax.ShapeDtypeStruct((t, vd), F32)],
        compiler_params=_params("parallel"),
        name="gla_proj",
    )(x, w_in, w_a2, b_a, tri)


def _gla_rec_kernel(qd_ref, ki_ref, ks_ref, v_ref, dec_ref, gate_ref, gng_ref, gnb_ref, o_ref, st_ref):
    tm = qd_ref.shape[1]
    dk = qd_ref.shape[2] // GLA_HEADS
    dv = v_ref.shape[2] // GLA_HEADS

    @pl.when(pl.program_id(1) == 0)
    def _():
        st_ref[...] = jnp.zeros_like(st_ref)

    rows = lax.broadcasted_iota(jnp.int32, (GLA_CHUNK, GLA_CHUNK), 0)
    cols = lax.broadcasted_iota(jnp.int32, (GLA_CHUNK, GLA_CHUNK), 1)
    causal = cols <= rows

    def chunk(c, carry):
        r0 = pl.multiple_of(c * GLA_CHUNK, GLA_CHUNK)
        rs = pl.ds(r0, GLA_CHUNK)
        for h in range(GLA_HEADS):
            ksl = slice(h * dk, (h + 1) * dk)
            vsl = slice(h * dv, (h + 1) * dv)
            qd = qd_ref[0, rs, ksl]
            ki = ki_ref[0, rs, ksl]
            ks = ks_ref[0, rs, ksl]
            vv = v_ref[0, rs, vsl]
            attn = jnp.where(causal, _dot_nt(qd, ki), 0.0).astype(BF16)
            st = st_ref[h]
            o = _dot(attn, vv) + _dot_nt(qd, st.astype(BF16))
            vt = jnp.transpose(vv.astype(F32)).astype(BF16)
            st_ref[h] = st * dec_ref[0, c, :, ksl] + _dot(vt, ks)
            on = _layer_norm(o, gng_ref[:, vsl], gnb_ref[:, vsl])
            o_ref[0, rs, vsl] = (on * gate_ref[0, rs, vsl]).astype(BF16)
        return carry

    lax.fori_loop(0, tm // GLA_CHUNK, chunk, 0)


def _gla_rec(qd, ki, ks, v, dec, gate, gn_g, gn_b):
    b, s, kd = qd.shape
    vd = v.shape[2]
    tm = TOKEN_TILE
    blk = lambda bi, i: (bi, i, 0)
    return pl.pallas_call(
        _gla_rec_kernel,
        grid=(b, s // tm),
        in_specs=[pl.BlockSpec((1, tm, kd), blk), pl.BlockSpec((1, tm, kd), blk), pl.BlockSpec((1, tm, kd), blk),
                  pl.BlockSpec((1, tm, vd), blk),
                  pl.BlockSpec((1, tm // GLA_CHUNK, 1, kd), lambda bi, i: (bi, i, 0, 0)),
                  pl.BlockSpec((1, tm, vd), blk), _resident(gn_g.shape), _resident(gn_b.shape)],
        out_specs=pl.BlockSpec((1, tm, vd), blk),
        out_shape=jax.ShapeDtypeStruct((b, s, vd), BF16),
        scratch_shapes=[pltpu.VMEM((GLA_HEADS, vd // GLA_HEADS, kd // GLA_HEADS), F32)],
        compiler_params=_params("arbitrary", "arbitrary"),
        name="gla_rec",
    )(qd, ki, ks, v, dec, gate, gn_g, gn_b)


def _fox_kv_kernel(x_ref, wk_ref, wvt_ref, wf_ref, bf_ref, tri_ref, spread_ref, group_ref, eye_ref,
                   k_ref, vt_ref, crelt_ref, cblk_ref, cend_ref, kpm_ref, carry_ref, kmax_ref):
    @pl.when(pl.program_id(1) == 0)
    def _():
        carry_ref[...] = jnp.zeros_like(carry_ref)
        kmax_ref[...] = jnp.zeros_like(kmax_ref)

    tk = x_ref.shape[1]
    xb = x_ref[0].astype(BF16)
    log_f = _log_sigmoid(_dot(xb, wf_ref[...]) + bf_ref[...])
    c_rel = _tri_cumsum(tri_ref[...], log_f) * LOG2E
    terms = jnp.concatenate(_split3(-c_rel), axis=1)
    bias = _dot(terms, spread_ref[...])
    crelt_ref[0, 0] = -_dot_nt(eye_ref[...], terms)
    k = _dot(xb, wk_ref[...])
    slot_lane = lax.broadcasted_iota(jnp.int32, k.shape, 1) % FOX_SLOT
    is_one_lane = (slot_lane >= FOX_HEAD_DIM + 3) & (slot_lane < FOX_HEAD_DIM + 6)
    k_aug = jnp.where(is_one_lane, 1.0, k + bias).astype(BF16)
    for h in range(FOX_HEADS):
        k_ref[0, 0, h] = k_aug[:, h * FOX_SLOT:(h + 1) * FOX_SLOT]
    vt = _dot_nt(wvt_ref[...], xb).astype(BF16).reshape(FOX_HEADS, FOX_HEAD_DIM, tk)
    pad_row = lax.broadcasted_iota(jnp.int32, (FOX_HEADS, FOX_V_ROWS - FOX_HEAD_DIM, tk), 1)
    vt_ref[0, 0] = jnp.concatenate([vt, jnp.where(pad_row == 0, 1.0, 0.0).astype(BF16)], axis=1)
    cblk_ref[0, 0] = carry_ref[...]
    carry_ref[...] = carry_ref[...] + c_rel[tk - 1:tk, :]
    cend_ref[0, 0] = carry_ref[...]
    kr = k.astype(BF16).astype(F32)
    norm2 = _dot((kr * kr).astype(BF16), group_ref[...]) * NORM_SLACK
    kmax_ref[...] = jnp.maximum(kmax_ref[...], jnp.sqrt(jnp.max(norm2, axis=0, keepdims=True)))
    kpm_ref[0, 0] = kmax_ref[...]


def _fox_kv(x, wk_aug, wvt, wf, b_f, tri, spread, group, eye):
    b, s, d = x.shape
    tk = FOX_BLOCK
    nkv = s // tk
    nh = wf.shape[1]
    blk4 = lambda bi, j: (bi, j, 0, 0)
    blk5 = lambda bi, j: (bi, j, 0, 0, 0)
    small = jax.ShapeDtypeStruct((b, nkv, 1, nh), F32)
    return pl.pallas_call(
        _fox_kv_kernel,
        grid=(b, nkv),
        in_specs=[pl.BlockSpec((1, tk, d), lambda bi, j: (bi, j, 0)), _resident(wk_aug.shape),
                  _resident(wvt.shape), _resident(wf.shape), _resident(b_f.shape), _resident(tri.shape),
                  _resident(spread.shape), _resident(group.shape), _resident(eye.shape)],
        out_specs=[pl.BlockSpec((1, 1, nh, tk, FOX_SLOT), blk5),
                   pl.BlockSpec((1, 1, nh, FOX_V_ROWS, tk), blk5),
                   pl.BlockSpec((1, 1, 3 * nh, tk), blk4),
                   pl.BlockSpec((1, 1, 1, nh), blk4), pl.BlockSpec((1, 1, 1, nh), blk4),
                   pl.BlockSpec((1, 1, 1, nh), blk4)],
        out_shape=[jax.ShapeDtypeStruct((b, nkv, nh, tk, FOX_SLOT), BF16),
                   jax.ShapeDtypeStruct((b, nkv, nh, FOX_V_ROWS, tk), BF16),
                   jax.ShapeDtypeStruct((b, nkv, 3 * nh, tk), F32),
                   small, small, small],
        scratch_shapes=[pltpu.VMEM((1, nh), F32), pltpu.VMEM((1, nh), F32)],
        compiler_params=_params("arbitrary", "arbitrary"),
        name="fox_kv",
    )(x, wk_aug, wvt, wf, b_f, tri, spread, group, eye)


def _fox_q_kernel(x_ref, wqt_ref, wg_ref, crelt_ref, spreadt_ref, cblk_ref, cend_ref, kpm_ref,
                  qt_ref, g_ref, need_ref, need_fixed_ref, shift_ref):
    i = pl.program_id(1)
    tq = x_ref.shape[1]
    xb = x_ref[0].astype(BF16)
    qt = _dot_nt(wqt_ref[...], xb) * (FOX_HEAD_DIM ** -0.5 * LOG2E)
    c_rows = _dot(spreadt_ref[...], crelt_ref[0, 0].astype(BF16))
    qt3 = (qt + c_rows).reshape(FOX_HEADS, FOX_SLOT, tq)
    slot_row = lax.broadcasted_iota(jnp.int32, qt3.shape, 1)
    is_one_row = (slot_row >= FOX_HEAD_DIM) & (slot_row < FOX_HEAD_DIM + 3)
    qt_ref[0] = jnp.where(is_one_row, 1.0, qt3).astype(BF16)
    g_ref[0] = _dot(xb, wg_ref[...])

    qr = qt.astype(BF16).astype(F32).reshape(FOX_HEADS, FOX_SLOT, tq)
    qn = jnp.sqrt(jnp.max(jnp.sum(qr * qr, axis=1), axis=1, keepdims=True))
    blk = lax.broadcasted_iota(jnp.int32, cblk_ref.shape[1:], 1)
    at_i = blk == i
    c_before = jnp.sum(jnp.where(at_i, cblk_ref[0], 0.0), axis=1, keepdims=True)
    k_norm = jnp.sum(jnp.where(at_i, kpm_ref[0], 0.0), axis=1, keepdims=True)
    qk_bound = NORM_SLACK * qn * k_norm

    def count_blocks(thr):
        needed = (blk < i) & jnp.logical_not(cend_ref[0] > thr)
        return jnp.sum(jnp.where(needed, 1.0, 0.0), axis=1, keepdims=True).astype(jnp.int32)

    need_ref[0, 0] = count_blocks(c_before + EXP2_ZERO_GAP + 2.0 * qk_bound)
    need_fixed_ref[0, 0] = count_blocks(c_before + EXP2_ZERO_GAP)
    shift_ref[0, 0] = qk_bound


def _fox_q(x, wqt_aug, wg, crelt, spreadt, cblk_t, cend_t, kpm_t):
    b, s, d = x.shape
    tm = FOX_BLOCK
    nkv = cblk_t.shape[2]
    per_batch = pl.BlockSpec((1, FOX_HEADS, nkv), lambda bi, i: (bi, 0, 0))
    return pl.pallas_call(
        _fox_q_kernel,
        grid=(b, s // tm),
        in_specs=[pl.BlockSpec((1, tm, d), lambda bi, i: (bi, i, 0)), _resident(wqt_aug.shape),
                  _resident(wg.shape),
                  pl.BlockSpec((1, 1, crelt.shape[2], tm), lambda bi, i: (bi, i, 0, 0)),
                  _resident(spreadt.shape), per_batch, per_batch, per_batch],
        out_specs=[pl.BlockSpec((1, FOX_HEADS, FOX_SLOT, tm), lambda bi, i: (bi, 0, 0, i)),
                   pl.BlockSpec((1, tm, wg.shape[1]), lambda bi, i: (bi, i, 0)),
                   pl.BlockSpec((1, 1, FOX_HEADS, 1), lambda bi, i: (bi, i, 0, 0)),
                   pl.BlockSpec((1, 1, FOX_HEADS, 1), lambda bi, i: (bi, i, 0, 0)),
                   pl.BlockSpec((1, 1, FOX_HEADS, 1), lambda bi, i: (bi, i, 0, 0))],
        out_shape=[jax.ShapeDtypeStruct((b, FOX_HEADS, FOX_SLOT, s), BF16),
                   jax.ShapeDtypeStruct((b, s, wg.shape[1]), F32),
                   jax.ShapeDtypeStruct((b, s // tm, FOX_HEADS, 1), jnp.int32),
                   jax.ShapeDtypeStruct((b, s // tm, FOX_HEADS, 1), jnp.int32),
                   jax.ShapeDtypeStruct((b, s // tm, FOX_HEADS, 1), F32)],
        compiler_params=_params("parallel", "parallel"),
        name="fox_q",
    )(x, wqt_aug, wg, crelt, spreadt, cblk_t, cend_t, kpm_t)


def _fox_attn_kernel(cblk_ref, need_ref, need_fixed_ref, shift_ref, qt_ref, k_ref, vt_ref, g_ref, o_ref,
                     acc_ref, *bufs, nkv):
    bi = pl.program_id(0)
    hp = pl.program_id(1)
    i = pl.program_id(2)
    tq = qt_ref.shape[3]
    tk = k_ref.shape[3]
    nhp = FOX_HEADS_PER_STEP
    assert nhp == 2
    dh = FOX_HEAD_DIM
    unroll = len(bufs)

    def head_base(hh):
        return (bi * FOX_HEADS + hp * nhp + hh) * nkv

    def scores(hh, j):
        return _dot(k_ref[0, j, hh], qt_ref[0, hh])

    key_pos = lax.broadcasted_iota(jnp.int32, (tk, tq), 0)
    qry_pos = lax.broadcasted_iota(jnp.int32, (tk, tq), 1)
    bound = [shift_ref[head_base(hh) + i] for hh in range(nhp)]
    fixed_shift_ok = jnp.maximum(bound[0], bound[1]) <= MAX_FIXED_SHIFT

    @pl.when(fixed_shift_ok)
    def _():
        for hh in range(nhp):
            p = jnp.where(key_pos <= qry_pos, jnp.exp2(scores(hh, i) - bound[hh]), 0.0)
            acc_ref[hh] = _dot(vt_ref[0, i, hh], p.astype(BF16))

        n0 = need_fixed_ref[head_base(0) + i]
        total = n0 + need_fixed_ref[head_base(1) + i]
        padded = ((total + unroll - 1) // unroll) * unroll

        def item(e):
            hh = (e >= n0).astype(jnp.int32)
            j = jnp.maximum(i - 1 - (e - hh * n0), 0)
            base = head_base(hh)
            shift = cblk_ref[base + i] - cblk_ref[base + j] - shift_ref[base + i]
            return hh, j, jnp.where(e < total, shift, -jnp.inf)

        def probabilities(e, p_ref):
            hh, j, shift = item(e)
            p_ref[...] = jnp.exp2(scores(hh, j) + shift).astype(BF16)

        def accumulate(e, p_ref):
            hh, j, _ = item(e)
            acc_ref[hh] = acc_ref[hh] + _dot(vt_ref[0, j, hh], p_ref[...])

        @pl.when(total > 0)
        def _():
            probabilities(0, bufs[0])

            def body(t, carry):
                e = unroll * t
                for u in range(unroll):
                    probabilities(e + u + 1, bufs[(u + 1) % unroll])
                    accumulate(e + u, bufs[u])
                return carry

            lax.fori_loop(0, padded // unroll - 1, body, 0)
            e = padded - unroll
            for u in range(unroll - 1):
                probabilities(e + u + 1, bufs[u + 1])
                accumulate(e + u, bufs[u])
            accumulate(padded - 1, bufs[unroll - 1])

    @pl.when(jnp.logical_not(fixed_shift_ok))
    def _():
        for hh in range(nhp):
            base = head_base(hh)
            s = jnp.where(key_pos <= qry_pos, scores(hh, i), -jnp.inf)
            m = jnp.max(s, axis=0, keepdims=True)
            acc_ref[hh] = _dot(vt_ref[0, i, hh], jnp.exp2(s - m).astype(BF16))

            def step(jj, m, hh=hh, base=base):
                j = i - 1 - jj
                s = scores(hh, j) + (cblk_ref[base + i] - cblk_ref[base + j])
                m_new = jnp.maximum(m, jnp.max(s, axis=0, keepdims=True))
                acc_ref[hh] = (jnp.exp2(m - m_new) * acc_ref[hh]
                               + _dot(vt_ref[0, j, hh], jnp.exp2(s - m_new).astype(BF16)))
                return m_new

            lax.fori_loop(0, need_ref[base + i], step, m)

    o = jnp.concatenate([acc_ref[hh, :dh, :] / acc_ref[hh, dh:dh + 1, :] for hh in range(nhp)], axis=0)
    o_ref[0] = (jnp.transpose(o) * _sigmoid(g_ref[0])).astype(BF16)


def _fox_attn(cblk, need, need_fixed, shift, qt, k, vt, g):
    b, nkv, nh, tk, slot = k.shape
    v_rows = vt.shape[3]
    s = nkv * tk
    tq = FOX_BLOCK
    nhp = FOX_HEADS_PER_STEP
    out_w = nhp * FOX_HEAD_DIM
    smem = pl.BlockSpec(memory_space=pltpu.SMEM)
    return pl.pallas_call(
        functools.partial(_fox_attn_kernel, nkv=nkv),
        grid=(b, nh // nhp, s // tq),
        in_specs=[smem, smem, smem, smem,
                  pl.BlockSpec((1, nhp, slot, tq), lambda bi, hp, i: (bi, hp, 0, i)),
                  pl.BlockSpec((1, nkv, nhp, tk, slot), lambda bi, hp, i: (bi, 0, hp, 0, 0)),
                  pl.BlockSpec((1, nkv, nhp, v_rows, tk), lambda bi, hp, i: (bi, 0, hp, 0, 0)),
                  pl.BlockSpec((1, tq, out_w), lambda bi, hp, i: (bi, i, hp))],
        out_specs=pl.BlockSpec((1, tq, out_w), lambda bi, hp, i: (bi, i, hp)),
        out_shape=jax.ShapeDtypeStruct((b, s, nh * FOX_HEAD_DIM), BF16),
        scratch_shapes=[pltpu.VMEM((nhp, v_rows, tq), F32)] + [pltpu.VMEM((tk, tq), BF16)] * FOX_ATTN_UNROLL,
        compiler_params=_params("arbitrary", "arbitrary", "arbitrary"),
        name="fox_attn",
    )(cblk, need, need_fixed, shift, qt, k, vt, g)


def _block_tri(n, block):
    r = jnp.arange(n)
    return ((r[:, None] >= r[None, :]) & (r[:, None] // block == r[None, :] // block)).astype(BF16)


def _spread_heads(w):
    d = w.shape[0]
    w = w.reshape(d, FOX_HEADS, FOX_HEAD_DIM)
    w = jnp.pad(w, ((0, 0), (0, 0), (0, FOX_SLOT - FOX_HEAD_DIM)))
    return w.reshape(d, FOX_HEADS * FOX_SLOT)


def _head_group_matrix():
    return (jnp.arange(FOX_HEADS * FOX_SLOT)[:, None] // FOX_SLOT == jnp.arange(FOX_HEADS)[None, :]).astype(BF16)


def _bias_spread_matrix(first_lane):
    rows = jnp.arange(3 * FOX_HEADS)
    term, head = rows // FOX_HEADS, rows % FOX_HEADS
    cols = head * FOX_SLOT + first_lane + term
    return (jnp.arange(FOX_HEADS * FOX_SLOT)[None, :] == cols[:, None]).astype(BF16)


def kernel(x, p, ffn1_w_in, ffn1_w_out, ln1_g, ln1_b, gla_w_in, gla_w_a2, gla_b_a, gla_gn_g, gla_gn_b, gla_w_o, fox_w_kvf, fox_b_f, fox_w_in, fox_w_o, ln2_g, ln2_b, ffn2_w_in, ffn2_w_out, ple_w_gate, ple_w_proj, ln3_g, ln3_b):
    bsz, seq, d = x.shape
    depth = ffn1_w_in.shape[0]
    n_gla = gla_w_in.shape[0]
    alpha = (2 * depth) ** 0.25
    t = bsz * seq
    fox_w = FOX_HEADS * FOX_HEAD_DIM
    row = lambda a: a.reshape(1, -1)

    tri_chunk = _block_tri(TRI_BLOCK, GLA_CHUNK)
    tri_block = _block_tri(FOX_BLOCK, FOX_BLOCK)
    key_bias_spread = _bias_spread_matrix(FOX_HEAD_DIM)
    query_bias_spread_t = _bias_spread_matrix(FOX_HEAD_DIM + 3).T

    xf = x.reshape(t, d)
    pf = p.reshape(depth, t, p.shape[-1])
    heads_major = lambda a: jnp.transpose(a[:, :, 0, :], (0, 2, 1))
    fox_k = fox_vt = fox_crelt = cblk_t = cend_t = kpm_t = None
    for i in range(depth):
        if i == n_gla:
            wk_aug = _spread_heads(fox_w_kvf[:, :fox_w]).astype(BF16)
            wvt = fox_w_kvf[:, fox_w:2 * fox_w].T.astype(BF16)
            wf = fox_w_kvf[:, 2 * fox_w:].astype(BF16)
            fox_k, fox_vt, fox_crelt, cblk, cend, kpm = _fox_kv(
                xf.reshape(bsz, seq, d), wk_aug, wvt, wf, row(fox_b_f), tri_block, key_bias_spread,
                _head_group_matrix(), jnp.eye(3 * FOX_HEADS, dtype=BF16))
            cblk_t, cend_t, kpm_t = heads_major(cblk), heads_major(cend), heads_major(kpm)
        xf = _ffn(xf, ffn1_w_in[i].astype(BF16), ffn1_w_out[i].astype(BF16), row(ln1_g[i]), row(ln1_b[i]), alpha)
        if i < n_gla:
            qd, ki, ks, v, dec, gate = _gla_proj(xf, gla_w_in[i].astype(BF16), gla_w_a2[i].astype(BF16),
                                                 row(gla_b_a[i]), tri_chunk)
            b3 = lambda a: a.reshape(bsz, seq, -1)
            mix = _gla_rec(b3(qd), b3(ki), b3(ks), b3(v), dec.reshape(bsz, seq // GLA_CHUNK, 1, -1), b3(gate),
                           row(gla_gn_g[i]), row(gla_gn_b[i])).reshape(t, -1)
            w_o = gla_w_o[i]
        else:
            j = i - n_gla
            wqt_aug = _spread_heads(fox_w_in[j][:, :fox_w]).T.astype(BF16)
            qt, g, need, need_fixed, shift = _fox_q(xf.reshape(bsz, seq, d), wqt_aug, fox_w_in[j][:, fox_w:].astype(BF16),
                                        fox_crelt, query_bias_spread_t, cblk_t, cend_t, kpm_t)
            flat = lambda a: jnp.transpose(a[:, :, :, 0], (0, 2, 1)).reshape(-1)
            mix = _fox_attn(cblk_t.reshape(-1), flat(need), flat(need_fixed), flat(shift), qt, fox_k, fox_vt,
                            g).reshape(t, -1)
            w_o = fox_w_o[j]
        xf = _ffn(xf, ffn2_w_in[i].astype(BF16), ffn2_w_out[i].astype(BF16), row(ln3_g[i]), row(ln3_b[i]), alpha,
                  mixer=(mix, w_o.astype(BF16), row(ln2_g[i]), row(ln2_b[i])),
                  ple=(pf[i], ple_w_gate[i].astype(BF16), ple_w_proj[i].astype(BF16)))
    return xf.reshape(bsz, seq, d)
```

```python
import functools
import math

import jax
import jax.numpy as jnp
from jax import lax
from jax.experimental import pallas as pl
from jax.experimental.pallas import tpu as pltpu

F32 = jnp.float32
BF16 = jnp.bfloat16

GLA_HEADS = 4
GLA_TAU = 16.0
GLA_CHUNK = 64
FOX_HEADS = 16
FOX_HEAD_DIM = 64
LN_EPS = 1e-5

V7X_VMEM_LIMIT_BYTES = 56 * 1024 * 1024

TOKEN_TILE = 512
FFN_TOKEN_TILE = 512
FOX_BLOCK = 512
FOX_SLOT = 128
FOX_HEADS_PER_STEP = 2
TRI_BLOCK = 256
LOG2E = math.log2(math.e)
EXP2_ZERO_GAP = 152.0
NORM_SLACK = 1.01
FOX_V_ROWS = 80
FOX_ATTN_UNROLL = 4
MAX_FIXED_SHIFT = 50.0


def _params(*sems):
    return pltpu.CompilerParams(dimension_semantics=sems, vmem_limit_bytes=V7X_VMEM_LIMIT_BYTES)


def _resident(shape):
    zeros = (0,) * len(shape)
    return pl.BlockSpec(shape, lambda *_: zeros, pipeline_mode=pl.Buffered(1))


def _dot(a, b):
    return jnp.dot(a, b, preferred_element_type=F32)


def _dot_nt(a, b):
    return lax.dot_general(a, b, (((1,), (1,)), ((), ())), preferred_element_type=F32)


def _split3(a):
    hi = a.astype(BF16)
    r1 = a - hi.astype(F32)
    mid = r1.astype(BF16)
    lo = (r1 - mid.astype(F32)).astype(BF16)
    return hi, mid, lo


def _tri_cumsum(tri, a):
    hi, mid, lo = _split3(a)
    return _dot(tri, hi) + _dot(tri, mid) + _dot(tri, lo)


def _log_sigmoid(z):
    return jnp.minimum(z, 0.0) - jnp.log1p(jnp.exp(-jnp.abs(z)))


def _sigmoid(z):
    return 1.0 / (1.0 + jnp.exp(-z))


def _layer_norm(z, g, b):
    mu = jnp.mean(z, axis=-1, keepdims=True)
    zc = z - mu
    var = jnp.mean(zc * zc, axis=-1, keepdims=True)
    return zc * lax.rsqrt(var + LN_EPS) * g + b


def _ffn_kernel(*refs, alpha, d_ff, after_mixer):
    if after_mixer:
        (x_ref, mix_ref, wo_ref, g2_ref, b2_ref, win_ref, wout_ref, g_ref, b_ref,
         p_ref, wgate_ref, wproj_ref, o_ref) = refs
        x = _layer_norm(alpha * x_ref[...] + _dot(mix_ref[...], wo_ref[...]), g2_ref[...], b2_ref[...])
    else:
        x_ref, win_ref, wout_ref, g_ref, b_ref, o_ref = refs
        x = x_ref[...]
    xb = x.astype(BF16)
    gate = _dot(xb, win_ref[:, :d_ff])
    up = _dot(xb, win_ref[:, d_ff:])
    h = (gate * _sigmoid(gate) * up).astype(BF16)
    z = alpha * x + 0.5 * _dot(h, wout_ref[...])
    if after_mixer:
        ple_gate = _sigmoid(_dot(xb, wgate_ref[...]))
        z = z + ple_gate * _dot(p_ref[...].astype(BF16), wproj_ref[...])
    o_ref[...] = _layer_norm(z, g_ref[...], b_ref[...])


def _ffn(x, w_in, w_out, ln_g, ln_b, alpha, mixer=None, ple=None):
    assert (mixer is None) == (ple is None)
    t, d = x.shape
    d_ff = w_out.shape[0]
    tm = FFN_TOKEN_TILE
    row = lambda i: (i, 0)
    in_specs = [pl.BlockSpec((tm, d), row)]
    args = [x]
    if mixer is not None:
        mix, w_o, ln2_g, ln2_b = mixer
        in_specs += [pl.BlockSpec((tm, mix.shape[1]), row), _resident(w_o.shape), _resident(ln2_g.shape),
                     _resident(ln2_b.shape)]
        args += [mix, w_o, ln2_g, ln2_b]
    in_specs += [_resident(w_in.shape), _resident(w_out.shape), _resident(ln_g.shape), _resident(ln_b.shape)]
    args += [w_in, w_out, ln_g, ln_b]
    if ple is not None:
        p, w_gate, w_proj = ple
        in_specs += [pl.BlockSpec((tm, p.shape[1]), row), _resident(w_gate.shape), _resident(w_proj.shape)]
        args += [p, w_gate, w_proj]
    return pl.pallas_call(
        functools.partial(_ffn_kernel, alpha=alpha, d_ff=d_ff, after_mixer=mixer is not None),
        grid=(t // tm,),
        in_specs=in_specs,
        out_specs=pl.BlockSpec((tm, d), row),
        out_shape=jax.ShapeDtypeStruct((t, d), F32),
        compiler_params=_params("parallel"),
        name="ffn_after_mixer" if mixer is not None else "ffn",
    )(*args)


def _gla_proj_kernel(x_ref, win_ref, wa2_ref, ba_ref, tri_ref,
                     qd_ref, ki_ref, ks_ref, v_ref, dec_ref, gate_ref, *, kd, vd):
    tm = x_ref.shape[0]
    xb = x_ref[...].astype(BF16)
    proj = _dot(xb, win_ref[...])
    q = proj[:, :kd] * ((kd // GLA_HEADS) ** -0.5)
    k = proj[:, kd:2 * kd]
    v = proj[:, 2 * kd:2 * kd + vd]
    r = proj[:, 2 * kd + vd:2 * kd + 2 * vd]
    a_lr = proj[:, 2 * kd + 2 * vd:]
    log_a = _log_sigmoid(_dot(a_lr.astype(BF16), wa2_ref[...]) + ba_ref[...]) * (1.0 / GLA_TAU)
    tri = tri_ref[...]
    bcum = jnp.concatenate(
        [_tri_cumsum(tri, log_a[s:s + TRI_BLOCK]) for s in range(0, tm, TRI_BLOCK)], axis=0)
    nc = tm // GLA_CHUNK
    b3 = bcum.reshape(nc, GLA_CHUNK, kd)
    b_last = b3[:, GLA_CHUNK - 1:GLA_CHUNK, :]
    qd_ref[...] = (q * jnp.exp(bcum)).astype(BF16)
    ki_ref[...] = (k * jnp.exp(-bcum)).astype(BF16)
    ks_ref[...] = (k.reshape(nc, GLA_CHUNK, kd) * jnp.exp(b_last - b3)).reshape(tm, kd).astype(BF16)
    dec_ref[...] = jnp.exp(b_last)
    v_ref[...] = v.astype(BF16)
    gate_ref[...] = r * _sigmoid(r)


def _gla_proj(x, w_in, w_a2, b_a, tri):
    t, d = x.shape
    kd = w_a2.shape[1]
    vd = (w_in.shape[1] - 2 * kd - w_a2.shape[0]) // 2
    tm = TOKEN_TILE
    row = lambda i: (i, 0)
    return pl.pallas_call(
        functools.partial(_gla_proj_kernel, kd=kd, vd=vd),
        grid=(t // tm,),
        in_specs=[pl.BlockSpec((tm, d), row), _resident(w_in.shape), _resident(w_a2.shape),
                  _resident(b_a.shape), _resident(tri.shape)],
        out_specs=[pl.BlockSpec((tm, kd), row), pl.BlockSpec((tm, kd), row), pl.BlockSpec((tm, kd), row),
                   pl.BlockSpec((tm, vd), row), pl.BlockSpec((tm // GLA_CHUNK, 1, kd), lambda i: (i, 0, 0)),
                   pl.BlockSpec((tm, vd), row)],
        out_shape=[jax.ShapeDtypeStruct((t, kd), BF16)] * 3 + [
            jax.ShapeDtypeStruct((t, vd), BF16),
            jax.ShapeDtypeStruct((t // GLA_CHUNK, 1, kd), F32),
            jax.ShapeDtypeStruct((t, vd), F32)],
        compiler_params=_params("parallel"),
        name="gla_proj",
    )(x, w_in, w_a2, b_a, tri)


def _gla_rec_kernel(qd_ref, ki_ref, ks_ref, v_ref, dec_ref, gate_ref, gng_ref, gnb_ref, o_ref, st_ref):
    tm = qd_ref.shape[1]
    dk = qd_ref.shape[2] // GLA_HEADS
    dv = v_ref.shape[2] // GLA_HEADS

    @pl.when(pl.program_id(1) == 0)
    def _():
        st_ref[...] = jnp.zeros_like(st_ref)

    rows = lax.broadcasted_iota(jnp.int32, (GLA_CHUNK, GLA_CHUNK), 0)
    cols = lax.broadcasted_iota(jnp.int32, (GLA_CHUNK, GLA_CHUNK), 1)
    causal = cols <= rows

    def chunk(c, carry):
        r0 = pl.multiple_of(c * GLA_CHUNK, GLA_CHUNK)
        rs = pl.ds(r0, GLA_CHUNK)
        for h in range(GLA_HEADS):
            ksl = slice(h * dk, (h + 1) * dk)
            vsl = slice(h * dv, (h + 1) * dv)
            qd = qd_ref[0, rs, ksl]
            ki = ki_ref[0, rs, ksl]
            ks = ks_ref[0, rs, ksl]
            vv = v_ref[0, rs, vsl]
            attn = jnp.where(causal, _dot_nt(qd, ki), 0.0).astype(BF16)
            st = st_ref[h]
            o = _dot(attn, vv) + _dot_nt(qd, st.astype(BF16))
            vt = jnp.transpose(vv.astype(F32)).astype(BF16)
            st_ref[h] = st * dec_ref[0, c, :, ksl] + _dot(vt, ks)
            on = _layer_norm(o, gng_ref[:, vsl], gnb_ref[:, vsl])
            o_ref[0, rs, vsl] = (on * gate_ref[0, rs, vsl]).astype(BF16)
        return carry

    lax.fori_loop(0, tm // GLA_CHUNK, chunk, 0, unroll=True)


def _gla_rec(qd, ki, ks, v, dec, gate, gn_g, gn_b):
    b, s, kd = qd.shape
    vd = v.shape[2]
    tm = TOKEN_TILE
    blk = lambda bi, i: (bi, i, 0)
    return pl.pallas_call(
        _gla_rec_kernel,
        grid=(b, s // tm),
        in_specs=[pl.BlockSpec((1, tm, kd), blk), pl.BlockSpec((1, tm, kd), blk), pl.BlockSpec((1, tm, kd), blk),
                  pl.BlockSpec((1, tm, vd), blk),
                  pl.BlockSpec((1, tm // GLA_CHUNK, 1, kd), lambda bi, i: (bi, i, 0, 0)),
                  pl.BlockSpec((1, tm, vd), blk), _resident(gn_g.shape), _resident(gn_b.shape)],
        out_specs=pl.BlockSpec((1, tm, vd), blk),
        out_shape=jax.ShapeDtypeStruct((b, s, vd), BF16),
        scratch_shapes=[pltpu.VMEM((GLA_HEADS, vd // GLA_HEADS, kd // GLA_HEADS), F32)],
        compiler_params=_params("arbitrary", "arbitrary"),
        name="gla_rec",
    )(qd, ki, ks, v, dec, gate, gn_g, gn_b)


def _fox_kv_kernel(x_ref, wk_ref, wvt_ref, wf_ref, bf_ref, tri_ref, spread_ref, group_ref, eye_ref,
                   k_ref, vt_ref, crelt_ref, cblk_ref, cend_ref, kpm_ref, carry_ref, kmax_ref):
    @pl.when(pl.program_id(1) == 0)
    def _():
        carry_ref[...] = jnp.zeros_like(carry_ref)
        kmax_ref[...] = jnp.zeros_like(kmax_ref)

    tk = x_ref.shape[1]
    xb = x_ref[0].astype(BF16)
    log_f = _log_sigmoid(_dot(xb, wf_ref[...]) + bf_ref[...])
    c_rel = _tri_cumsum(tri_ref[...], log_f) * LOG2E
    terms = jnp.concatenate(_split3(-c_rel), axis=1)
    bias = _dot(terms, spread_ref[...])
    crelt_ref[0, 0] = -_dot_nt(eye_ref[...], terms)
    k = _dot(xb, wk_ref[...])
    slot_lane = lax.broadcasted_iota(jnp.int32, k.shape, 1) % FOX_SLOT
    is_one_lane = (slot_lane >= FOX_HEAD_DIM + 3) & (slot_lane < FOX_HEAD_DIM + 6)
    k_aug = jnp.where(is_one_lane, 1.0, k + bias).astype(BF16)
    for h in range(FOX_HEADS):
        k_ref[0, 0, h] = k_aug[:, h * FOX_SLOT:(h + 1) * FOX_SLOT]
    vt = _dot_nt(wvt_ref[...], xb).astype(BF16).reshape(FOX_HEADS, FOX_HEAD_DIM, tk)
    pad_row = lax.broadcasted_iota(jnp.int32, (FOX_HEADS, FOX_V_ROWS - FOX_HEAD_DIM, tk), 1)
    vt_ref[0, 0] = jnp.concatenate([vt, jnp.where(pad_row == 0, 1.0, 0.0).astype(BF16)], axis=1)
    cblk_ref[0, 0] = carry_ref[...]
    carry_ref[...] = carry_ref[...] + c_rel[tk - 1:tk, :]
    cend_ref[0, 0] = carry_ref[...]
    kr = k.astype(BF16).astype(F32)
    norm2 = _dot((kr * kr).astype(BF16), group_ref[...]) * NORM_SLACK
    kmax_ref[...] = jnp.maximum(kmax_ref[...], jnp.sqrt(jnp.max(norm2, axis=0, keepdims=True)))
    kpm_ref[0, 0] = kmax_ref[...]


def _fox_kv(x, wk_aug, wvt, wf, b_f, tri, spread, group, eye):
    b, s, d = x.shape
    tk = FOX_BLOCK
    nkv = s // tk
    nh = wf.shape[1]
    blk4 = lambda bi, j: (bi, j, 0, 0)
    blk5 = lambda bi, j: (bi, j, 0, 0, 0)
    small = jax.ShapeDtypeStruct((b, nkv, 1, nh), F32)
    return pl.pallas_call(
        _fox_kv_kernel,
        grid=(b, nkv),
        in_specs=[pl.BlockSpec((1, tk, d), lambda bi, j: (bi, j, 0)), _resident(wk_aug.shape),
                  _resident(wvt.shape), _resident(wf.shape), _resident(b_f.shape), _resident(tri.shape),
                  _resident(spread.shape), _resident(group.shape), _resident(eye.shape)],
        out_specs=[pl.BlockSpec((1, 1, nh, tk, FOX_SLOT), blk5),
                   pl.BlockSpec((1, 1, nh, FOX_V_ROWS, tk), blk5),
                   pl.BlockSpec((1, 1, 3 * nh, tk), blk4),
                   pl.BlockSpec((1, 1, 1, nh), blk4), pl.BlockSpec((1, 1, 1, nh), blk4),
                   pl.BlockSpec((1, 1, 1, nh), blk4)],
        out_shape=[jax.ShapeDtypeStruct((b, nkv, nh, tk, FOX_SLOT), BF16),
                   jax.ShapeDtypeStruct((b, nkv, nh, FOX_V_ROWS, tk), BF16),
                   jax.ShapeDtypeStruct((b, nkv, 3 * nh, tk), F32),
                   small, small, small],
        scratch_shapes=[pltpu.VMEM((1, nh), F32), pltpu.VMEM((1, nh), F32)],
        compiler_params=_params("arbitrary", "arbitrary"),
        name="fox_kv",
    )(x, wk_aug, wvt, wf, b_f, tri, spread, group, eye)


def _fox_q_kernel(x_ref, wqt_ref, wg_ref, crelt_ref, spreadt_ref, cblk_ref, cend_ref, kpm_ref,
                  qt_ref, g_ref, need_ref, need_fixed_ref, shift_ref):
    i = pl.program_id(1)
    tq = x_ref.shape[1]
    xb = x_ref[0].astype(BF16)
    q = _dot_nt(wqt_ref[...], xb) * (FOX_HEAD_DIM ** -0.5 * LOG2E)
    q = q.reshape(FOX_HEADS, FOX_HEAD_DIM, tq)
    qt = jnp.concatenate([q, jnp.zeros((FOX_HEADS, FOX_SLOT - FOX_HEAD_DIM, tq), F32)], axis=1)
    qt = qt.reshape(FOX_HEADS * FOX_SLOT, tq)
    c_rows = _dot(spreadt_ref[...], crelt_ref[0, 0].astype(BF16))
    qt3 = (qt + c_rows).reshape(FOX_HEADS, FOX_SLOT, tq)
    slot_row = lax.broadcasted_iota(jnp.int32, qt3.shape, 1)
    is_one_row = (slot_row >= FOX_HEAD_DIM) & (slot_row < FOX_HEAD_DIM + 3)
    qt_ref[0] = jnp.where(is_one_row, 1.0, qt3).astype(BF16)
    g_ref[0] = _dot(xb, wg_ref[...])

    qr = qt.astype(BF16).astype(F32).reshape(FOX_HEADS, FOX_SLOT, tq)
    qn = jnp.sqrt(jnp.max(jnp.sum(qr * qr, axis=1), axis=1, keepdims=True))
    blk = lax.broadcasted_iota(jnp.int32, cblk_ref.shape[1:], 1)
    at_i = blk == i
    c_before = jnp.sum(jnp.where(at_i, cblk_ref[0], 0.0), axis=1, keepdims=True)
    k_norm = jnp.sum(jnp.where(at_i, kpm_ref[0], 0.0), axis=1, keepdims=True)
    qk_bound = NORM_SLACK * qn * k_norm

    def count_blocks(thr):
        needed = (blk < i) & jnp.logical_not(cend_ref[0] > thr)
        return jnp.sum(jnp.where(needed, 1.0, 0.0), axis=1, keepdims=True).astype(jnp.int32)

    need_ref[0, 0] = count_blocks(c_before + EXP2_ZERO_GAP + 2.0 * qk_bound)
    need_fixed_ref[0, 0] = count_blocks(c_before + EXP2_ZERO_GAP)
    shift_ref[0, 0] = qk_bound


def _fox_q(x, wqt, wg, crelt, spreadt, cblk_t, cend_t, kpm_t):
    b, s, d = x.shape
    tm = FOX_BLOCK
    nkv = cblk_t.shape[2]
    per_batch = pl.BlockSpec((1, FOX_HEADS, nkv), lambda bi, i: (bi, 0, 0))
    return pl.pallas_call(
        _fox_q_kernel,
        grid=(b, s // tm),
        in_specs=[pl.BlockSpec((1, tm, d), lambda bi, i: (bi, i, 0)), _resident(wqt.shape),
                  _resident(wg.shape),
                  pl.BlockSpec((1, 1, crelt.shape[2], tm), lambda bi, i: (bi, i, 0, 0)),
                  _resident(spreadt.shape), per_batch, per_batch, per_batch],
        out_specs=[pl.BlockSpec((1, FOX_HEADS, FOX_SLOT, tm), lambda bi, i: (bi, 0, 0, i)),
                   pl.BlockSpec((1, tm, wg.shape[1]), lambda bi, i: (bi, i, 0)),
                   pl.BlockSpec((1, 1, FOX_HEADS, 1), lambda bi, i: (bi, i, 0, 0)),
                   pl.BlockSpec((1, 1, FOX_HEADS, 1), lambda bi, i: (bi, i, 0, 0)),
                   pl.BlockSpec((1, 1, FOX_HEADS, 1), lambda bi, i: (bi, i, 0, 0))],
        out_shape=[jax.ShapeDtypeStruct((b, FOX_HEADS, FOX_SLOT, s), BF16),
                   jax.ShapeDtypeStruct((b, s, wg.shape[1]), F32),
                   jax.ShapeDtypeStruct((b, s // tm, FOX_HEADS, 1), jnp.int32),
                   jax.ShapeDtypeStruct((b, s // tm, FOX_HEADS, 1), jnp.int32),
                   jax.ShapeDtypeStruct((b, s // tm, FOX_HEADS, 1), F32)],
        compiler_params=_params("parallel", "parallel"),
        name="fox_q",
    )(x, wqt, wg, crelt, spreadt, cblk_t, cend_t, kpm_t)


def _fox_attn_kernel(cblk_ref, need_ref, need_fixed_ref, shift_ref, qt_ref, k_ref, vt_ref, g_ref, o_ref,
                     acc_ref, *bufs, nkv):
    bi = pl.program_id(0)
    hp = pl.program_id(1)
    i = pl.program_id(2)
    tq = qt_ref.shape[3]
    tk = k_ref.shape[3]
    nhp = FOX_HEADS_PER_STEP
    assert nhp == 2
    dh = FOX_HEAD_DIM
    unroll = len(bufs)

    def head_base(hh):
        return (bi * FOX_HEADS + hp * nhp + hh) * nkv

    def scores(hh, j):
        return _dot(k_ref[0, j, hh], qt_ref[0, hh])

    key_pos = lax.broadcasted_iota(jnp.int32, (tk, tq), 0)
    qry_pos = lax.broadcasted_iota(jnp.int32, (tk, tq), 1)
    bound = [shift_ref[head_base(hh) + i] for hh in range(nhp)]
    fixed_shift_ok = jnp.maximum(bound[0], bound[1]) <= MAX_FIXED_SHIFT

    @pl.when(fixed_shift_ok)
    def _():
        for hh in range(nhp):
            p = jnp.where(key_pos <= qry_pos, jnp.exp2(scores(hh, i) - bound[hh]), 0.0)
            acc_ref[hh] = _dot(vt_ref[0, i, hh], p.astype(BF16))

        n0 = need_fixed_ref[head_base(0) + i]
        total = n0 + need_fixed_ref[head_base(1) + i]
        padded = ((total + unroll - 1) // unroll) * unroll

        def item(e):
            hh = (e >= n0).astype(jnp.int32)
            j = jnp.maximum(i - 1 - (e - hh * n0), 0)
            base = head_base(hh)
            shift = cblk_ref[base + i] - cblk_ref[base + j] - shift_ref[base + i]
            return hh, j, jnp.where(e < total, shift, -jnp.inf)

        def probabilities(e, p_ref):
            hh, j, shift = item(e)
            p_ref[...] = jnp.exp2(scores(hh, j) + shift).astype(BF16)

        def accumulate(e, p_ref):
            hh, j, _ = item(e)
            acc_ref[hh] = acc_ref[hh] + _dot(vt_ref[0, j, hh], p_ref[...])

        def group(t, carry):
            for u in range(unroll):
                probabilities(unroll * t + u, bufs[u])
            for u in range(unroll):
                accumulate(unroll * t + u, bufs[u])
            return carry

        lax.fori_loop(0, padded // unroll, group, 0)

    @pl.when(jnp.logical_not(fixed_shift_ok))
    def _():
        for hh in range(nhp):
            base = head_base(hh)
            s = jnp.where(key_pos <= qry_pos, scores(hh, i), -jnp.inf)
            m = jnp.max(s, axis=0, keepdims=True)
            acc_ref[hh] = _dot(vt_ref[0, i, hh], jnp.exp2(s - m).astype(BF16))

            def step(jj, m, hh=hh, base=base):
                j = i - 1 - jj
                s = scores(hh, j) + (cblk_ref[base + i] - cblk_ref[base + j])
                m_new = jnp.maximum(m, jnp.max(s, axis=0, keepdims=True))
                acc_ref[hh] = (jnp.exp2(m - m_new) * acc_ref[hh]
                               + _dot(vt_ref[0, j, hh], jnp.exp2(s - m_new).astype(BF16)))
                return m_new

            lax.fori_loop(0, need_ref[base + i], step, m)

    o = jnp.concatenate([acc_ref[hh, :dh, :] / acc_ref[hh, dh:dh + 1, :] for hh in range(nhp)], axis=0)
    o_ref[0] = (jnp.transpose(o) * _sigmoid(g_ref[0])).astype(BF16)


def _fox_attn(cblk, need, need_fixed, shift, qt, k, vt, g):
    b, nkv, nh, tk, slot = k.shape
    v_rows = vt.shape[3]
    s = nkv * tk
    tq = FOX_BLOCK
    nhp = FOX_HEADS_PER_STEP
    out_w = nhp * FOX_HEAD_DIM
    smem = pl.BlockSpec(memory_space=pltpu.SMEM)
    return pl.pallas_call(
        functools.partial(_fox_attn_kernel, nkv=nkv),
        grid=(b, nh // nhp, s // tq),
        in_specs=[smem, smem, smem, smem,
                  pl.BlockSpec((1, nhp, slot, tq), lambda bi, hp, i: (bi, hp, 0, i)),
                  pl.BlockSpec((1, nkv, nhp, tk, slot), lambda bi, hp, i: (bi, 0, hp, 0, 0)),
                  pl.BlockSpec((1, nkv, nhp, v_rows, tk), lambda bi, hp, i: (bi, 0, hp, 0, 0)),
                  pl.BlockSpec((1, tq, out_w), lambda bi, hp, i: (bi, i, hp))],
        out_specs=pl.BlockSpec((1, tq, out_w), lambda bi, hp, i: (bi, i, hp)),
        out_shape=jax.ShapeDtypeStruct((b, s, nh * FOX_HEAD_DIM), BF16),
        scratch_shapes=[pltpu.VMEM((nhp, v_rows, tq), F32)] + [pltpu.VMEM((tk, tq), BF16)] * FOX_ATTN_UNROLL,
        compiler_params=_params("arbitrary", "arbitrary", "arbitrary"),
        name="fox_attn",
    )(cblk, need, need_fixed, shift, qt, k, vt, g)


def _block_tri(n, block):
    r = jnp.arange(n)
    return ((r[:, None] >= r[None, :]) & (r[:, None] // block == r[None, :] // block)).astype(BF16)


def _spread_heads(w):
    d = w.shape[0]
    w = w.reshape(d, FOX_HEADS, FOX_HEAD_DIM)
    w = jnp.pad(w, ((0, 0), (0, 0), (0, FOX_SLOT - FOX_HEAD_DIM)))
    return w.reshape(d, FOX_HEADS * FOX_SLOT)


def _head_group_matrix():
    return (jnp.arange(FOX_HEADS * FOX_SLOT)[:, None] // FOX_SLOT == jnp.arange(FOX_HEADS)[None, :]).astype(BF16)


def _bias_spread_matrix(first_lane):
    rows = jnp.arange(3 * FOX_HEADS)
    term, head = rows // FOX_HEADS, rows % FOX_HEADS
    cols = head * FOX_SLOT + first_lane + term
    return (jnp.arange(FOX_HEADS * FOX_SLOT)[None, :] == cols[:, None]).astype(BF16)


def kernel(x, p, ffn1_w_in, ffn1_w_out, ln1_g, ln1_b, gla_w_in, gla_w_a2, gla_b_a, gla_gn_g, gla_gn_b, gla_w_o, fox_w_kvf, fox_b_f, fox_w_in, fox_w_o, ln2_g, ln2_b, ffn2_w_in, ffn2_w_out, ple_w_gate, ple_w_proj, ln3_g, ln3_b):
    bsz, seq, d = x.shape
    depth = ffn1_w_in.shape[0]
    n_gla = gla_w_in.shape[0]
    alpha = (2 * depth) ** 0.25
    t = bsz * seq
    fox_w = FOX_HEADS * FOX_HEAD_DIM
    row = lambda a: a.reshape(1, -1)

    tri_chunk = _block_tri(TRI_BLOCK, GLA_CHUNK)
    tri_block = _block_tri(FOX_BLOCK, FOX_BLOCK)
    key_bias_spread = _bias_spread_matrix(FOX_HEAD_DIM)
    query_bias_spread_t = _bias_spread_matrix(FOX_HEAD_DIM + 3).T

    xf = x.reshape(t, d)
    pf = p.reshape(depth, t, p.shape[-1])
    heads_major = lambda a: jnp.transpose(a[:, :, 0, :], (0, 2, 1))
    fox_k = fox_vt = fox_crelt = cblk_t = cend_t = kpm_t = None
    for i in range(depth):
        if i == n_gla:
            wk_aug = _spread_heads(fox_w_kvf[:, :fox_w]).astype(BF16)
            wvt = fox_w_kvf[:, fox_w:2 * fox_w].T.astype(BF16)
            wf = fox_w_kvf[:, 2 * fox_w:].astype(BF16)
            fox_k, fox_vt, fox_crelt, cblk, cend, kpm = _fox_kv(
                xf.reshape(bsz, seq, d), wk_aug, wvt, wf, row(fox_b_f), tri_block, key_bias_spread,
                _head_group_matrix(), jnp.eye(3 * FOX_HEADS, dtype=BF16))
            cblk_t, cend_t, kpm_t = heads_major(cblk), heads_major(cend), heads_major(kpm)
        xf = _ffn(xf, ffn1_w_in[i].astype(BF16), ffn1_w_out[i].astype(BF16), row(ln1_g[i]), row(ln1_b[i]), alpha)
        if i < n_gla:
            qd, ki, ks, v, dec, gate = _gla_proj(xf, gla_w_in[i].astype(BF16), gla_w_a2[i].astype(BF16),
                                                 row(gla_b_a[i]), tri_chunk)
            b3 = lambda a: a.reshape(bsz, seq, -1)
            mix = _gla_rec(b3(qd), b3(ki), b3(ks), b3(v), dec.reshape(bsz, seq // GLA_CHUNK, 1, -1), b3(gate),
                           row(gla_gn_g[i]), row(gla_gn_b[i])).reshape(t, -1)
            w_o = gla_w_o[i]
        else:
            j = i - n_gla
            wqt = fox_w_in[j][:, :fox_w].T.astype(BF16)
            qt, g, need, need_fixed, shift = _fox_q(xf.reshape(bsz, seq, d), wqt, fox_w_in[j][:, fox_w:].astype(BF16),
                                        fox_crelt, query_bias_spread_t, cblk_t, cend_t, kpm_t)
            flat = lambda a: jnp.transpose(a[:, :, :, 0], (0, 2, 1)).reshape(-1)
            mix = _fox_attn(cblk_t.reshape(-1), flat(need), flat(need_fixed), flat(shift), qt, fox_k, fox_vt,
                            g).reshape(t, -1)
            w_o = fox_w_o[j]
        xf = _ffn(xf, ffn2_w_in[i].astype(BF16), ffn2_w_out[i].astype(BF16), row(ln3_g[i]), row(ln3_b[i]), alpha,
                  mixer=(mix, w_o.astype(BF16), row(ln2_g[i]), row(ln2_b[i])),
                  ple=(pf[i], ple_w_gate[i].astype(BF16), ple_w_proj[i].astype(BF16)))
    return xf.reshape(bsz, seq, d)
```

```python
import functools
import math

import jax
import jax.numpy as jnp
from jax import lax
from jax.experimental import pallas as pl
from jax.experimental.pallas import tpu as pltpu

F32 = jnp.float32
BF16 = jnp.bfloat16

GLA_HEADS = 4
GLA_TAU = 16.0
GLA_CHUNK = 64
FOX_HEADS = 16
FOX_HEAD_DIM = 64
LN_EPS = 1e-5

V7X_VMEM_LIMIT_BYTES = 56 * 1024 * 1024

TOKEN_TILE = 512
FFN_TOKEN_TILE = 512
FOX_BLOCK = 512
FOX_SLOT = 128
FOX_HEADS_PER_STEP = 2
TRI_BLOCK = 256
LOG2E = math.log2(math.e)
EXP2_ZERO_GAP = 136.0
NORM_SLACK = 1.01
FOX_V_ROWS = 80
FOX_ATTN_UNROLL = 4
MAX_FIXED_SHIFT = 50.0


def _params(*sems):
    return pltpu.CompilerParams(dimension_semantics=sems, vmem_limit_bytes=V7X_VMEM_LIMIT_BYTES)


def _resident(shape):
    zeros = (0,) * len(shape)
    return pl.BlockSpec(shape, lambda *_: zeros, pipeline_mode=pl.Buffered(1))


def _dot(a, b):
    return jnp.dot(a, b, preferred_element_type=F32)


def _dot_nt(a, b):
    return lax.dot_general(a, b, (((1,), (1,)), ((), ())), preferred_element_type=F32)


def _split3(a):
    hi = a.astype(BF16)
    r1 = a - hi.astype(F32)
    mid = r1.astype(BF16)
    lo = (r1 - mid.astype(F32)).astype(BF16)
    return hi, mid, lo


def _tri_cumsum(tri, a):
    hi, mid, lo = _split3(a)
    return _dot(tri, hi) + _dot(tri, mid) + _dot(tri, lo)


def _log_sigmoid(z):
    return jnp.minimum(z, 0.0) - jnp.log1p(jnp.exp(-jnp.abs(z)))


def _sigmoid(z):
    return 1.0 / (1.0 + jnp.exp(-z))


def _layer_norm(z, g, b):
    mu = jnp.mean(z, axis=-1, keepdims=True)
    zc = z - mu
    var = jnp.mean(zc * zc, axis=-1, keepdims=True)
    return zc * lax.rsqrt(var + LN_EPS) * g + b


def _ffn_kernel(*refs, alpha, d_ff, after_mixer):
    if after_mixer:
        (x_ref, mix_ref, wo_ref, g2_ref, b2_ref, win_ref, wout_ref, g_ref, b_ref,
         p_ref, wgate_ref, wproj_ref, o_ref) = refs
        x = _layer_norm(alpha * x_ref[...] + _dot(mix_ref[...], wo_ref[...]), g2_ref[...], b2_ref[...])
    else:
        x_ref, win_ref, wout_ref, g_ref, b_ref, o_ref = refs
        x = x_ref[...]
    xb = x.astype(BF16)
    gate = _dot(xb, win_ref[:, :d_ff])
    up = _dot(xb, win_ref[:, d_ff:])
    h = (gate * _sigmoid(gate) * up).astype(BF16)
    z = alpha * x + 0.5 * _dot(h, wout_ref[...])
    if after_mixer:
        ple_gate = _sigmoid(_dot(xb, wgate_ref[...]))
        z = z + ple_gate * _dot(p_ref[...].astype(BF16), wproj_ref[...])
    o_ref[...] = _layer_norm(z, g_ref[...], b_ref[...])


def _ffn(x, w_in, w_out, ln_g, ln_b, alpha, mixer=None, ple=None):
    assert (mixer is None) == (ple is None)
    t, d = x.shape
    d_ff = w_out.shape[0]
    tm = FFN_TOKEN_TILE
    row = lambda i: (i, 0)
    in_specs = [pl.BlockSpec((tm, d), row)]
    args = [x]
    if mixer is not None:
        mix, w_o, ln2_g, ln2_b = mixer
        in_specs += [pl.BlockSpec((tm, mix.shape[1]), row), _resident(w_o.shape), _resident(ln2_g.shape),
                     _resident(ln2_b.shape)]
        args += [mix, w_o, ln2_g, ln2_b]
    in_specs += [_resident(w_in.shape), _resident(w_out.shape), _resident(ln_g.shape), _resident(ln_b.shape)]
    args += [w_in, w_out, ln_g, ln_b]
    if ple is not None:
        p, w_gate, w_proj = ple
        in_specs += [pl.BlockSpec((tm, p.shape[1]), row), _resident(w_gate.shape), _resident(w_proj.shape)]
        args += [p, w_gate, w_proj]
    return pl.pallas_call(
        functools.partial(_ffn_kernel, alpha=alpha, d_ff=d_ff, after_mixer=mixer is not None),
        grid=(t // tm,),
        in_specs=in_specs,
        out_specs=pl.BlockSpec((tm, d), row),
        out_shape=jax.ShapeDtypeStruct((t, d), F32),
        compiler_params=_params("parallel"),
        name="ffn_after_mixer" if mixer is not None else "ffn",
    )(*args)


def _gla_proj_kernel(x_ref, win_ref, wa2_ref, ba_ref, tri_ref,
                     qd_ref, ki_ref, ks_ref, v_ref, dec_ref, gate_ref, *, kd, vd):
    tm = x_ref.shape[0]
    xb = x_ref[...].astype(BF16)
    proj = _dot(xb, win_ref[...])
    q = proj[:, :kd] * ((kd // GLA_HEADS) ** -0.5)
    k = proj[:, kd:2 * kd]
    v = proj[:, 2 * kd:2 * kd + vd]
    r = proj[:, 2 * kd + vd:2 * kd + 2 * vd]
    a_lr = proj[:, 2 * kd + 2 * vd:]
    log_a = _log_sigmoid(_dot(a_lr.astype(BF16), wa2_ref[...]) + ba_ref[...]) * (1.0 / GLA_TAU)
    tri = tri_ref[...]
    bcum = jnp.concatenate(
        [_tri_cumsum(tri, log_a[s:s + TRI_BLOCK]) for s in range(0, tm, TRI_BLOCK)], axis=0)
    nc = tm // GLA_CHUNK
    b3 = bcum.reshape(nc, GLA_CHUNK, kd)
    b_last = b3[:, GLA_CHUNK - 1:GLA_CHUNK, :]
    qd_ref[...] = (q * jnp.exp(bcum)).astype(BF16)
    ki_ref[...] = (k * jnp.exp(-bcum)).astype(BF16)
    ks_ref[...] = (k.reshape(nc, GLA_CHUNK, kd) * jnp.exp(b_last - b3)).reshape(tm, kd).astype(BF16)
    dec_ref[...] = jnp.exp(b_last)
    v_ref[...] = v.astype(BF16)
    gate_ref[...] = r * _sigmoid(r)


def _gla_proj(x, w_in, w_a2, b_a, tri):
    t, d = x.shape
    kd = w_a2.shape[1]
    vd = (w_in.shape[1] - 2 * kd - w_a2.shape[0]) // 2
    tm = TOKEN_TILE
    row = lambda i: (i, 0)
    return pl.pallas_call(
        functools.partial(_gla_proj_kernel, kd=kd, vd=vd),
        grid=(t // tm,),
        in_specs=[pl.BlockSpec((tm, d), row), _resident(w_in.shape), _resident(w_a2.shape),
                  _resident(b_a.shape), _resident(tri.shape)],
        out_specs=[pl.BlockSpec((tm, kd), row), pl.BlockSpec((tm, kd), row), pl.BlockSpec((tm, kd), row),
                   pl.BlockSpec((tm, vd), row), pl.BlockSpec((tm // GLA_CHUNK, 1, kd), lambda i: (i, 0, 0)),
                   pl.BlockSpec((tm, vd), row)],
        out_shape=[jax.ShapeDtypeStruct((t, kd), BF16)] * 3 + [
            jax.ShapeDtypeStruct((t, vd), BF16),
            jax.ShapeDtypeStruct((t // GLA_CHUNK, 1, kd), F32),
            jax.ShapeDtypeStruct((t, vd), F32)],
        compiler_params=_params("parallel"),
        name="gla_proj",
    )(x, w_in, w_a2, b_a, tri)


def _gla_rec_kernel(qd_ref, ki_ref, ks_ref, v_ref, dec_ref, gate_ref, gng_ref, gnb_ref, o_ref, st_ref):
    tm = qd_ref.shape[1]
    dk = qd_ref.shape[2] // GLA_HEADS
    dv = v_ref.shape[2] // GLA_HEADS

    @pl.when(pl.program_id(1) == 0)
    def _():
        st_ref[...] = jnp.zeros_like(st_ref)

    rows = lax.broadcasted_iota(jnp.int32, (GLA_CHUNK, GLA_CHUNK), 0)
    cols = lax.broadcasted_iota(jnp.int32, (GLA_CHUNK, GLA_CHUNK), 1)
    causal = cols <= rows

    def chunk(c, carry):
        r0 = pl.multiple_of(c * GLA_CHUNK, GLA_CHUNK)
        rs = pl.ds(r0, GLA_CHUNK)
        for h in range(GLA_HEADS):
            ksl = slice(h * dk, (h + 1) * dk)
            vsl = slice(h * dv, (h + 1) * dv)
            qd = qd_ref[0, rs, ksl]
            ki = ki_ref[0, rs, ksl]
            ks = ks_ref[0, rs, ksl]
            vv = v_ref[0, rs, vsl]
            attn = jnp.where(causal, _dot_nt(qd, ki), 0.0).astype(BF16)
            st = st_ref[h]
            o = _dot(attn, vv) + _dot_nt(qd, st.astype(BF16))
            vt = jnp.transpose(vv.astype(F32)).astype(BF16)
            st_ref[h] = st * dec_ref[0, c, :, ksl] + _dot(vt, ks)
            on = _layer_norm(o, gng_ref[:, vsl], gnb_ref[:, vsl])
            o_ref[0, rs, vsl] = (on * gate_ref[0, rs, vsl]).astype(BF16)
        return carry

    lax.fori_loop(0, tm // GLA_CHUNK, chunk, 0, unroll=True)


def _gla_rec(qd, ki, ks, v, dec, gate, gn_g, gn_b):
    b, s, kd = qd.shape
    vd = v.shape[2]
    tm = TOKEN_TILE
    blk = lambda bi, i: (bi, i, 0)
    return pl.pallas_call(
        _gla_rec_kernel,
        grid=(b, s // tm),
        in_specs=[pl.BlockSpec((1, tm, kd), blk), pl.BlockSpec((1, tm, kd), blk), pl.BlockSpec((1, tm, kd), blk),
                  pl.BlockSpec((1, tm, vd), blk),
                  pl.BlockSpec((1, tm // GLA_CHUNK, 1, kd), lambda bi, i: (bi, i, 0, 0)),
                  pl.BlockSpec((1, tm, vd), blk), _resident(gn_g.shape), _resident(gn_b.shape)],
        out_specs=pl.BlockSpec((1, tm, vd), blk),
        out_shape=jax.ShapeDtypeStruct((b, s, vd), BF16),
        scratch_shapes=[pltpu.VMEM((GLA_HEADS, vd // GLA_HEADS, kd // GLA_HEADS), F32)],
        compiler_params=_params("arbitrary", "arbitrary"),
        name="gla_rec",
    )(qd, ki, ks, v, dec, gate, gn_g, gn_b)


def _fox_kv_kernel(x_ref, wk_ref, wvt_ref, wf_ref, bf_ref, tri_ref, spread_ref, group_ref, eye_ref,
                   k_ref, vt_ref, crelt_ref, cblk_ref, cend_ref, kpm_ref, carry_ref, kmax_ref):
    @pl.when(pl.program_id(1) == 0)
    def _():
        carry_ref[...] = jnp.zeros_like(carry_ref)
        kmax_ref[...] = jnp.zeros_like(kmax_ref)

    tk = x_ref.shape[1]
    xb = x_ref[0].astype(BF16)
    log_f = _log_sigmoid(_dot(xb, wf_ref[...]) + bf_ref[...])
    c_rel = _tri_cumsum(tri_ref[...], log_f) * LOG2E
    terms = jnp.concatenate(_split3(-c_rel), axis=1)
    bias = _dot(terms, spread_ref[...])
    crelt_ref[0, 0] = -_dot_nt(eye_ref[...], terms)
    k = _dot(xb, wk_ref[...])
    slot_lane = lax.broadcasted_iota(jnp.int32, k.shape, 1) % FOX_SLOT
    is_one_lane = (slot_lane >= FOX_HEAD_DIM + 3) & (slot_lane < FOX_HEAD_DIM + 6)
    k_aug = jnp.where(is_one_lane, 1.0, k + bias).astype(BF16)
    for h in range(FOX_HEADS):
        k_ref[0, 0, h] = k_aug[:, h * FOX_SLOT:(h + 1) * FOX_SLOT]
    vt = _dot_nt(wvt_ref[...], xb).astype(BF16).reshape(FOX_HEADS, FOX_HEAD_DIM, tk)
    pad_row = lax.broadcasted_iota(jnp.int32, (FOX_HEADS, FOX_V_ROWS - FOX_HEAD_DIM, tk), 1)
    vt_ref[0, 0] = jnp.concatenate([vt, jnp.where(pad_row == 0, 1.0, 0.0).astype(BF16)], axis=1)
    cblk_ref[0, 0] = carry_ref[...]
    carry_ref[...] = carry_ref[...] + c_rel[tk - 1:tk, :]
    cend_ref[0, 0] = carry_ref[...]
    kr = k.astype(BF16).astype(F32)
    norm2 = _dot((kr * kr).astype(BF16), group_ref[...]) * NORM_SLACK
    kmax_ref[...] = jnp.maximum(kmax_ref[...], jnp.sqrt(jnp.max(norm2, axis=0, keepdims=True)))
    kpm_ref[0, 0] = kmax_ref[...]


def _fox_kv(x, wk_aug, wvt, wf, b_f, tri, spread, group, eye):
    b, s, d = x.shape
    tk = FOX_BLOCK
    nkv = s // tk
    nh = wf.shape[1]
    blk4 = lambda bi, j: (bi, j, 0, 0)
    blk5 = lambda bi, j: (bi, j, 0, 0, 0)
    small = jax.ShapeDtypeStruct((b, nkv, 1, nh), F32)
    return pl.pallas_call(
        _fox_kv_kernel,
        grid=(b, nkv),
        in_specs=[pl.BlockSpec((1, tk, d), lambda bi, j: (bi, j, 0)), _resident(wk_aug.shape),
                  _resident(wvt.shape), _resident(wf.shape), _resident(b_f.shape), _resident(tri.shape),
                  _resident(spread.shape), _resident(group.shape), _resident(eye.shape)],
        out_specs=[pl.BlockSpec((1, 1, nh, tk, FOX_SLOT), blk5),
                   pl.BlockSpec((1, 1, nh, FOX_V_ROWS, tk), blk5),
                   pl.BlockSpec((1, 1, 3 * nh, tk), blk4),
                   pl.BlockSpec((1, 1, 1, nh), blk4), pl.BlockSpec((1, 1, 1, nh), blk4),
                   pl.BlockSpec((1, 1, 1, nh), blk4)],
        out_shape=[jax.ShapeDtypeStruct((b, nkv, nh, tk, FOX_SLOT), BF16),
                   jax.ShapeDtypeStruct((b, nkv, nh, FOX_V_ROWS, tk), BF16),
                   jax.ShapeDtypeStruct((b, nkv, 3 * nh, tk), F32),
                   small, small, small],
        scratch_shapes=[pltpu.VMEM((1, nh), F32), pltpu.VMEM((1, nh), F32)],
        compiler_params=_params("arbitrary", "arbitrary"),
        name="fox_kv",
    )(x, wk_aug, wvt, wf, b_f, tri, spread, group, eye)


def _fox_q_kernel(x_ref, wqt_ref, wg_ref, crelt_ref, spreadt_ref, cblk_ref, cend_ref, kpm_ref,
                  qt_ref, g_ref, need_ref, need_fixed_ref, shift_ref):
    i = pl.program_id(1)
    tq = x_ref.shape[1]
    xb = x_ref[0].astype(BF16)
    q = _dot_nt(wqt_ref[...], xb) * (FOX_HEAD_DIM ** -0.5 * LOG2E)
    q = q.reshape(FOX_HEADS, FOX_HEAD_DIM, tq)
    qt = jnp.concatenate([q, jnp.zeros((FOX_HEADS, FOX_SLOT - FOX_HEAD_DIM, tq), F32)], axis=1)
    qt = qt.reshape(FOX_HEADS * FOX_SLOT, tq)
    c_rows = _dot(spreadt_ref[...], crelt_ref[0, 0].astype(BF16))
    qt3 = (qt + c_rows).reshape(FOX_HEADS, FOX_SLOT, tq)
    slot_row = lax.broadcasted_iota(jnp.int32, qt3.shape, 1)
    is_one_row = (slot_row >= FOX_HEAD_DIM) & (slot_row < FOX_HEAD_DIM + 3)
    qt_ref[0] = jnp.where(is_one_row, 1.0, qt3).astype(BF16)
    g_ref[0] = _dot(xb, wg_ref[...])

    qr = qt.astype(BF16).astype(F32).reshape(FOX_HEADS, FOX_SLOT, tq)
    qn = jnp.sqrt(jnp.max(jnp.sum(qr * qr, axis=1), axis=1, keepdims=True))
    blk = lax.broadcasted_iota(jnp.int32, cblk_ref.shape[1:], 1)
    at_i = blk == i
    c_before = jnp.sum(jnp.where(at_i, cblk_ref[0], 0.0), axis=1, keepdims=True)
    k_norm = jnp.sum(jnp.where(at_i, kpm_ref[0], 0.0), axis=1, keepdims=True)
    qk_bound = NORM_SLACK * qn * k_norm

    def count_blocks(thr):
        needed = (blk < i) & jnp.logical_not(cend_ref[0] > thr)
        return jnp.sum(jnp.where(needed, 1.0, 0.0), axis=1, keepdims=True).astype(jnp.int32)

    need_ref[0, 0] = count_blocks(c_before + EXP2_ZERO_GAP + 2.0 * qk_bound)
    need_fixed_ref[0, 0] = count_blocks(c_before + EXP2_ZERO_GAP)
    shift_ref[0, 0] = qk_bound


def _fox_q(x, wqt, wg, crelt, spreadt, cblk_t, cend_t, kpm_t):
    b, s, d = x.shape
    tm = FOX_BLOCK
    nkv = cblk_t.shape[2]
    per_batch = pl.BlockSpec((1, FOX_HEADS, nkv), lambda bi, i: (bi, 0, 0))
    return pl.pallas_call(
        _fox_q_kernel,
        grid=(b, s // tm),
        in_specs=[pl.BlockSpec((1, tm, d), lambda bi, i: (bi, i, 0)), _resident(wqt.shape),
                  _resident(wg.shape),
                  pl.BlockSpec((1, 1, crelt.shape[2], tm), lambda bi, i: (bi, i, 0, 0)),
                  _resident(spreadt.shape), per_batch, per_batch, per_batch],
        out_specs=[pl.BlockSpec((1, FOX_HEADS, FOX_SLOT, tm), lambda bi, i: (bi, 0, 0, i)),
                   pl.BlockSpec((1, tm, wg.shape[1]), lambda bi, i: (bi, i, 0)),
                   pl.BlockSpec((1, 1, FOX_HEADS, 1), lambda bi, i: (bi, i, 0, 0)),
                   pl.BlockSpec((1, 1, FOX_HEADS, 1), lambda bi, i: (bi, i, 0, 0)),
                   pl.BlockSpec((1, 1, FOX_HEADS, 1), lambda bi, i: (bi, i, 0, 0))],
        out_shape=[jax.ShapeDtypeStruct((b, FOX_HEADS, FOX_SLOT, s), BF16),
                   jax.ShapeDtypeStruct((b, s, wg.shape[1]), F32),
                   jax.ShapeDtypeStruct((b, s // tm, FOX_HEADS, 1), jnp.int32),
                   jax.ShapeDtypeStruct((b, s // tm, FOX_HEADS, 1), jnp.int32),
                   jax.ShapeDtypeStruct((b, s // tm, FOX_HEADS, 1), F32)],
        compiler_params=_params("parallel", "parallel"),
        name="fox_q",
    )(x, wqt, wg, crelt, spreadt, cblk_t, cend_t, kpm_t)


def _fox_attn_kernel(cblk_ref, need_ref, need_fixed_ref, shift_ref, qt_ref, k_ref, vt_ref, g_ref, o_ref,
                     acc_ref, *bufs, nkv):
    bi = pl.program_id(0)
    hp = pl.program_id(1)
    i = pl.program_id(2)
    tq = qt_ref.shape[3]
    tk = k_ref.shape[3]
    nhp = FOX_HEADS_PER_STEP
    assert nhp == 2
    dh = FOX_HEAD_DIM
    unroll = len(bufs)

    def head_base(hh):
        return (bi * FOX_HEADS + hp * nhp + hh) * nkv

    def scores(hh, j):
        return _dot(k_ref[0, j, hh], qt_ref[0, hh])

    key_pos = lax.broadcasted_iota(jnp.int32, (tk, tq), 0)
    qry_pos = lax.broadcasted_iota(jnp.int32, (tk, tq), 1)
    bound = [shift_ref[head_base(hh) + i] for hh in range(nhp)]
    fixed_shift_ok = jnp.maximum(bound[0], bound[1]) <= MAX_FIXED_SHIFT

    @pl.when(fixed_shift_ok)
    def _():
        for hh in range(nhp):
            p = jnp.where(key_pos <= qry_pos, jnp.exp2(scores(hh, i) - bound[hh]), 0.0)
            acc_ref[hh] = _dot(vt_ref[0, i, hh], p.astype(BF16))

        n0 = need_fixed_ref[head_base(0) + i]
        total = n0 + need_fixed_ref[head_base(1) + i]

        def item(e):
            hh = (e >= n0).astype(jnp.int32)
            j = jnp.maximum(i - 1 - (e - hh * n0), 0)
            base = head_base(hh)
            shift = cblk_ref[base + i] - cblk_ref[base + j] - shift_ref[base + i]
            return hh, j, jnp.where(e < total, shift, -jnp.inf)

        def probabilities(e, p_ref):
            hh, j, shift = item(e)
            p_ref[...] = jnp.exp2(scores(hh, j) + shift).astype(BF16)

        def accumulate(e, p_ref):
            hh, j, _ = item(e)
            acc_ref[hh] = acc_ref[hh] + _dot(vt_ref[0, j, hh], p_ref[...])

        def run_group(first, size):
            for u in range(size):
                probabilities(first + u, bufs[u])
            for u in range(size):
                accumulate(first + u, bufs[u])

        def full_group(t, carry):
            run_group(unroll * t, unroll)
            return carry

        n_full = total // unroll
        lax.fori_loop(0, n_full, full_group, 0)
        rest = total - n_full * unroll

        @pl.when(rest > unroll // 2)
        def _():
            run_group(n_full * unroll, unroll)

        @pl.when((rest > 0) & (rest <= unroll // 2))
        def _():
            run_group(n_full * unroll, unroll // 2)

    @pl.when(jnp.logical_not(fixed_shift_ok))
    def _():
        for hh in range(nhp):
            base = head_base(hh)
            s = jnp.where(key_pos <= qry_pos, scores(hh, i), -jnp.inf)
            m = jnp.max(s, axis=0, keepdims=True)
            acc_ref[hh] = _dot(vt_ref[0, i, hh], jnp.exp2(s - m).astype(BF16))

            def step(jj, m, hh=hh, base=base):
                j = i - 1 - jj
                s = scores(hh, j) + (cblk_ref[base + i] - cblk_ref[base + j])
                m_new = jnp.maximum(m, jnp.max(s, axis=0, keepdims=True))
                acc_ref[hh] = (jnp.exp2(m - m_new) * acc_ref[hh]
                               + _dot(vt_ref[0, j, hh], jnp.exp2(s - m_new).astype(BF16)))
                return m_new

            lax.fori_loop(0, need_ref[base + i], step, m)

    o = jnp.concatenate([acc_ref[hh, :dh, :] / acc_ref[hh, dh:dh + 1, :] for hh in range(nhp)], axis=0)
    o_ref[0] = (jnp.transpose(o) * _sigmoid(g_ref[0])).astype(BF16)


def _fox_attn(cblk, need, need_fixed, shift, qt, k, vt, g):
    b, nkv, nh, tk, slot = k.shape
    v_rows = vt.shape[3]
    s = nkv * tk
    tq = FOX_BLOCK
    nhp = FOX_HEADS_PER_STEP
    out_w = nhp * FOX_HEAD_DIM
    smem = pl.BlockSpec(memory_space=pltpu.SMEM)
    return pl.pallas_call(
        functools.partial(_fox_attn_kernel, nkv=nkv),
        grid=(b, nh // nhp, s // tq),
        in_specs=[smem, smem, smem, smem,
                  pl.BlockSpec((1, nhp, slot, tq), lambda bi, hp, i: (bi, hp, 0, i)),
                  pl.BlockSpec((1, nkv, nhp, tk, slot), lambda bi, hp, i: (bi, 0, hp, 0, 0)),
                  pl.BlockSpec((1, nkv, nhp, v_rows, tk), lambda bi, hp, i: (bi, 0, hp, 0, 0)),
                  pl.BlockSpec((1, tq, out_w), lambda bi, hp, i: (bi, i, hp))],
        out_specs=pl.BlockSpec((1, tq, out_w), lambda bi, hp, i: (bi, i, hp)),
        out_shape=jax.ShapeDtypeStruct((b, s, nh * FOX_HEAD_DIM), BF16),
        scratch_shapes=[pltpu.VMEM((nhp, v_rows, tq), F32)] + [pltpu.VMEM((tk, tq), BF16)] * FOX_ATTN_UNROLL,
        compiler_params=_params("arbitrary", "arbitrary", "arbitrary"),
        name="fox_attn",
    )(cblk, need, need_fixed, shift, qt, k, vt, g)


def _block_tri(n, block):
    r = jnp.arange(n)
    return ((r[:, None] >= r[None, :]) & (r[:, None] // block == r[None, :] // block)).astype(BF16)


def _spread_heads(w):
    d = w.shape[0]
    w = w.reshape(d, FOX_HEADS, FOX_HEAD_DIM)
    w = jnp.pad(w, ((0, 0), (0, 0), (0, FOX_SLOT - FOX_HEAD_DIM)))
    return w.reshape(d, FOX_HEADS * FOX_SLOT)


def _head_group_matrix():
    return (jnp.arange(FOX_HEADS * FOX_SLOT)[:, None] // FOX_SLOT == jnp.arange(FOX_HEADS)[None, :]).astype(BF16)


def _bias_spread_matrix(first_lane):
    rows = jnp.arange(3 * FOX_HEADS)
    term, head = rows // FOX_HEADS, rows % FOX_HEADS
    cols = head * FOX_SLOT + first_lane + term
    return (jnp.arange(FOX_HEADS * FOX_SLOT)[None, :] == cols[:, None]).astype(BF16)


def kernel(x, p, ffn1_w_in, ffn1_w_out, ln1_g, ln1_b, gla_w_in, gla_w_a2, gla_b_a, gla_gn_g, gla_gn_b, gla_w_o, fox_w_kvf, fox_b_f, fox_w_in, fox_w_o, ln2_g, ln2_b, ffn2_w_in, ffn2_w_out, ple_w_gate, ple_w_proj, ln3_g, ln3_b):
    bsz, seq, d = x.shape
    depth = ffn1_w_in.shape[0]
    n_gla = gla_w_in.shape[0]
    alpha = (2 * depth) ** 0.25
    t = bsz * seq
    fox_w = FOX_HEADS * FOX_HEAD_DIM
    row = lambda a: a.reshape(1, -1)

    tri_chunk = _block_tri(TRI_BLOCK, GLA_CHUNK)
    tri_block = _block_tri(FOX_BLOCK, FOX_BLOCK)
    key_bias_spread = _bias_spread_matrix(FOX_HEAD_DIM)
    query_bias_spread_t = _bias_spread_matrix(FOX_HEAD_DIM + 3).T

    xf = x.reshape(t, d)
    pf = p.reshape(depth, t, p.shape[-1])
    heads_major = lambda a: jnp.transpose(a[:, :, 0, :], (0, 2, 1))
    fox_k = fox_vt = fox_crelt = cblk_t = cend_t = kpm_t = None
    for i in range(depth):
        if i == n_gla:
            wk_aug = _spread_heads(fox_w_kvf[:, :fox_w]).astype(BF16)
            wvt = fox_w_kvf[:, fox_w:2 * fox_w].T.astype(BF16)
            wf = fox_w_kvf[:, 2 * fox_w:].astype(BF16)
            fox_k, fox_vt, fox_crelt, cblk, cend, kpm = _fox_kv(
                xf.reshape(bsz, seq, d), wk_aug, wvt, wf, row(fox_b_f), tri_block, key_bias_spread,
                _head_group_matrix(), jnp.eye(3 * FOX_HEADS, dtype=BF16))
            cblk_t, cend_t, kpm_t = heads_major(cblk), heads_major(cend), heads_major(kpm)
        xf = _ffn(xf, ffn1_w_in[i].astype(BF16), ffn1_w_out[i].astype(BF16), row(ln1_g[i]), row(ln1_b[i]), alpha)
        if i < n_gla:
            qd, ki, ks, v, dec, gate = _gla_proj(xf, gla_w_in[i].astype(BF16), gla_w_a2[i].astype(BF16),
                                                 row(gla_b_a[i]), tri_chunk)
            b3 = lambda a: a.reshape(bsz, seq, -1)
            mix = _gla_rec(b3(qd), b3(ki), b3(ks), b3(v), dec.reshape(bsz, seq // GLA_CHUNK, 1, -1), b3(gate),
                           row(gla_gn_g[i]), row(gla_gn_b[i])).reshape(t, -1)
            w_o = gla_w_o[i]
        else:
            j = i - n_gla
            wqt = fox_w_in[j][:, :fox_w].T.astype(BF16)
            qt, g, need, need_fixed, shift = _fox_q(xf.reshape(bsz, seq, d), wqt, fox_w_in[j][:, fox_w:].astype(BF16),
                                        fox_crelt, query_bias_spread_t, cblk_t, cend_t, kpm_t)
            flat = lambda a: jnp.transpose(a[:, :, :, 0], (0, 2, 1)).reshape(-1)
            mix = _fox_attn(cblk_t.reshape(-1), flat(need), flat(need_fixed), flat(shift), qt, fox_k, fox_vt,
                            g).reshape(t, -1)
            w_o = fox_w_o[j]
        xf = _ffn(xf, ffn2_w_in[i].astype(BF16), ffn2_w_out[i].astype(BF16), row(ln3_g[i]), row(ln3_b[i]), alpha,
                  mixer=(mix, w_o.astype(BF16), row(ln2_g[i]), row(ln2_b[i])),
                  ple=(pf[i], ple_w_gate[i].astype(BF16), ple_w_proj[i].astype(BF16)))
    return xf.reshape(bsz, seq, d)
```

```python
import functools
import math

import jax
import jax.numpy as jnp
from jax import lax
from jax.experimental import pallas as pl
from jax.experimental.pallas import tpu as pltpu

F32 = jnp.float32
BF16 = jnp.bfloat16

GLA_HEADS = 4
GLA_TAU = 16.0
GLA_CHUNK = 64
FOX_HEADS = 16
FOX_HEAD_DIM = 64
LN_EPS = 1e-5

V7X_VMEM_LIMIT_BYTES = 56 * 1024 * 1024

TOKEN_TILE = 512
FFN_TOKEN_TILE = 512
FOX_BLOCK = 512
FOX_SLOT = 128
FOX_HEADS_PER_STEP = 2
TRI_BLOCK = 256
LOG2E = math.log2(math.e)
EXP2_ZERO_GAP = 136.0
NORM_SLACK = 1.01
FOX_V_ROWS = 80
FOX_ATTN_UNROLL = 4
MAX_FIXED_SHIFT = 50.0


def _params(*sems):
    return pltpu.CompilerParams(dimension_semantics=sems, vmem_limit_bytes=V7X_VMEM_LIMIT_BYTES)


def _resident(shape):
    zeros = (0,) * len(shape)
    return pl.BlockSpec(shape, lambda *_: zeros, pipeline_mode=pl.Buffered(1))


def _dot(a, b):
    return jnp.dot(a, b, preferred_element_type=F32)


def _dot_nt(a, b):
    return lax.dot_general(a, b, (((1,), (1,)), ((), ())), preferred_element_type=F32)


def _split3(a):
    hi = a.astype(BF16)
    r1 = a - hi.astype(F32)
    mid = r1.astype(BF16)
    lo = (r1 - mid.astype(F32)).astype(BF16)
    return hi, mid, lo


def _tri_cumsum(tri, a):
    hi, mid, lo = _split3(a)
    return _dot(tri, hi) + _dot(tri, mid) + _dot(tri, lo)


def _log_sigmoid(z):
    return jnp.minimum(z, 0.0) - jnp.log1p(jnp.exp(-jnp.abs(z)))


def _sigmoid(z):
    return 1.0 / (1.0 + jnp.exp(-z))


def _layer_norm(z, g, b):
    mu = jnp.mean(z, axis=-1, keepdims=True)
    zc = z - mu
    var = jnp.mean(zc * zc, axis=-1, keepdims=True)
    return zc * lax.rsqrt(var + LN_EPS) * g + b


def _ffn_kernel(*refs, alpha, d_ff, after_mixer):
    if after_mixer:
        (x_ref, mix_ref, wo_ref, g2_ref, b2_ref, win_ref, wout_ref, g_ref, b_ref,
         p_ref, wgate_ref, wproj_ref, o_ref) = refs
        x = _layer_norm(alpha * x_ref[...] + _dot(mix_ref[...], wo_ref[...]), g2_ref[...], b2_ref[...])
    else:
        x_ref, win_ref, wout_ref, g_ref, b_ref, o_ref = refs
        x = x_ref[...]
    xb = x.astype(BF16)
    gate = _dot(xb, win_ref[:, :d_ff])
    up = _dot(xb, win_ref[:, d_ff:])
    h = (gate * _sigmoid(gate) * up).astype(BF16)
    z = alpha * x + 0.5 * _dot(h, wout_ref[...])
    if after_mixer:
        ple_gate = _sigmoid(_dot(xb, wgate_ref[...]))
        z = z + ple_gate * _dot(p_ref[0].astype(BF16), wproj_ref[...])
    o_ref[...] = _layer_norm(z, g_ref[...], b_ref[...])


def _ffn(x, w_in, w_out, ln_g, ln_b, alpha, mixer=None, ple=None):
    assert (mixer is None) == (ple is None)
    t, d = x.shape
    d_ff = w_out.shape[0]
    tm = FFN_TOKEN_TILE
    row = lambda i: (i, 0)
    in_specs = [pl.BlockSpec((tm, d), row)]
    args = [x]
    if mixer is not None:
        mix, w_o, ln2_g, ln2_b = mixer
        in_specs += [pl.BlockSpec((tm, mix.shape[1]), row), _resident(w_o.shape), _resident(ln2_g.shape),
                     _resident(ln2_b.shape)]
        args += [mix, w_o, ln2_g, ln2_b]
    in_specs += [_resident(w_in.shape), _resident(w_out.shape), _resident(ln_g.shape), _resident(ln_b.shape)]
    args += [w_in, w_out, ln_g, ln_b]
    if ple is not None:
        p, layer, w_gate, w_proj = ple
        in_specs += [pl.BlockSpec((1, tm, p.shape[2]), lambda i: (layer, i, 0)), _resident(w_gate.shape),
                     _resident(w_proj.shape)]
        args += [p, w_gate, w_proj]
    return pl.pallas_call(
        functools.partial(_ffn_kernel, alpha=alpha, d_ff=d_ff, after_mixer=mixer is not None),
        grid=(t // tm,),
        in_specs=in_specs,
        out_specs=pl.BlockSpec((tm, d), row),
        out_shape=jax.ShapeDtypeStruct((t, d), F32),
        compiler_params=_params("parallel"),
        name="ffn_after_mixer" if mixer is not None else "ffn",
    )(*args)


def _gla_proj_kernel(x_ref, win_ref, wa2_ref, ba_ref, tri_ref,
                     qd_ref, ki_ref, ks_ref, v_ref, dec_ref, gate_ref, *, kd, vd):
    tm = x_ref.shape[0]
    xb = x_ref[...].astype(BF16)
    proj = _dot(xb, win_ref[...])
    q = proj[:, :kd] * ((kd // GLA_HEADS) ** -0.5)
    k = proj[:, kd:2 * kd]
    v = proj[:, 2 * kd:2 * kd + vd]
    r = proj[:, 2 * kd + vd:2 * kd + 2 * vd]
    a_lr = proj[:, 2 * kd + 2 * vd:]
    log_a = _log_sigmoid(_dot(a_lr.astype(BF16), wa2_ref[...]) + ba_ref[...]) * (1.0 / GLA_TAU)
    tri = tri_ref[...]
    bcum = jnp.concatenate(
        [_tri_cumsum(tri, log_a[s:s + TRI_BLOCK]) for s in range(0, tm, TRI_BLOCK)], axis=0)
    nc = tm // GLA_CHUNK
    b3 = bcum.reshape(nc, GLA_CHUNK, kd)
    b_last = b3[:, GLA_CHUNK - 1:GLA_CHUNK, :]
    qd_ref[...] = (q * jnp.exp(bcum)).astype(BF16)
    ki_ref[...] = (k * jnp.exp(-bcum)).astype(BF16)
    ks_ref[...] = (k.reshape(nc, GLA_CHUNK, kd) * jnp.exp(b_last - b3)).reshape(tm, kd).astype(BF16)
    dec_ref[...] = jnp.exp(b_last)
    v_ref[...] = v.astype(BF16)
    gate_ref[...] = r * _sigmoid(r)


def _gla_proj(x, w_in, w_a2, b_a, tri):
    t, d = x.shape
    kd = w_a2.shape[1]
    vd = (w_in.shape[1] - 2 * kd - w_a2.shape[0]) // 2
    tm = TOKEN_TILE
    row = lambda i: (i, 0)
    return pl.pallas_call(
        functools.partial(_gla_proj_kernel, kd=kd, vd=vd),
        grid=(t // tm,),
        in_specs=[pl.BlockSpec((tm, d), row), _resident(w_in.shape), _resident(w_a2.shape),
                  _resident(b_a.shape), _resident(tri.shape)],
        out_specs=[pl.BlockSpec((tm, kd), row), pl.BlockSpec((tm, kd), row), pl.BlockSpec((tm, kd), row),
                   pl.BlockSpec((tm, vd), row), pl.BlockSpec((tm // GLA_CHUNK, 1, kd), lambda i: (i, 0, 0)),
                   pl.BlockSpec((tm, vd), row)],
        out_shape=[jax.ShapeDtypeStruct((t, kd), BF16)] * 3 + [
            jax.ShapeDtypeStruct((t, vd), BF16),
            jax.ShapeDtypeStruct((t // GLA_CHUNK, 1, kd), F32),
            jax.ShapeDtypeStruct((t, vd), F32)],
        compiler_params=_params("parallel"),
        name="gla_proj",
    )(x, w_in, w_a2, b_a, tri)


def _gla_rec_kernel(qd_ref, ki_ref, ks_ref, v_ref, dec_ref, gate_ref, gng_ref, gnb_ref, o_ref, st_ref):
    tm = qd_ref.shape[1]
    dk = qd_ref.shape[2] // GLA_HEADS
    dv = v_ref.shape[2] // GLA_HEADS

    @pl.when(pl.program_id(1) == 0)
    def _():
        st_ref[...] = jnp.zeros_like(st_ref)

    rows = lax.broadcasted_iota(jnp.int32, (GLA_CHUNK, GLA_CHUNK), 0)
    cols = lax.broadcasted_iota(jnp.int32, (GLA_CHUNK, GLA_CHUNK), 1)
    causal = cols <= rows

    def chunk(c, carry):
        r0 = pl.multiple_of(c * GLA_CHUNK, GLA_CHUNK)
        rs = pl.ds(r0, GLA_CHUNK)
        for h in range(GLA_HEADS):
            ksl = slice(h * dk, (h + 1) * dk)
            vsl = slice(h * dv, (h + 1) * dv)
            qd = qd_ref[0, rs, ksl]
            ki = ki_ref[0, rs, ksl]
            ks = ks_ref[0, rs, ksl]
            vv = v_ref[0, rs, vsl]
            attn = jnp.where(causal, _dot_nt(qd, ki), 0.0).astype(BF16)
            st = st_ref[h]
            o = _dot(attn, vv) + _dot_nt(qd, st.astype(BF16))
            vt = jnp.transpose(vv.astype(F32)).astype(BF16)
            st_ref[h] = st * dec_ref[0, c, :, ksl] + _dot(vt, ks)
            on = _layer_norm(o, gng_ref[:, vsl], gnb_ref[:, vsl])
            o_ref[0, rs, vsl] = (on * gate_ref[0, rs, vsl]).astype(BF16)
        return carry

    lax.fori_loop(0, tm // GLA_CHUNK, chunk, 0, unroll=True)


def _gla_rec(qd, ki, ks, v, dec, gate, gn_g, gn_b):
    b, s, kd = qd.shape
    vd = v.shape[2]
    tm = TOKEN_TILE
    blk = lambda bi, i: (bi, i, 0)
    return pl.pallas_call(
        _gla_rec_kernel,
        grid=(b, s // tm),
        in_specs=[pl.BlockSpec((1, tm, kd), blk), pl.BlockSpec((1, tm, kd), blk), pl.BlockSpec((1, tm, kd), blk),
                  pl.BlockSpec((1, tm, vd), blk),
                  pl.BlockSpec((1, tm // GLA_CHUNK, 1, kd), lambda bi, i: (bi, i, 0, 0)),
                  pl.BlockSpec((1, tm, vd), blk), _resident(gn_g.shape), _resident(gn_b.shape)],
        out_specs=pl.BlockSpec((1, tm, vd), blk),
        out_shape=jax.ShapeDtypeStruct((b, s, vd), BF16),
        scratch_shapes=[pltpu.VMEM((GLA_HEADS, vd // GLA_HEADS, kd // GLA_HEADS), F32)],
        compiler_params=_params("arbitrary", "arbitrary"),
        name="gla_rec",
    )(qd, ki, ks, v, dec, gate, gn_g, gn_b)


def _fox_kv_kernel(x_ref, wk_ref, wvt_ref, wf_ref, bf_ref, tri_ref, spread_ref, group_ref, eye_ref,
                   k_ref, vt_ref, crelt_ref, cblk_ref, cend_ref, kpm_ref, carry_ref, kmax_ref):
    @pl.when(pl.program_id(1) == 0)
    def _():
        carry_ref[...] = jnp.zeros_like(carry_ref)
        kmax_ref[...] = jnp.zeros_like(kmax_ref)

    tk = x_ref.shape[1]
    xb = x_ref[0].astype(BF16)
    log_f = _log_sigmoid(_dot(xb, wf_ref[...]) + bf_ref[...])
    c_rel = _tri_cumsum(tri_ref[...], log_f) * LOG2E
    terms = jnp.concatenate(_split3(-c_rel), axis=1)
    bias = _dot(terms, spread_ref[...])
    crelt_ref[0, 0] = -_dot_nt(eye_ref[...], terms)
    k = _dot(xb, wk_ref[...])
    slot_lane = lax.broadcasted_iota(jnp.int32, k.shape, 1) % FOX_SLOT
    is_one_lane = (slot_lane >= FOX_HEAD_DIM + 3) & (slot_lane < FOX_HEAD_DIM + 6)
    k_aug = jnp.where(is_one_lane, 1.0, k + bias).astype(BF16)
    for h in range(FOX_HEADS):
        k_ref[0, 0, h] = k_aug[:, h * FOX_SLOT:(h + 1) * FOX_SLOT]
    vt = _dot_nt(wvt_ref[...], xb).astype(BF16).reshape(FOX_HEADS, FOX_HEAD_DIM, tk)
    pad_row = lax.broadcasted_iota(jnp.int32, (FOX_HEADS, FOX_V_ROWS - FOX_HEAD_DIM, tk), 1)
    vt_ref[0, 0] = jnp.concatenate([vt, jnp.where(pad_row == 0, 1.0, 0.0).astype(BF16)], axis=1)
    cblk_ref[0, 0] = carry_ref[...]
    carry_ref[...] = carry_ref[...] + c_rel[tk - 1:tk, :]
    cend_ref[0, 0] = carry_ref[...]
    kr = k.astype(BF16).astype(F32)
    norm2 = _dot((kr * kr).astype(BF16), group_ref[...]) * NORM_SLACK
    kmax_ref[...] = jnp.maximum(kmax_ref[...], jnp.sqrt(jnp.max(norm2, axis=0, keepdims=True)))
    kpm_ref[0, 0] = kmax_ref[...]


def _fox_kv(x, wk_aug, wvt, wf, b_f, tri, spread, group, eye):
    b, s, d = x.shape
    tk = FOX_BLOCK
    nkv = s // tk
    nh = wf.shape[1]
    blk4 = lambda bi, j: (bi, j, 0, 0)
    blk5 = lambda bi, j: (bi, j, 0, 0, 0)
    small = jax.ShapeDtypeStruct((b, nkv, 1, nh), F32)
    return pl.pallas_call(
        _fox_kv_kernel,
        grid=(b, nkv),
        in_specs=[pl.BlockSpec((1, tk, d), lambda bi, j: (bi, j, 0)), _resident(wk_aug.shape),
                  _resident(wvt.shape), _resident(wf.shape), _resident(b_f.shape), _resident(tri.shape),
                  _resident(spread.shape), _resident(group.shape), _resident(eye.shape)],
        out_specs=[pl.BlockSpec((1, 1, nh, tk, FOX_SLOT), blk5),
                   pl.BlockSpec((1, 1, nh, FOX_V_ROWS, tk), blk5),
                   pl.BlockSpec((1, 1, 3 * nh, tk), blk4),
                   pl.BlockSpec((1, 1, 1, nh), blk4), pl.BlockSpec((1, 1, 1, nh), blk4),
                   pl.BlockSpec((1, 1, 1, nh), blk4)],
        out_shape=[jax.ShapeDtypeStruct((b, nkv, nh, tk, FOX_SLOT), BF16),
                   jax.ShapeDtypeStruct((b, nkv, nh, FOX_V_ROWS, tk), BF16),
                   jax.ShapeDtypeStruct((b, nkv, 3 * nh, tk), F32),
                   small, small, small],
        scratch_shapes=[pltpu.VMEM((1, nh), F32), pltpu.VMEM((1, nh), F32)],
        compiler_params=_params("arbitrary", "arbitrary"),
        name="fox_kv",
    )(x, wk_aug, wvt, wf, b_f, tri, spread, group, eye)


def _fox_q_kernel(x_ref, wqt_ref, wg_ref, crelt_ref, spreadt_ref, cblk_ref, cend_ref, kpm_ref,
                  qt_ref, g_ref, need_ref, need_fixed_ref, shift_ref):
    i = pl.program_id(1)
    tq = x_ref.shape[1]
    xb = x_ref[0].astype(BF16)
    q = _dot_nt(wqt_ref[...], xb) * (FOX_HEAD_DIM ** -0.5 * LOG2E)
    q = q.reshape(FOX_HEADS, FOX_HEAD_DIM, tq)
    qt = jnp.concatenate([q, jnp.zeros((FOX_HEADS, FOX_SLOT - FOX_HEAD_DIM, tq), F32)], axis=1)
    qt = qt.reshape(FOX_HEADS * FOX_SLOT, tq)
    c_rows = _dot(spreadt_ref[...], crelt_ref[0, 0].astype(BF16))
    qt3 = (qt + c_rows).reshape(FOX_HEADS, FOX_SLOT, tq)
    slot_row = lax.broadcasted_iota(jnp.int32, qt3.shape, 1)
    is_one_row = (slot_row >= FOX_HEAD_DIM) & (slot_row < FOX_HEAD_DIM + 3)
    qt_ref[0] = jnp.where(is_one_row, 1.0, qt3).astype(BF16)
    g_ref[0] = _dot(xb, wg_ref[...])

    qr = qt.astype(BF16).astype(F32).reshape(FOX_HEADS, FOX_SLOT, tq)
    qn = jnp.sqrt(jnp.max(jnp.sum(qr * qr, axis=1), axis=1, keepdims=True))
    blk = lax.broadcasted_iota(jnp.int32, cblk_ref.shape[1:], 1)
    at_i = blk == i
    c_before = jnp.sum(jnp.where(at_i, cblk_ref[0], 0.0), axis=1, keepdims=True)
    k_norm = jnp.sum(jnp.where(at_i, kpm_ref[0], 0.0), axis=1, keepdims=True)
    qk_bound = NORM_SLACK * qn * k_norm

    def count_blocks(thr):
        needed = (blk < i) & jnp.logical_not(cend_ref[0] > thr)
        return jnp.sum(jnp.where(needed, 1.0, 0.0), axis=1, keepdims=True).astype(jnp.int32)

    need_ref[0, 0] = count_blocks(c_before + EXP2_ZERO_GAP + 2.0 * qk_bound)
    need_fixed_ref[0, 0] = count_blocks(c_before + EXP2_ZERO_GAP)
    shift_ref[0, 0] = qk_bound


def _fox_q(x, wqt, wg, crelt, spreadt, cblk_t, cend_t, kpm_t):
    b, s, d = x.shape
    tm = FOX_BLOCK
    nkv = cblk_t.shape[2]
    per_batch = pl.BlockSpec((1, FOX_HEADS, nkv), lambda bi, i: (bi, 0, 0))
    return pl.pallas_call(
        _fox_q_kernel,
        grid=(b, s // tm),
        in_specs=[pl.BlockSpec((1, tm, d), lambda bi, i: (bi, i, 0)), _resident(wqt.shape),
                  _resident(wg.shape),
                  pl.BlockSpec((1, 1, crelt.shape[2], tm), lambda bi, i: (bi, i, 0, 0)),
                  _resident(spreadt.shape), per_batch, per_batch, per_batch],
        out_specs=[pl.BlockSpec((1, FOX_HEADS, FOX_SLOT, tm), lambda bi, i: (bi, 0, 0, i)),
                   pl.BlockSpec((1, tm, wg.shape[1]), lambda bi, i: (bi, i, 0)),
                   pl.BlockSpec((1, 1, FOX_HEADS, 1), lambda bi, i: (bi, i, 0, 0)),
                   pl.BlockSpec((1, 1, FOX_HEADS, 1), lambda bi, i: (bi, i, 0, 0)),
                   pl.BlockSpec((1, 1, FOX_HEADS, 1), lambda bi, i: (bi, i, 0, 0))],
        out_shape=[jax.ShapeDtypeStruct((b, FOX_HEADS, FOX_SLOT, s), BF16),
                   jax.ShapeDtypeStruct((b, s, wg.shape[1]), F32),
                   jax.ShapeDtypeStruct((b, s // tm, FOX_HEADS, 1), jnp.int32),
                   jax.ShapeDtypeStruct((b, s // tm, FOX_HEADS, 1), jnp.int32),
                   jax.ShapeDtypeStruct((b, s // tm, FOX_HEADS, 1), F32)],
        compiler_params=_params("parallel", "parallel"),
        name="fox_q",
    )(x, wqt, wg, crelt, spreadt, cblk_t, cend_t, kpm_t)


def _fox_attn_kernel(cblk_ref, need_ref, need_fixed_ref, shift_ref, qt_ref, k_ref, vt_ref, g_ref, o_ref,
                     acc_ref, *bufs, nkv):
    bi = pl.program_id(0)
    hp = pl.program_id(1)
    i = pl.program_id(2)
    tq = qt_ref.shape[3]
    tk = k_ref.shape[3]
    nhp = FOX_HEADS_PER_STEP
    assert nhp == 2
    dh = FOX_HEAD_DIM
    unroll = len(bufs)

    def head_base(hh):
        return (bi * FOX_HEADS + hp * nhp + hh) * nkv

    def scores(hh, j):
        return _dot(k_ref[0, j, hh], qt_ref[0, hh])

    key_pos = lax.broadcasted_iota(jnp.int32, (tk, tq), 0)
    qry_pos = lax.broadcasted_iota(jnp.int32, (tk, tq), 1)
    bound = [shift_ref[head_base(hh) + i] for hh in range(nhp)]
    fixed_shift_ok = jnp.maximum(bound[0], bound[1]) <= MAX_FIXED_SHIFT

    @pl.when(fixed_shift_ok)
    def _():
        for hh in range(nhp):
            p = jnp.where(key_pos <= qry_pos, jnp.exp2(scores(hh, i) - bound[hh]), 0.0)
            acc_ref[hh] = _dot(vt_ref[0, i, hh], p.astype(BF16))

        n0 = need_fixed_ref[head_base(0) + i]
        total = n0 + need_fixed_ref[head_base(1) + i]

        def item(e):
            hh = (e >= n0).astype(jnp.int32)
            j = jnp.maximum(i - 1 - (e - hh * n0), 0)
            base = head_base(hh)
            shift = cblk_ref[base + i] - cblk_ref[base + j] - shift_ref[base + i]
            return hh, j, jnp.where(e < total, shift, -jnp.inf)

        def probabilities(e, p_ref):
            hh, j, shift = item(e)
            p_ref[...] = jnp.exp2(scores(hh, j) + shift).astype(BF16)

        def accumulate(e, p_ref):
            hh, j, _ = item(e)
            acc_ref[hh] = acc_ref[hh] + _dot(vt_ref[0, j, hh], p_ref[...])

        def run_group(first, size):
            for u in range(size):
                probabilities(first + u, bufs[u])
            for u in range(size):
                accumulate(first + u, bufs[u])

        def full_group(t, carry):
            run_group(unroll * t, unroll)
            return carry

        n_full = total // unroll
        lax.fori_loop(0, n_full, full_group, 0)
        rest = total - n_full * unroll

        @pl.when(rest > unroll // 2)
        def _():
            run_group(n_full * unroll, unroll)

        @pl.when((rest > 0) & (rest <= unroll // 2))
        def _():
            run_group(n_full * unroll, unroll // 2)

    @pl.when(jnp.logical_not(fixed_shift_ok))
    def _():
        for hh in range(nhp):
            base = head_base(hh)
            s = jnp.where(key_pos <= qry_pos, scores(hh, i), -jnp.inf)
            m = jnp.max(s, axis=0, keepdims=True)
            acc_ref[hh] = _dot(vt_ref[0, i, hh], jnp.exp2(s - m).astype(BF16))

            def step(jj, m, hh=hh, base=base):
                j = i - 1 - jj
                s = scores(hh, j) + (cblk_ref[base + i] - cblk_ref[base + j])
                m_new = jnp.maximum(m, jnp.max(s, axis=0, keepdims=True))
                acc_ref[hh] = (jnp.exp2(m - m_new) * acc_ref[hh]
                               + _dot(vt_ref[0, j, hh], jnp.exp2(s - m_new).astype(BF16)))
                return m_new

            lax.fori_loop(0, need_ref[base + i], step, m)

    o = jnp.concatenate([acc_ref[hh, :dh, :] / acc_ref[hh, dh:dh + 1, :] for hh in range(nhp)], axis=0)
    o_ref[0] = (jnp.transpose(o) * _sigmoid(g_ref[0])).astype(BF16)


def _fox_attn(cblk, need, need_fixed, shift, qt, k, vt, g):
    b, nkv, nh, tk, slot = k.shape
    v_rows = vt.shape[3]
    s = nkv * tk
    tq = FOX_BLOCK
    nhp = FOX_HEADS_PER_STEP
    out_w = nhp * FOX_HEAD_DIM
    smem = pl.BlockSpec(memory_space=pltpu.SMEM)
    return pl.pallas_call(
        functools.partial(_fox_attn_kernel, nkv=nkv),
        grid=(b, nh // nhp, s // tq),
        in_specs=[smem, smem, smem, smem,
                  pl.BlockSpec((1, nhp, slot, tq), lambda bi, hp, i: (bi, hp, 0, i)),
                  pl.BlockSpec((1, nkv, nhp, tk, slot), lambda bi, hp, i: (bi, 0, hp, 0, 0)),
                  pl.BlockSpec((1, nkv, nhp, v_rows, tk), lambda bi, hp, i: (bi, 0, hp, 0, 0)),
                  pl.BlockSpec((1, tq, out_w), lambda bi, hp, i: (bi, i, hp))],
        out_specs=pl.BlockSpec((1, tq, out_w), lambda bi, hp, i: (bi, i, hp)),
        out_shape=jax.ShapeDtypeStruct((b, s, nh * FOX_HEAD_DIM), BF16),
        scratch_shapes=[pltpu.VMEM((nhp, v_rows, tq), F32)] + [pltpu.VMEM((tk, tq), BF16)] * FOX_ATTN_UNROLL,
        compiler_params=_params("arbitrary", "arbitrary", "arbitrary"),
        name="fox_attn",
    )(cblk, need, need_fixed, shift, qt, k, vt, g)


def _block_tri(n, block):
    r = jnp.arange(n)
    return ((r[:, None] >= r[None, :]) & (r[:, None] // block == r[None, :] // block)).astype(BF16)


def _spread_heads(w):
    d = w.shape[0]
    w = w.reshape(d, FOX_HEADS, FOX_HEAD_DIM)
    w = jnp.pad(w, ((0, 0), (0, 0), (0, FOX_SLOT - FOX_HEAD_DIM)))
    return w.reshape(d, FOX_HEADS * FOX_SLOT)


def _head_group_matrix():
    return (jnp.arange(FOX_HEADS * FOX_SLOT)[:, None] // FOX_SLOT == jnp.arange(FOX_HEADS)[None, :]).astype(BF16)


def _bias_spread_matrix(first_lane):
    rows = jnp.arange(3 * FOX_HEADS)
    term, head = rows // FOX_HEADS, rows % FOX_HEADS
    cols = head * FOX_SLOT + first_lane + term
    return (jnp.arange(FOX_HEADS * FOX_SLOT)[None, :] == cols[:, None]).astype(BF16)


def kernel(x, p, ffn1_w_in, ffn1_w_out, ln1_g, ln1_b, gla_w_in, gla_w_a2, gla_b_a, gla_gn_g, gla_gn_b, gla_w_o, fox_w_kvf, fox_b_f, fox_w_in, fox_w_o, ln2_g, ln2_b, ffn2_w_in, ffn2_w_out, ple_w_gate, ple_w_proj, ln3_g, ln3_b):
    bsz, seq, d = x.shape
    depth = ffn1_w_in.shape[0]
    n_gla = gla_w_in.shape[0]
    alpha = (2 * depth) ** 0.25
    t = bsz * seq
    fox_w = FOX_HEADS * FOX_HEAD_DIM
    row = lambda a: a.reshape(1, -1)

    tri_chunk = _block_tri(TRI_BLOCK, GLA_CHUNK)
    tri_block = _block_tri(FOX_BLOCK, FOX_BLOCK)
    key_bias_spread = _bias_spread_matrix(FOX_HEAD_DIM)
    query_bias_spread_t = _bias_spread_matrix(FOX_HEAD_DIM + 3).T

    xf = x.reshape(t, d)
    pf = p.reshape(depth, t, p.shape[-1])
    heads_major = lambda a: jnp.transpose(a[:, :, 0, :], (0, 2, 1))
    fox_k = fox_vt = fox_crelt = cblk_t = cend_t = kpm_t = None
    for i in range(depth):
        if i == n_gla:
            wk_aug = _spread_heads(fox_w_kvf[:, :fox_w]).astype(BF16)
            wvt = fox_w_kvf[:, fox_w:2 * fox_w].T.astype(BF16)
            wf = fox_w_kvf[:, 2 * fox_w:].astype(BF16)
            fox_k, fox_vt, fox_crelt, cblk, cend, kpm = _fox_kv(
                xf.reshape(bsz, seq, d), wk_aug, wvt, wf, row(fox_b_f), tri_block, key_bias_spread,
                _head_group_matrix(), jnp.eye(3 * FOX_HEADS, dtype=BF16))
            cblk_t, cend_t, kpm_t = heads_major(cblk), heads_major(cend), heads_major(kpm)
        xf = _ffn(xf, ffn1_w_in[i].astype(BF16), ffn1_w_out[i].astype(BF16), row(ln1_g[i]), row(ln1_b[i]), alpha)
        if i < n_gla:
            qd, ki, ks, v, dec, gate = _gla_proj(xf, gla_w_in[i].astype(BF16), gla_w_a2[i].astype(BF16),
                                                 row(gla_b_a[i]), tri_chunk)
            b3 = lambda a: a.reshape(bsz, seq, -1)
            mix = _gla_rec(b3(qd), b3(ki), b3(ks), b3(v), dec.reshape(bsz, seq // GLA_CHUNK, 1, -1), b3(gate),
                           row(gla_gn_g[i]), row(gla_gn_b[i])).reshape(t, -1)
            w_o = gla_w_o[i]
        else:
            j = i - n_gla
            wqt = fox_w_in[j][:, :fox_w].T.astype(BF16)
            qt, g, need, need_fixed, shift = _fox_q(xf.reshape(bsz, seq, d), wqt, fox_w_in[j][:, fox_w:].astype(BF16),
                                        fox_crelt, query_bias_spread_t, cblk_t, cend_t, kpm_t)
            flat = lambda a: jnp.transpose(a[:, :, :, 0], (0, 2, 1)).reshape(-1)
            mix = _fox_attn(cblk_t.reshape(-1), flat(need), flat(need_fixed), flat(shift), qt, fox_k, fox_vt,
                            g).reshape(t, -1)
            w_o = fox_w_o[j]
        xf = _ffn(xf, ffn2_w_in[i].astype(BF16), ffn2_w_out[i].astype(BF16), row(ln3_g[i]), row(ln3_b[i]), alpha,
                  mixer=(mix, w_o.astype(BF16), row(ln2_g[i]), row(ln2_b[i])),
                  ple=(pf, i, ple_w_gate[i].astype(BF16), ple_w_proj[i].astype(BF16)))
    return xf.reshape(bsz, seq, d)
```

```python
import functools
import math

import jax
import jax.numpy as jnp
from jax import lax
from jax.experimental import pallas as pl
from jax.experimental.pallas import tpu as pltpu

F32 = jnp.float32
BF16 = jnp.bfloat16

GLA_HEADS = 4
GLA_TAU = 16.0
GLA_CHUNK = 64
FOX_HEADS = 16
FOX_HEAD_DIM = 64
LN_EPS = 1e-5

V7X_VMEM_LIMIT_BYTES = 56 * 1024 * 1024

TOKEN_TILE = 512
FFN_TOKEN_TILE = 512
FOX_BLOCK = 512
FOX_SLOT = 128
FOX_HEADS_PER_STEP = 2
TRI_BLOCK = 256
LOG2E = math.log2(math.e)
EXP2_ZERO_GAP = 136.0
NORM_SLACK = 1.01
FOX_V_ROWS = 80
FOX_ATTN_UNROLL = 4
MAX_FIXED_SHIFT = 50.0


def _params(*sems):
    return pltpu.CompilerParams(dimension_semantics=sems, vmem_limit_bytes=V7X_VMEM_LIMIT_BYTES)


def _resident(shape):
    zeros = (0,) * len(shape)
    return pl.BlockSpec(shape, lambda *_: zeros, pipeline_mode=pl.Buffered(1))


def _dot(a, b):
    return jnp.dot(a, b, preferred_element_type=F32)


def _dot_nt(a, b):
    return lax.dot_general(a, b, (((1,), (1,)), ((), ())), preferred_element_type=F32)


def _split3(a):
    hi = a.astype(BF16)
    r1 = a - hi.astype(F32)
    mid = r1.astype(BF16)
    lo = (r1 - mid.astype(F32)).astype(BF16)
    return hi, mid, lo


def _tri_cumsum(tri, a):
    hi, mid, lo = _split3(a)
    return _dot(tri, hi) + _dot(tri, mid) + _dot(tri, lo)


def _log_sigmoid(z):
    return jnp.minimum(z, 0.0) - jnp.log1p(jnp.exp(-jnp.abs(z)))


def _sigmoid(z):
    return 1.0 / (1.0 + jnp.exp(-z))


def _layer_norm(z, g, b):
    mu = jnp.mean(z, axis=-1, keepdims=True)
    zc = z - mu
    var = jnp.mean(zc * zc, axis=-1, keepdims=True)
    return zc * lax.rsqrt(var + LN_EPS) * g + b


def _ffn_kernel(*refs, alpha, d_ff, after_mixer):
    if after_mixer:
        (x_ref, mix_ref, wo_ref, g2_ref, b2_ref, win_ref, wout_ref, g_ref, b_ref,
         p_ref, wgate_ref, wproj_ref, o_ref) = refs
        x = _layer_norm(alpha * x_ref[...] + _dot(mix_ref[...], wo_ref[...]), g2_ref[...], b2_ref[...])
    else:
        x_ref, win_ref, wout_ref, g_ref, b_ref, o_ref = refs
        x = x_ref[...]
    xb = x.astype(BF16)
    gate = _dot(xb, win_ref[:, :d_ff])
    up = _dot(xb, win_ref[:, d_ff:])
    h = (gate * _sigmoid(gate) * up).astype(BF16)
    z = alpha * x + 0.5 * _dot(h, wout_ref[...])
    if after_mixer:
        ple_gate = _sigmoid(_dot(xb, wgate_ref[...]))
        z = z + ple_gate * _dot(p_ref[0].astype(BF16), wproj_ref[...])
    o_ref[...] = _layer_norm(z, g_ref[...], b_ref[...])


def _ffn(x, w_in, w_out, ln_g, ln_b, alpha, mixer=None, ple=None):
    assert (mixer is None) == (ple is None)
    t, d = x.shape
    d_ff = w_out.shape[0]
    tm = FFN_TOKEN_TILE
    row = lambda i: (i, 0)
    in_specs = [pl.BlockSpec((tm, d), row)]
    args = [x]
    if mixer is not None:
        mix, w_o, ln2_g, ln2_b = mixer
        in_specs += [pl.BlockSpec((tm, mix.shape[1]), row), _resident(w_o.shape), _resident(ln2_g.shape),
                     _resident(ln2_b.shape)]
        args += [mix, w_o, ln2_g, ln2_b]
    in_specs += [_resident(w_in.shape), _resident(w_out.shape), _resident(ln_g.shape), _resident(ln_b.shape)]
    args += [w_in, w_out, ln_g, ln_b]
    if ple is not None:
        p, layer, w_gate, w_proj = ple
        in_specs += [pl.BlockSpec((1, tm, p.shape[2]), lambda i: (layer, i, 0)), _resident(w_gate.shape),
                     _resident(w_proj.shape)]
        args += [p, w_gate, w_proj]
    return pl.pallas_call(
        functools.partial(_ffn_kernel, alpha=alpha, d_ff=d_ff, after_mixer=mixer is not None),
        grid=(t // tm,),
        in_specs=in_specs,
        out_specs=pl.BlockSpec((tm, d), row),
        out_shape=jax.ShapeDtypeStruct((t, d), F32),
        compiler_params=_params("parallel"),
        name="ffn_after_mixer" if mixer is not None else "ffn",
    )(*args)


def _gla_proj_kernel(x_ref, win_ref, wa2_ref, ba_ref, tri_ref,
                     qd_ref, ki_ref, ks_ref, v_ref, dec_ref, r_ref, *, kd, vd):
    tm = x_ref.shape[0]
    xb = x_ref[...].astype(BF16)
    a_lr = _dot(xb, win_ref[:, 2 * kd + 2 * vd:])
    log_a = _log_sigmoid(_dot(a_lr.astype(BF16), wa2_ref[...]) + ba_ref[...]) * (1.0 / GLA_TAU)
    qk = _dot(xb, win_ref[:, :2 * kd])
    q = qk[:, :kd] * ((kd // GLA_HEADS) ** -0.5)
    k = qk[:, kd:]
    r = _dot(xb, win_ref[:, 2 * kd + vd:2 * kd + 2 * vd])
    v = _dot(xb, win_ref[:, 2 * kd:2 * kd + vd])
    tri = tri_ref[...]
    bcum = jnp.concatenate(
        [_tri_cumsum(tri, log_a[s:s + TRI_BLOCK]) for s in range(0, tm, TRI_BLOCK)], axis=0)
    nc = tm // GLA_CHUNK
    b3 = bcum.reshape(nc, GLA_CHUNK, kd)
    b_last = b3[:, GLA_CHUNK - 1:GLA_CHUNK, :]
    qd_ref[...] = (q * jnp.exp(bcum)).astype(BF16)
    ki_ref[...] = (k * jnp.exp(-bcum)).astype(BF16)
    ks_ref[...] = (k.reshape(nc, GLA_CHUNK, kd) * jnp.exp(b_last - b3)).reshape(tm, kd).astype(BF16)
    dec_ref[...] = jnp.exp(b_last)
    v_ref[...] = v.astype(BF16)
    r_ref[...] = r


def _gla_proj(x, w_in, w_a2, b_a, tri):
    t, d = x.shape
    kd = w_a2.shape[1]
    vd = (w_in.shape[1] - 2 * kd - w_a2.shape[0]) // 2
    tm = TOKEN_TILE
    row = lambda i: (i, 0)
    return pl.pallas_call(
        functools.partial(_gla_proj_kernel, kd=kd, vd=vd),
        grid=(t // tm,),
        in_specs=[pl.BlockSpec((tm, d), row), _resident(w_in.shape), _resident(w_a2.shape),
                  _resident(b_a.shape), _resident(tri.shape)],
        out_specs=[pl.BlockSpec((tm, kd), row), pl.BlockSpec((tm, kd), row), pl.BlockSpec((tm, kd), row),
                   pl.BlockSpec((tm, vd), row), pl.BlockSpec((tm // GLA_CHUNK, 1, kd), lambda i: (i, 0, 0)),
                   pl.BlockSpec((tm, vd), row)],
        out_shape=[jax.ShapeDtypeStruct((t, kd), BF16)] * 3 + [
            jax.ShapeDtypeStruct((t, vd), BF16),
            jax.ShapeDtypeStruct((t // GLA_CHUNK, 1, kd), F32),
            jax.ShapeDtypeStruct((t, vd), F32)],
        compiler_params=_params("parallel"),
        name="gla_proj",
    )(x, w_in, w_a2, b_a, tri)


def _gla_rec_kernel(qd_ref, ki_ref, ks_ref, v_ref, dec_ref, r_ref, gng_ref, gnb_ref, o_ref, st_ref):
    tm = qd_ref.shape[1]
    dk = qd_ref.shape[2] // GLA_HEADS
    dv = v_ref.shape[2] // GLA_HEADS

    @pl.when(pl.program_id(1) == 0)
    def _():
        st_ref[...] = jnp.zeros_like(st_ref)

    rows = lax.broadcasted_iota(jnp.int32, (GLA_CHUNK, GLA_CHUNK), 0)
    cols = lax.broadcasted_iota(jnp.int32, (GLA_CHUNK, GLA_CHUNK), 1)
    causal = cols <= rows

    def chunk(c, carry):
        r0 = pl.multiple_of(c * GLA_CHUNK, GLA_CHUNK)
        rs = pl.ds(r0, GLA_CHUNK)
        for h in range(GLA_HEADS):
            ksl = slice(h * dk, (h + 1) * dk)
            vsl = slice(h * dv, (h + 1) * dv)
            qd = qd_ref[0, rs, ksl]
            ki = ki_ref[0, rs, ksl]
            ks = ks_ref[0, rs, ksl]
            vv = v_ref[0, rs, vsl]
            attn = jnp.where(causal, _dot_nt(qd, ki), 0.0).astype(BF16)
            st = st_ref[h]
            o = _dot(attn, vv) + _dot_nt(qd, st.astype(BF16))
            vt = jnp.transpose(vv.astype(F32)).astype(BF16)
            st_ref[h] = st * dec_ref[0, c, :, ksl] + _dot(vt, ks)
            on = _layer_norm(o, gng_ref[:, vsl], gnb_ref[:, vsl])
            r = r_ref[0, rs, vsl]
            o_ref[0, rs, vsl] = (on * (r * _sigmoid(r))).astype(BF16)
        return carry

    lax.fori_loop(0, tm // GLA_CHUNK, chunk, 0, unroll=True)


def _gla_rec(qd, ki, ks, v, dec, r, gn_g, gn_b):
    b, s, kd = qd.shape
    vd = v.shape[2]
    tm = TOKEN_TILE
    blk = lambda bi, i: (bi, i, 0)
    return pl.pallas_call(
        _gla_rec_kernel,
        grid=(b, s // tm),
        in_specs=[pl.BlockSpec((1, tm, kd), blk), pl.BlockSpec((1, tm, kd), blk), pl.BlockSpec((1, tm, kd), blk),
                  pl.BlockSpec((1, tm, vd), blk),
                  pl.BlockSpec((1, tm // GLA_CHUNK, 1, kd), lambda bi, i: (bi, i, 0, 0)),
                  pl.BlockSpec((1, tm, vd), blk), _resident(gn_g.shape), _resident(gn_b.shape)],
        out_specs=pl.BlockSpec((1, tm, vd), blk),
        out_shape=jax.ShapeDtypeStruct((b, s, vd), BF16),
        scratch_shapes=[pltpu.VMEM((GLA_HEADS, vd // GLA_HEADS, kd // GLA_HEADS), F32)],
        compiler_params=_params("arbitrary", "arbitrary"),
        name="gla_rec",
    )(qd, ki, ks, v, dec, r, gn_g, gn_b)


def _fox_kv_kernel(x_ref, wk_ref, wvt_ref, wf_ref, bf_ref, tri_ref, spread_ref, group_ref, eye_ref,
                   k_ref, vt_ref, crelt_ref, cblk_ref, cend_ref, kpm_ref, carry_ref, kmax_ref):
    @pl.when(pl.program_id(1) == 0)
    def _():
        carry_ref[...] = jnp.zeros_like(carry_ref)
        kmax_ref[...] = jnp.zeros_like(kmax_ref)

    tk = x_ref.shape[1]
    xb = x_ref[0].astype(BF16)
    log_f = _log_sigmoid(_dot(xb, wf_ref[...]) + bf_ref[...])
    c_rel = _tri_cumsum(tri_ref[...], log_f) * LOG2E
    terms = jnp.concatenate(_split3(-c_rel), axis=1)
    bias = _dot(terms, spread_ref[...])
    crelt_ref[0, 0] = -_dot_nt(eye_ref[...], terms)
    k = _dot(xb, wk_ref[...])
    slot_lane = lax.broadcasted_iota(jnp.int32, k.shape, 1) % FOX_SLOT
    is_one_lane = (slot_lane >= FOX_HEAD_DIM + 3) & (slot_lane < FOX_HEAD_DIM + 6)
    k_aug = jnp.where(is_one_lane, 1.0, k + bias).astype(BF16)
    for h in range(FOX_HEADS):
        k_ref[0, 0, h] = k_aug[:, h * FOX_SLOT:(h + 1) * FOX_SLOT]
    vt = _dot_nt(wvt_ref[...], xb).astype(BF16).reshape(FOX_HEADS, FOX_HEAD_DIM, tk)
    pad_row = lax.broadcasted_iota(jnp.int32, (FOX_HEADS, FOX_V_ROWS - FOX_HEAD_DIM, tk), 1)
    vt_ref[0, 0] = jnp.concatenate([vt, jnp.where(pad_row == 0, 1.0, 0.0).astype(BF16)], axis=1)
    cblk_ref[0, 0] = carry_ref[...]
    carry_ref[...] = carry_ref[...] + c_rel[tk - 1:tk, :]
    cend_ref[0, 0] = carry_ref[...]
    kr = k.astype(BF16).astype(F32)
    norm2 = _dot((kr * kr).astype(BF16), group_ref[...]) * NORM_SLACK
    kmax_ref[...] = jnp.maximum(kmax_ref[...], jnp.sqrt(jnp.max(norm2, axis=0, keepdims=True)))
    kpm_ref[0, 0] = kmax_ref[...]


def _fox_kv(x, wk_aug, wvt, wf, b_f, tri, spread, group, eye):
    b, s, d = x.shape
    tk = FOX_BLOCK
    nkv = s // tk
    nh = wf.shape[1]
    blk4 = lambda bi, j: (bi, j, 0, 0)
    blk5 = lambda bi, j: (bi, j, 0, 0, 0)
    small = jax.ShapeDtypeStruct((b, nkv, 1, nh), F32)
    return pl.pallas_call(
        _fox_kv_kernel,
        grid=(b, nkv),
        in_specs=[pl.BlockSpec((1, tk, d), lambda bi, j: (bi, j, 0)), _resident(wk_aug.shape),
                  _resident(wvt.shape), _resident(wf.shape), _resident(b_f.shape), _resident(tri.shape),
                  _resident(spread.shape), _resident(group.shape), _resident(eye.shape)],
        out_specs=[pl.BlockSpec((1, 1, nh, tk, FOX_SLOT), blk5),
                   pl.BlockSpec((1, 1, nh, FOX_V_ROWS, tk), blk5),
                   pl.BlockSpec((1, 1, 3 * nh, tk), blk4),
                   pl.BlockSpec((1, 1, 1, nh), blk4), pl.BlockSpec((1, 1, 1, nh), blk4),
                   pl.BlockSpec((1, 1, 1, nh), blk4)],
        out_shape=[jax.ShapeDtypeStruct((b, nkv, nh, tk, FOX_SLOT), BF16),
                   jax.ShapeDtypeStruct((b, nkv, nh, FOX_V_ROWS, tk), BF16),
                   jax.ShapeDtypeStruct((b, nkv, 3 * nh, tk), F32),
                   small, small, small],
        scratch_shapes=[pltpu.VMEM((1, nh), F32), pltpu.VMEM((1, nh), F32)],
        compiler_params=_params("arbitrary", "arbitrary"),
        name="fox_kv",
    )(x, wk_aug, wvt, wf, b_f, tri, spread, group, eye)


def _fox_q_kernel(x_ref, wqt_ref, wg_ref, crelt_ref, spreadt_ref, cblk_ref, cend_ref, kpm_ref,
                  qt_ref, g_ref, need_ref, need_fixed_ref, shift_ref):
    i = pl.program_id(1)
    tq = x_ref.shape[1]
    xb = x_ref[0].astype(BF16)
    q = _dot_nt(wqt_ref[...], xb) * (FOX_HEAD_DIM ** -0.5 * LOG2E)
    q = q.reshape(FOX_HEADS, FOX_HEAD_DIM, tq)
    qt = jnp.concatenate([q, jnp.zeros((FOX_HEADS, FOX_SLOT - FOX_HEAD_DIM, tq), F32)], axis=1)
    qt = qt.reshape(FOX_HEADS * FOX_SLOT, tq)
    c_rows = _dot(spreadt_ref[...], crelt_ref[0, 0].astype(BF16))
    qt3 = (qt + c_rows).reshape(FOX_HEADS, FOX_SLOT, tq)
    slot_row = lax.broadcasted_iota(jnp.int32, qt3.shape, 1)
    is_one_row = (slot_row >= FOX_HEAD_DIM) & (slot_row < FOX_HEAD_DIM + 3)
    qt_ref[0] = jnp.where(is_one_row, 1.0, qt3).astype(BF16)
    g_ref[0] = _dot(xb, wg_ref[...])

    qr = qt.astype(BF16).astype(F32).reshape(FOX_HEADS, FOX_SLOT, tq)
    qn = jnp.sqrt(jnp.max(jnp.sum(qr * qr, axis=1), axis=1, keepdims=True))
    blk = lax.broadcasted_iota(jnp.int32, cblk_ref.shape[1:], 1)
    at_i = blk == i
    c_before = jnp.sum(jnp.where(at_i, cblk_ref[0], 0.0), axis=1, keepdims=True)
    k_norm = jnp.sum(jnp.where(at_i, kpm_ref[0], 0.0), axis=1, keepdims=True)
    qk_bound = NORM_SLACK * qn * k_norm

    def count_blocks(thr):
        needed = (blk < i) & jnp.logical_not(cend_ref[0] > thr)
        return jnp.sum(jnp.where(needed, 1.0, 0.0), axis=1, keepdims=True).astype(jnp.int32)

    need_ref[0, 0] = count_blocks(c_before + EXP2_ZERO_GAP + 2.0 * qk_bound)
    need_fixed_ref[0, 0] = count_blocks(c_before + EXP2_ZERO_GAP)
    shift_ref[0, 0] = qk_bound


def _fox_q(x, wqt, wg, crelt, spreadt, cblk_t, cend_t, kpm_t):
    b, s, d = x.shape
    tm = FOX_BLOCK
    nkv = cblk_t.shape[2]
    per_batch = pl.BlockSpec((1, FOX_HEADS, nkv), lambda bi, i: (bi, 0, 0))
    return pl.pallas_call(
        _fox_q_kernel,
        grid=(b, s // tm),
        in_specs=[pl.BlockSpec((1, tm, d), lambda bi, i: (bi, i, 0)), _resident(wqt.shape),
                  _resident(wg.shape),
                  pl.BlockSpec((1, 1, crelt.shape[2], tm), lambda bi, i: (bi, i, 0, 0)),
                  _resident(spreadt.shape), per_batch, per_batch, per_batch],
        out_specs=[pl.BlockSpec((1, FOX_HEADS, FOX_SLOT, tm), lambda bi, i: (bi, 0, 0, i)),
                   pl.BlockSpec((1, tm, wg.shape[1]), lambda bi, i: (bi, i, 0)),
                   pl.BlockSpec((1, 1, FOX_HEADS, 1), lambda bi, i: (bi, i, 0, 0)),
                   pl.BlockSpec((1, 1, FOX_HEADS, 1), lambda bi, i: (bi, i, 0, 0)),
                   pl.BlockSpec((1, 1, FOX_HEADS, 1), lambda bi, i: (bi, i, 0, 0))],
        out_shape=[jax.ShapeDtypeStruct((b, FOX_HEADS, FOX_SLOT, s), BF16),
                   jax.ShapeDtypeStruct((b, s, wg.shape[1]), F32),
                   jax.ShapeDtypeStruct((b, s // tm, FOX_HEADS, 1), jnp.int32),
                   jax.ShapeDtypeStruct((b, s // tm, FOX_HEADS, 1), jnp.int32),
                   jax.ShapeDtypeStruct((b, s // tm, FOX_HEADS, 1), F32)],
        compiler_params=_params("parallel", "parallel"),
        name="fox_q",
    )(x, wqt, wg, crelt, spreadt, cblk_t, cend_t, kpm_t)


def _fox_attn_kernel(cblk_ref, need_ref, need_fixed_ref, shift_ref, qt_ref, k_ref, vt_ref, g_ref, o_ref,
                     acc_ref, *bufs, nkv):
    bi = pl.program_id(0)
    hp = pl.program_id(1)
    i = pl.program_id(2)
    tq = qt_ref.shape[3]
    tk = k_ref.shape[3]
    nhp = FOX_HEADS_PER_STEP
    assert nhp == 2
    dh = FOX_HEAD_DIM
    unroll = len(bufs)

    def head_base(hh):
        return (bi * FOX_HEADS + hp * nhp + hh) * nkv

    def scores(hh, j):
        return _dot(k_ref[0, j, hh], qt_ref[0, hh])

    key_pos = lax.broadcasted_iota(jnp.int32, (tk, tq), 0)
    qry_pos = lax.broadcasted_iota(jnp.int32, (tk, tq), 1)
    bound = [shift_ref[head_base(hh) + i] for hh in range(nhp)]
    fixed_shift_ok = jnp.maximum(bound[0], bound[1]) <= MAX_FIXED_SHIFT

    @pl.when(fixed_shift_ok)
    def _():
        n0 = need_fixed_ref[head_base(0) + i]
        total = n0 + need_fixed_ref[head_base(1) + i]
        lead = unroll - nhp

        def item(e):
            hh = (e >= n0).astype(jnp.int32)
            j = jnp.maximum(i - 1 - (e - hh * n0), 0)
            base = head_base(hh)
            shift = cblk_ref[base + i] - cblk_ref[base + j] - shift_ref[base + i]
            return hh, j, jnp.where(e < total, shift, -jnp.inf)

        def probabilities(e, p_ref):
            hh, j, shift = item(e)
            p_ref[...] = jnp.exp2(scores(hh, j) + shift).astype(BF16)

        def accumulate(e, p_ref):
            hh, j, _ = item(e)
            acc_ref[hh] = acc_ref[hh] + _dot(vt_ref[0, j, hh], p_ref[...])

        def run_group(first, size):
            for u in range(size):
                probabilities(first + u, bufs[u])
            for u in range(size):
                accumulate(first + u, bufs[u])

        for hh in range(nhp):
            p = jnp.where(key_pos <= qry_pos, jnp.exp2(scores(hh, i) - bound[hh]), 0.0)
            bufs[hh][...] = p.astype(BF16)
        for u in range(lead):
            probabilities(u, bufs[nhp + u])
        for hh in range(nhp):
            acc_ref[hh] = _dot(vt_ref[0, i, hh], bufs[hh][...])
        for u in range(lead):
            accumulate(u, bufs[nhp + u])

        def full_group(t, carry):
            run_group(lead + unroll * t, unroll)
            return carry

        left = jnp.maximum(total - lead, 0)
        n_full = left // unroll
        lax.fori_loop(0, n_full, full_group, 0)
        rest = left - n_full * unroll

        @pl.when(rest > unroll // 2)
        def _():
            run_group(lead + n_full * unroll, unroll)

        @pl.when((rest > 0) & (rest <= unroll // 2))
        def _():
            run_group(lead + n_full * unroll, unroll // 2)

    @pl.when(jnp.logical_not(fixed_shift_ok))
    def _():
        for hh in range(nhp):
            base = head_base(hh)
            s = jnp.where(key_pos <= qry_pos, scores(hh, i), -jnp.inf)
            m = jnp.max(s, axis=0, keepdims=True)
            acc_ref[hh] = _dot(vt_ref[0, i, hh], jnp.exp2(s - m).astype(BF16))

            def step(jj, m, hh=hh, base=base):
                j = i - 1 - jj
                s = scores(hh, j) + (cblk_ref[base + i] - cblk_ref[base + j])
                m_new = jnp.maximum(m, jnp.max(s, axis=0, keepdims=True))
                acc_ref[hh] = (jnp.exp2(m - m_new) * acc_ref[hh]
                               + _dot(vt_ref[0, j, hh], jnp.exp2(s - m_new).astype(BF16)))
                return m_new

            lax.fori_loop(0, need_ref[base + i], step, m)

    o = jnp.concatenate([acc_ref[hh, :dh, :] / acc_ref[hh, dh:dh + 1, :] for hh in range(nhp)], axis=0)
    o_ref[0] = (jnp.transpose(o) * _sigmoid(g_ref[0])).astype(BF16)


def _fox_attn(cblk, need, need_fixed, shift, qt, k, vt, g):
    b, nkv, nh, tk, slot = k.shape
    v_rows = vt.shape[3]
    s = nkv * tk
    tq = FOX_BLOCK
    nhp = FOX_HEADS_PER_STEP
    out_w = nhp * FOX_HEAD_DIM
    smem = pl.BlockSpec(memory_space=pltpu.SMEM)
    return pl.pallas_call(
        functools.partial(_fox_attn_kernel, nkv=nkv),
        grid=(b, nh // nhp, s // tq),
        in_specs=[smem, smem, smem, smem,
                  pl.BlockSpec((1, nhp, slot, tq), lambda bi, hp, i: (bi, hp, 0, i)),
                  pl.BlockSpec((1, nkv, nhp, tk, slot), lambda bi, hp, i: (bi, 0, hp, 0, 0)),
                  pl.BlockSpec((1, nkv, nhp, v_rows, tk), lambda bi, hp, i: (bi, 0, hp, 0, 0)),
                  pl.BlockSpec((1, tq, out_w), lambda bi, hp, i: (bi, i, hp))],
        out_specs=pl.BlockSpec((1, tq, out_w), lambda bi, hp, i: (bi, i, hp)),
        out_shape=jax.ShapeDtypeStruct((b, s, nh * FOX_HEAD_DIM), BF16),
        scratch_shapes=[pltpu.VMEM((nhp, v_rows, tq), F32)] + [pltpu.VMEM((tk, tq), BF16)] * FOX_ATTN_UNROLL,
        compiler_params=_params("arbitrary", "arbitrary", "arbitrary"),
        name="fox_attn",
    )(cblk, need, need_fixed, shift, qt, k, vt, g)


def _block_tri(n, block):
    r = jnp.arange(n)
    return ((r[:, None] >= r[None, :]) & (r[:, None] // block == r[None, :] // block)).astype(BF16)


def _spread_heads(w):
    d = w.shape[0]
    w = w.reshape(d, FOX_HEADS, FOX_HEAD_DIM)
    w = jnp.pad(w, ((0, 0), (0, 0), (0, FOX_SLOT - FOX_HEAD_DIM)))
    return w.reshape(d, FOX_HEADS * FOX_SLOT)


def _head_group_matrix():
    return (jnp.arange(FOX_HEADS * FOX_SLOT)[:, None] // FOX_SLOT == jnp.arange(FOX_HEADS)[None, :]).astype(BF16)


def _bias_spread_matrix(first_lane):
    rows = jnp.arange(3 * FOX_HEADS)
    term, head = rows // FOX_HEADS, rows % FOX_HEADS
    cols = head * FOX_SLOT + first_lane + term
    return (jnp.arange(FOX_HEADS * FOX_SLOT)[None, :] == cols[:, None]).astype(BF16)


def kernel(x, p, ffn1_w_in, ffn1_w_out, ln1_g, ln1_b, gla_w_in, gla_w_a2, gla_b_a, gla_gn_g, gla_gn_b, gla_w_o, fox_w_kvf, fox_b_f, fox_w_in, fox_w_o, ln2_g, ln2_b, ffn2_w_in, ffn2_w_out, ple_w_gate, ple_w_proj, ln3_g, ln3_b):
    bsz, seq, d = x.shape
    depth = ffn1_w_in.shape[0]
    n_gla = gla_w_in.shape[0]
    alpha = (2 * depth) ** 0.25
    t = bsz * seq
    fox_w = FOX_HEADS * FOX_HEAD_DIM
    row = lambda a: a.reshape(1, -1)

    tri_chunk = _block_tri(TRI_BLOCK, GLA_CHUNK)
    tri_block = _block_tri(FOX_BLOCK, FOX_BLOCK)
    key_bias_spread = _bias_spread_matrix(FOX_HEAD_DIM)
    query_bias_spread_t = _bias_spread_matrix(FOX_HEAD_DIM + 3).T

    xf = x.reshape(t, d)
    pf = p.reshape(depth, t, p.shape[-1])
    heads_major = lambda a: jnp.transpose(a[:, :, 0, :], (0, 2, 1))
    fox_k = fox_vt = fox_crelt = cblk_t = cend_t = kpm_t = None
    for i in range(depth):
        if i == n_gla:
            wk_aug = _spread_heads(fox_w_kvf[:, :fox_w]).astype(BF16)
            wvt = fox_w_kvf[:, fox_w:2 * fox_w].T.astype(BF16)
            wf = fox_w_kvf[:, 2 * fox_w:].astype(BF16)
            fox_k, fox_vt, fox_crelt, cblk, cend, kpm = _fox_kv(
                xf.reshape(bsz, seq, d), wk_aug, wvt, wf, row(fox_b_f), tri_block, key_bias_spread,
                _head_group_matrix(), jnp.eye(3 * FOX_HEADS, dtype=BF16))
            cblk_t, cend_t, kpm_t = heads_major(cblk), heads_major(cend), heads_major(kpm)
        xf = _ffn(xf, ffn1_w_in[i].astype(BF16), ffn1_w_out[i].astype(BF16), row(ln1_g[i]), row(ln1_b[i]), alpha)
        if i < n_gla:
            qd, ki, ks, v, dec, r = _gla_proj(xf, gla_w_in[i].astype(BF16), gla_w_a2[i].astype(BF16),
                                                 row(gla_b_a[i]), tri_chunk)
            b3 = lambda a: a.reshape(bsz, seq, -1)
            mix = _gla_rec(b3(qd), b3(ki), b3(ks), b3(v), dec.reshape(bsz, seq // GLA_CHUNK, 1, -1), b3(r),
                           row(gla_gn_g[i]), row(gla_gn_b[i])).reshape(t, -1)
            w_o = gla_w_o[i]
        else:
            j = i - n_gla
            wqt = fox_w_in[j][:, :fox_w].T.astype(BF16)
            qt, g, need, need_fixed, shift = _fox_q(xf.reshape(bsz, seq, d), wqt, fox_w_in[j][:, fox_w:].astype(BF16),
                                        fox_crelt, query_bias_spread_t, cblk_t, cend_t, kpm_t)
            flat = lambda a: jnp.transpose(a[:, :, :, 0], (0, 2, 1)).reshape(-1)
            mix = _fox_attn(cblk_t.reshape(-1), flat(need), flat(need_fixed), flat(shift), qt, fox_k, fox_vt,
                            g).reshape(t, -1)
            w_o = fox_w_o[j]
        xf = _ffn(xf, ffn2_w_in[i].astype(BF16), ffn2_w_out[i].astype(BF16), row(ln3_g[i]), row(ln3_b[i]), alpha,
                  mixer=(mix, w_o.astype(BF16), row(ln2_g[i]), row(ln2_b[i])),
                  ple=(pf, i, ple_w_gate[i].astype(BF16), ple_w_proj[i].astype(BF16)))
    return xf.reshape(bsz, seq, d)
```

```python
import functools
import math

import jax
import jax.numpy as jnp
from jax import lax
from jax.experimental import pallas as pl
from jax.experimental.pallas import tpu as pltpu

F32 = jnp.float32
BF16 = jnp.bfloat16

GLA_HEADS = 4
GLA_TAU = 16.0
GLA_CHUNK = 64
FOX_HEADS = 16
FOX_HEAD_DIM = 64
LN_EPS = 1e-5

V7X_VMEM_LIMIT_BYTES = 56 * 1024 * 1024

TOKEN_TILE = 512
FFN_TOKEN_TILE = 512
FOX_BLOCK = 512
FOX_SLOT = 128
FOX_HEADS_PER_STEP = 2
TRI_BLOCK = 256
LOG2E = math.log2(math.e)
EXP2_ZERO_GAP = 136.0
NORM_SLACK = 1.01
FOX_V_ROWS = 80
FOX_ATTN_UNROLL = 4
MAX_FIXED_SHIFT = 50.0


def _params(*sems):
    return pltpu.CompilerParams(dimension_semantics=sems, vmem_limit_bytes=V7X_VMEM_LIMIT_BYTES)


def _resident(shape):
    zeros = (0,) * len(shape)
    return pl.BlockSpec(shape, lambda *_: zeros, pipeline_mode=pl.Buffered(1))


def _dot(a, b):
    return jnp.dot(a, b, preferred_element_type=F32)


def _dot_nt(a, b):
    return lax.dot_general(a, b, (((1,), (1,)), ((), ())), preferred_element_type=F32)


def _split3(a):
    hi = a.astype(BF16)
    r1 = a - hi.astype(F32)
    mid = r1.astype(BF16)
    lo = (r1 - mid.astype(F32)).astype(BF16)
    return hi, mid, lo


def _tri_cumsum(tri, a):
    hi, mid, lo = _split3(a)
    return _dot(tri, hi) + _dot(tri, mid) + _dot(tri, lo)


def _log_sigmoid(z):
    return jnp.minimum(z, 0.0) - jnp.log1p(jnp.exp(-jnp.abs(z)))


def _sigmoid(z):
    return 1.0 / (1.0 + jnp.exp(-z))


def _layer_norm(z, g, b):
    mu = jnp.mean(z, axis=-1, keepdims=True)
    zc = z - mu
    var = jnp.mean(zc * zc, axis=-1, keepdims=True)
    return zc * lax.rsqrt(var + LN_EPS) * g + b


def _ffn_kernel(*refs, alpha, d_ff, after_mixer):
    if after_mixer:
        (x_ref, mix_ref, wo_ref, g2_ref, b2_ref, win_ref, wout_ref, g_ref, b_ref,
         p_ref, wgate_ref, wproj_ref, o_ref) = refs
    else:
        x_ref, win_ref, wout_ref, g_ref, b_ref, o_ref = refs
    tm = x_ref.shape[0]
    for rows in (slice(0, tm // 2), slice(tm // 2, tm)):
        if after_mixer:
            x = _layer_norm(alpha * x_ref[rows, :] + _dot(mix_ref[rows, :], wo_ref[...]), g2_ref[...], b2_ref[...])
        else:
            x = x_ref[rows, :]
        xb = x.astype(BF16)
        gate = _dot(xb, win_ref[:, :d_ff])
        up = _dot(xb, win_ref[:, d_ff:])
        h = (gate * _sigmoid(gate) * up).astype(BF16)
        z = alpha * x + 0.5 * _dot(h, wout_ref[...])
        if after_mixer:
            ple_gate = _sigmoid(_dot(xb, wgate_ref[...]))
            z = z + ple_gate * _dot(p_ref[0, rows, :].astype(BF16), wproj_ref[...])
        o_ref[rows, :] = _layer_norm(z, g_ref[...], b_ref[...])


def _ffn(x, w_in, w_out, ln_g, ln_b, alpha, mixer=None, ple=None):
    assert (mixer is None) == (ple is None)
    t, d = x.shape
    d_ff = w_out.shape[0]
    tm = FFN_TOKEN_TILE
    row = lambda i: (i, 0)
    in_specs = [pl.BlockSpec((tm, d), row)]
    args = [x]
    if mixer is not None:
        mix, w_o, ln2_g, ln2_b = mixer
        in_specs += [pl.BlockSpec((tm, mix.shape[1]), row), _resident(w_o.shape), _resident(ln2_g.shape),
                     _resident(ln2_b.shape)]
        args += [mix, w_o, ln2_g, ln2_b]
    in_specs += [_resident(w_in.shape), _resident(w_out.shape), _resident(ln_g.shape), _resident(ln_b.shape)]
    args += [w_in, w_out, ln_g, ln_b]
    if ple is not None:
        p, layer, w_gate, w_proj = ple
        in_specs += [pl.BlockSpec((1, tm, p.shape[2]), lambda i: (layer, i, 0)), _resident(w_gate.shape),
                     _resident(w_proj.shape)]
        args += [p, w_gate, w_proj]
    return pl.pallas_call(
        functools.partial(_ffn_kernel, alpha=alpha, d_ff=d_ff, after_mixer=mixer is not None),
        grid=(t // tm,),
        in_specs=in_specs,
        out_specs=pl.BlockSpec((tm, d), row),
        out_shape=jax.ShapeDtypeStruct((t, d), F32),
        compiler_params=_params("parallel"),
        name="ffn_after_mixer" if mixer is not None else "ffn",
    )(*args)


def _gla_proj_kernel(x_ref, win_ref, wa2_ref, ba_ref, tri_ref,
                     qd_ref, ki_ref, ks_ref, v_ref, dec_ref, r_ref, *, kd, vd):
    tm = x_ref.shape[0]
    xb = x_ref[...].astype(BF16)
    a_lr = _dot(xb, win_ref[:, 2 * kd + 2 * vd:])
    log_a = _log_sigmoid(_dot(a_lr.astype(BF16), wa2_ref[...]) + ba_ref[...]) * (1.0 / GLA_TAU)
    qk = _dot(xb, win_ref[:, :2 * kd])
    q = qk[:, :kd] * ((kd // GLA_HEADS) ** -0.5)
    k = qk[:, kd:]
    r = _dot(xb, win_ref[:, 2 * kd + vd:2 * kd + 2 * vd])
    v = _dot(xb, win_ref[:, 2 * kd:2 * kd + vd])
    tri = tri_ref[...]
    bcum = jnp.concatenate(
        [_tri_cumsum(tri, log_a[s:s + TRI_BLOCK]) for s in range(0, tm, TRI_BLOCK)], axis=0)
    nc = tm // GLA_CHUNK
    b3 = bcum.reshape(nc, GLA_CHUNK, kd)
    b_last = b3[:, GLA_CHUNK - 1:GLA_CHUNK, :]
    qd_ref[...] = (q * jnp.exp(bcum)).astype(BF16)
    ki_ref[...] = (k * jnp.exp(-bcum)).astype(BF16)
    ks_ref[...] = (k.reshape(nc, GLA_CHUNK, kd) * jnp.exp(b_last - b3)).reshape(tm, kd).astype(BF16)
    dec_ref[...] = jnp.exp(b_last)
    v_ref[...] = v.astype(BF16)
    r_ref[...] = r


def _gla_proj(x, w_in, w_a2, b_a, tri):
    t, d = x.shape
    kd = w_a2.shape[1]
    vd = (w_in.shape[1] - 2 * kd - w_a2.shape[0]) // 2
    tm = TOKEN_TILE
    row = lambda i: (i, 0)
    return pl.pallas_call(
        functools.partial(_gla_proj_kernel, kd=kd, vd=vd),
        grid=(t // tm,),
        in_specs=[pl.BlockSpec((tm, d), row), _resident(w_in.shape), _resident(w_a2.shape),
                  _resident(b_a.shape), _resident(tri.shape)],
        out_specs=[pl.BlockSpec((tm, kd), row), pl.BlockSpec((tm, kd), row), pl.BlockSpec((tm, kd), row),
                   pl.BlockSpec((tm, vd), row), pl.BlockSpec((tm // GLA_CHUNK, 1, kd), lambda i: (i, 0, 0)),
                   pl.BlockSpec((tm, vd), row)],
        out_shape=[jax.ShapeDtypeStruct((t, kd), BF16)] * 3 + [
            jax.ShapeDtypeStruct((t, vd), BF16),
            jax.ShapeDtypeStruct((t // GLA_CHUNK, 1, kd), F32),
            jax.ShapeDtypeStruct((t, vd), F32)],
        compiler_params=_params("parallel"),
        name="gla_proj",
    )(x, w_in, w_a2, b_a, tri)


def _gla_rec_kernel(qd_ref, ki_ref, ks_ref, v_ref, dec_ref, r_ref, gng_ref, gnb_ref, o_ref, st_ref):
    tm = qd_ref.shape[1]
    dk = qd_ref.shape[2] // GLA_HEADS
    dv = v_ref.shape[2] // GLA_HEADS

    @pl.when(pl.program_id(1) == 0)
    def _():
        st_ref[...] = jnp.zeros_like(st_ref)

    rows = lax.broadcasted_iota(jnp.int32, (GLA_CHUNK, GLA_CHUNK), 0)
    cols = lax.broadcasted_iota(jnp.int32, (GLA_CHUNK, GLA_CHUNK), 1)
    causal = cols <= rows

    def chunk(c, carry):
        r0 = pl.multiple_of(c * GLA_CHUNK, GLA_CHUNK)
        rs = pl.ds(r0, GLA_CHUNK)
        for h in range(GLA_HEADS):
            ksl = slice(h * dk, (h + 1) * dk)
            vsl = slice(h * dv, (h + 1) * dv)
            qd = qd_ref[0, rs, ksl]
            ki = ki_ref[0, rs, ksl]
            ks = ks_ref[0, rs, ksl]
            vv = v_ref[0, rs, vsl]
            attn = jnp.where(causal, _dot_nt(qd, ki), 0.0).astype(BF16)
            st = st_ref[h]
            o = _dot(attn, vv) + _dot_nt(qd, st.astype(BF16))
            vt = jnp.transpose(vv.astype(F32)).astype(BF16)
            st_ref[h] = st * dec_ref[0, c, :, ksl] + _dot(vt, ks)
            on = _layer_norm(o, gng_ref[:, vsl], gnb_ref[:, vsl])
            r = r_ref[0, rs, vsl]
            o_ref[0, rs, vsl] = (on * (r * _sigmoid(r))).astype(BF16)
        return carry

    lax.fori_loop(0, tm // GLA_CHUNK, chunk, 0, unroll=True)


def _gla_rec(qd, ki, ks, v, dec, r, gn_g, gn_b):
    b, s, kd = qd.shape
    vd = v.shape[2]
    tm = TOKEN_TILE
    blk = lambda bi, i: (bi, i, 0)
    return pl.pallas_call(
        _gla_rec_kernel,
        grid=(b, s // tm),
        in_specs=[pl.BlockSpec((1, tm, kd), blk), pl.BlockSpec((1, tm, kd), blk), pl.BlockSpec((1, tm, kd), blk),
                  pl.BlockSpec((1, tm, vd), blk),
                  pl.BlockSpec((1, tm // GLA_CHUNK, 1, kd), lambda bi, i: (bi, i, 0, 0)),
                  pl.BlockSpec((1, tm, vd), blk), _resident(gn_g.shape), _resident(gn_b.shape)],
        out_specs=pl.BlockSpec((1, tm, vd), blk),
        out_shape=jax.ShapeDtypeStruct((b, s, vd), BF16),
        scratch_shapes=[pltpu.VMEM((GLA_HEADS, vd // GLA_HEADS, kd // GLA_HEADS), F32)],
        compiler_params=_params("arbitrary", "arbitrary"),
        name="gla_rec",
    )(qd, ki, ks, v, dec, r, gn_g, gn_b)


def _fox_kv_kernel(x_ref, wk_ref, wvt_ref, wf_ref, bf_ref, tri_ref, spread_ref, group_ref, eye_ref,
                   k_ref, vt_ref, crelt_ref, cblk_ref, cend_ref, kpm_ref, carry_ref, kmax_ref):
    @pl.when(pl.program_id(1) == 0)
    def _():
        carry_ref[...] = jnp.zeros_like(carry_ref)
        kmax_ref[...] = jnp.zeros_like(kmax_ref)

    tk = x_ref.shape[1]
    xb = x_ref[0].astype(BF16)
    log_f = _log_sigmoid(_dot(xb, wf_ref[...]) + bf_ref[...])
    c_rel = _tri_cumsum(tri_ref[...], log_f) * LOG2E
    terms = jnp.concatenate(_split3(-c_rel), axis=1)
    bias = _dot(terms, spread_ref[...])
    crelt_ref[0, 0] = -_dot_nt(eye_ref[...], terms)
    k = _dot(xb, wk_ref[...])
    slot_lane = lax.broadcasted_iota(jnp.int32, k.shape, 1) % FOX_SLOT
    is_one_lane = (slot_lane >= FOX_HEAD_DIM + 3) & (slot_lane < FOX_HEAD_DIM + 6)
    k_aug = jnp.where(is_one_lane, 1.0, k + bias).astype(BF16)
    for h in range(FOX_HEADS):
        k_ref[0, 0, h] = k_aug[:, h * FOX_SLOT:(h + 1) * FOX_SLOT]
    vt = _dot_nt(wvt_ref[...], xb).astype(BF16).reshape(FOX_HEADS, FOX_HEAD_DIM, tk)
    pad_row = lax.broadcasted_iota(jnp.int32, (FOX_HEADS, FOX_V_ROWS - FOX_HEAD_DIM, tk), 1)
    vt_ref[0, 0] = jnp.concatenate([vt, jnp.where(pad_row == 0, 1.0, 0.0).astype(BF16)], axis=1)
    cblk_ref[0, 0] = carry_ref[...]
    carry_ref[...] = carry_ref[...] + c_rel[tk - 1:tk, :]
    cend_ref[0, 0] = carry_ref[...]
    kr = k.astype(BF16).astype(F32)
    norm2 = _dot((kr * kr).astype(BF16), group_ref[...]) * NORM_SLACK
    kmax_ref[...] = jnp.maximum(kmax_ref[...], jnp.sqrt(jnp.max(norm2, axis=0, keepdims=True)))
    kpm_ref[0, 0] = kmax_ref[...]


def _fox_kv(x, wk_aug, wvt, wf, b_f, tri, spread, group, eye):
    b, s, d = x.shape
    tk = FOX_BLOCK
    nkv = s // tk
    nh = wf.shape[1]
    blk4 = lambda bi, j: (bi, j, 0, 0)
    blk5 = lambda bi, j: (bi, j, 0, 0, 0)
    small = jax.ShapeDtypeStruct((b, nkv, 1, nh), F32)
    return pl.pallas_call(
        _fox_kv_kernel,
        grid=(b, nkv),
        in_specs=[pl.BlockSpec((1, tk, d), lambda bi, j: (bi, j, 0)), _resident(wk_aug.shape),
                  _resident(wvt.shape), _resident(wf.shape), _resident(b_f.shape), _resident(tri.shape),
                  _resident(spread.shape), _resident(group.shape), _resident(eye.shape)],
        out_specs=[pl.BlockSpec((1, 1, nh, tk, FOX_SLOT), blk5),
                   pl.BlockSpec((1, 1, nh, FOX_V_ROWS, tk), blk5),
                   pl.BlockSpec((1, 1, 3 * nh, tk), blk4),
                   pl.BlockSpec((1, 1, 1, nh), blk4), pl.BlockSpec((1, 1, 1, nh), blk4),
                   pl.BlockSpec((1, 1, 1, nh), blk4)],
        out_shape=[jax.ShapeDtypeStruct((b, nkv, nh, tk, FOX_SLOT), BF16),
                   jax.ShapeDtypeStruct((b, nkv, nh, FOX_V_ROWS, tk), BF16),
                   jax.ShapeDtypeStruct((b, nkv, 3 * nh, tk), F32),
                   small, small, small],
        scratch_shapes=[pltpu.VMEM((1, nh), F32), pltpu.VMEM((1, nh), F32)],
        compiler_params=_params("arbitrary", "arbitrary"),
        name="fox_kv",
    )(x, wk_aug, wvt, wf, b_f, tri, spread, group, eye)


def _fox_q_kernel(x_ref, wqt_ref, wg_ref, crelt_ref, spreadt_ref, cblk_ref, cend_ref, kpm_ref,
                  qt_ref, g_ref, need_ref, need_fixed_ref, shift_ref):
    i = pl.program_id(1)
    tq = x_ref.shape[1]
    xb = x_ref[0].astype(BF16)
    q = _dot_nt(wqt_ref[...], xb) * (FOX_HEAD_DIM ** -0.5 * LOG2E)
    q = q.reshape(FOX_HEADS, FOX_HEAD_DIM, tq)
    qt = jnp.concatenate([q, jnp.zeros((FOX_HEADS, FOX_SLOT - FOX_HEAD_DIM, tq), F32)], axis=1)
    qt = qt.reshape(FOX_HEADS * FOX_SLOT, tq)
    c_rows = _dot(spreadt_ref[...], crelt_ref[0, 0].astype(BF16))
    qt3 = (qt + c_rows).reshape(FOX_HEADS, FOX_SLOT, tq)
    slot_row = lax.broadcasted_iota(jnp.int32, qt3.shape, 1)
    is_one_row = (slot_row >= FOX_HEAD_DIM) & (slot_row < FOX_HEAD_DIM + 3)
    qt_ref[0] = jnp.where(is_one_row, 1.0, qt3).astype(BF16)
    g_ref[0] = _dot(xb, wg_ref[...])

    qr = qt.astype(BF16).astype(F32).reshape(FOX_HEADS, FOX_SLOT, tq)
    qn = jnp.sqrt(jnp.max(jnp.sum(qr * qr, axis=1), axis=1, keepdims=True))
    blk = lax.broadcasted_iota(jnp.int32, cblk_ref.shape[1:], 1)
    at_i = blk == i
    c_before = jnp.sum(jnp.where(at_i, cblk_ref[0], 0.0), axis=1, keepdims=True)
    k_norm = jnp.sum(jnp.where(at_i, kpm_ref[0], 0.0), axis=1, keepdims=True)
    qk_bound = NORM_SLACK * qn * k_norm

    def count_blocks(thr):
        needed = (blk < i) & jnp.logical_not(cend_ref[0] > thr)
        return jnp.sum(jnp.where(needed, 1.0, 0.0), axis=1, keepdims=True).astype(jnp.int32)

    need_ref[0, 0] = count_blocks(c_before + EXP2_ZERO_GAP + 2.0 * qk_bound)
    need_fixed_ref[0, 0] = count_blocks(c_before + EXP2_ZERO_GAP)
    shift_ref[0, 0] = qk_bound


def _fox_q(x, wqt, wg, crelt, spreadt, cblk_t, cend_t, kpm_t):
    b, s, d = x.shape
    tm = FOX_BLOCK
    nkv = cblk_t.shape[2]
    per_batch = pl.BlockSpec((1, FOX_HEADS, nkv), lambda bi, i: (bi, 0, 0))
    return pl.pallas_call(
        _fox_q_kernel,
        grid=(b, s // tm),
        in_specs=[pl.BlockSpec((1, tm, d), lambda bi, i: (bi, i, 0)), _resident(wqt.shape),
                  _resident(wg.shape),
                  pl.BlockSpec((1, 1, crelt.shape[2], tm), lambda bi, i: (bi, i, 0, 0)),
                  _resident(spreadt.shape), per_batch, per_batch, per_batch],
        out_specs=[pl.BlockSpec((1, FOX_HEADS, FOX_SLOT, tm), lambda bi, i: (bi, 0, 0, i)),
                   pl.BlockSpec((1, tm, wg.shape[1]), lambda bi, i: (bi, i, 0)),
                   pl.BlockSpec((1, 1, FOX_HEADS, 1), lambda bi, i: (bi, i, 0, 0)),
                   pl.BlockSpec((1, 1, FOX_HEADS, 1), lambda bi, i: (bi, i, 0, 0)),
                   pl.BlockSpec((1, 1, FOX_HEADS, 1), lambda bi, i: (bi, i, 0, 0))],
        out_shape=[jax.ShapeDtypeStruct((b, FOX_HEADS, FOX_SLOT, s), BF16),
                   jax.ShapeDtypeStruct((b, s, wg.shape[1]), F32),
                   jax.ShapeDtypeStruct((b, s // tm, FOX_HEADS, 1), jnp.int32),
                   jax.ShapeDtypeStruct((b, s // tm, FOX_HEADS, 1), jnp.int32),
                   jax.ShapeDtypeStruct((b, s // tm, FOX_HEADS, 1), F32)],
        compiler_params=_params("parallel", "parallel"),
        name="fox_q",
    )(x, wqt, wg, crelt, spreadt, cblk_t, cend_t, kpm_t)


def _fox_attn_kernel(cblk_ref, need_ref, need_fixed_ref, shift_ref, qt_ref, k_ref, vt_ref, g_ref, o_ref,
                     acc_ref, *bufs, nkv):
    bi = pl.program_id(0)
    hp = pl.program_id(1)
    i = pl.program_id(2)
    tq = qt_ref.shape[3]
    tk = k_ref.shape[3]
    nhp = FOX_HEADS_PER_STEP
    assert nhp == 2
    dh = FOX_HEAD_DIM
    unroll = len(bufs)

    def head_base(hh):
        return (bi * FOX_HEADS + hp * nhp + hh) * nkv

    def scores(hh, j):
        return _dot(k_ref[0, j, hh], qt_ref[0, hh])

    key_pos = lax.broadcasted_iota(jnp.int32, (tk, tq), 0)
    qry_pos = lax.broadcasted_iota(jnp.int32, (tk, tq), 1)
    bound = [shift_ref[head_base(hh) + i] for hh in range(nhp)]
    fixed_shift_ok = jnp.maximum(bound[0], bound[1]) <= MAX_FIXED_SHIFT

    @pl.when(fixed_shift_ok)
    def _():
        n0 = need_fixed_ref[head_base(0) + i]
        total = n0 + need_fixed_ref[head_base(1) + i]
        lead = unroll - nhp

        def item(e):
            hh = (e >= n0).astype(jnp.int32)
            j = jnp.maximum(i - 1 - (e - hh * n0), 0)
            base = head_base(hh)
            shift = cblk_ref[base + i] - cblk_ref[base + j] - shift_ref[base + i]
            return hh, j, jnp.where(e < total, shift, -jnp.inf)

        def probabilities(e, p_ref):
            hh, j, shift = item(e)
            p_ref[...] = jnp.exp2(scores(hh, j) + shift).astype(BF16)

        def accumulate(e, p_ref):
            hh, j, _ = item(e)
            acc_ref[hh] = acc_ref[hh] + _dot(vt_ref[0, j, hh], p_ref[...])

        def run_group(first, size):
            for u in range(size):
                probabilities(first + u, bufs[u])
            for u in range(size):
                accumulate(first + u, bufs[u])

        for hh in range(nhp):
            p = jnp.where(key_pos <= qry_pos, jnp.exp2(scores(hh, i) - bound[hh]), 0.0)
            bufs[hh][...] = p.astype(BF16)
        for u in range(lead):
            probabilities(u, bufs[nhp + u])
        for hh in range(nhp):
            acc_ref[hh] = _dot(vt_ref[0, i, hh], bufs[hh][...])
        for u in range(lead):
            accumulate(u, bufs[nhp + u])

        def full_group(t, carry):
            run_group(lead + unroll * t, unroll)
            return carry

        left = jnp.maximum(total - lead, 0)
        n_full = left // unroll
        lax.fori_loop(0, n_full, full_group, 0)
        rest = left - n_full * unroll

        @pl.when(rest > unroll // 2)
        def _():
            run_group(lead + n_full * unroll, unroll)

        @pl.when((rest > 0) & (rest <= unroll // 2))
        def _():
            run_group(lead + n_full * unroll, unroll // 2)

    @pl.when(jnp.logical_not(fixed_shift_ok))
    def _():
        for hh in range(nhp):
            base = head_base(hh)
            s = jnp.where(key_pos <= qry_pos, scores(hh, i), -jnp.inf)
            m = jnp.max(s, axis=0, keepdims=True)
            acc_ref[hh] = _dot(vt_ref[0, i, hh], jnp.exp2(s - m).astype(BF16))

            def step(jj, m, hh=hh, base=base):
                j = i - 1 - jj
                s = scores(hh, j) + (cblk_ref[base + i] - cblk_ref[base + j])
                m_new = jnp.maximum(m, jnp.max(s, axis=0, keepdims=True))
                acc_ref[hh] = (jnp.exp2(m - m_new) * acc_ref[hh]
                               + _dot(vt_ref[0, j, hh], jnp.exp2(s - m_new).astype(BF16)))
                return m_new

            lax.fori_loop(0, need_ref[base + i], step, m)

    o = jnp.concatenate([acc_ref[hh, :dh, :] / acc_ref[hh, dh:dh + 1, :] for hh in range(nhp)], axis=0)
    o_ref[0] = (jnp.transpose(o) * _sigmoid(g_ref[0])).astype(BF16)


def _fox_attn(cblk, need, need_fixed, shift, qt, k, vt, g):
    b, nkv, nh, tk, slot = k.shape
    v_rows = vt.shape[3]
    s = nkv * tk
    tq = FOX_BLOCK
    nhp = FOX_HEADS_PER_STEP
    out_w = nhp * FOX_HEAD_DIM
    smem = pl.BlockSpec(memory_space=pltpu.SMEM)
    return pl.pallas_call(
        functools.partial(_fox_attn_kernel, nkv=nkv),
        grid=(b, nh // nhp, s // tq),
        in_specs=[smem, smem, smem, smem,
                  pl.BlockSpec((1, nhp, slot, tq), lambda bi, hp, i: (bi, hp, 0, i)),
                  pl.BlockSpec((1, nkv, nhp, tk, slot), lambda bi, hp, i: (bi, 0, hp, 0, 0)),
                  pl.BlockSpec((1, nkv, nhp, v_rows, tk), lambda bi, hp, i: (bi, 0, hp, 0, 0)),
                  pl.BlockSpec((1, tq, out_w), lambda bi, hp, i: (bi, i, hp))],
        out_specs=pl.BlockSpec((1, tq, out_w), lambda bi, hp, i: (bi, i, hp)),
        out_shape=jax.ShapeDtypeStruct((b, s, nh * FOX_HEAD_DIM), BF16),
        scratch_shapes=[pltpu.VMEM((nhp, v_rows, tq), F32)] + [pltpu.VMEM((tk, tq), BF16)] * FOX_ATTN_UNROLL,
        compiler_params=_params("arbitrary", "arbitrary", "arbitrary"),
        name="fox_attn",
    )(cblk, need, need_fixed, shift, qt, k, vt, g)


def _block_tri(n, block):
    r = jnp.arange(n)
    return ((r[:, None] >= r[None, :]) & (r[:, None] // block == r[None, :] // block)).astype(BF16)


def _spread_heads(w):
    d = w.shape[0]
    w = w.reshape(d, FOX_HEADS, FOX_HEAD_DIM)
    w = jnp.pad(w, ((0, 0), (0, 0), (0, FOX_SLOT - FOX_HEAD_DIM)))
    return w.reshape(d, FOX_HEADS * FOX_SLOT)


def _head_group_matrix():
    return (jnp.arange(FOX_HEADS * FOX_SLOT)[:, None] // FOX_SLOT == jnp.arange(FOX_HEADS)[None, :]).astype(BF16)


def _bias_spread_matrix(first_lane):
    rows = jnp.arange(3 * FOX_HEADS)
    term, head = rows // FOX_HEADS, rows % FOX_HEADS
    cols = head * FOX_SLOT + first_lane + term
    return (jnp.arange(FOX_HEADS * FOX_SLOT)[None, :] == cols[:, None]).astype(BF16)


def kernel(x, p, ffn1_w_in, ffn1_w_out, ln1_g, ln1_b, gla_w_in, gla_w_a2, gla_b_a, gla_gn_g, gla_gn_b, gla_w_o, fox_w_kvf, fox_b_f, fox_w_in, fox_w_o, ln2_g, ln2_b, ffn2_w_in, ffn2_w_out, ple_w_gate, ple_w_proj, ln3_g, ln3_b):
    bsz, seq, d = x.shape
    depth = ffn1_w_in.shape[0]
    n_gla = gla_w_in.shape[0]
    alpha = (2 * depth) ** 0.25
    t = bsz * seq
    fox_w = FOX_HEADS * FOX_HEAD_DIM
    row = lambda a: a.reshape(1, -1)

    tri_chunk = _block_tri(TRI_BLOCK, GLA_CHUNK)
    tri_block = _block_tri(FOX_BLOCK, FOX_BLOCK)
    key_bias_spread = _bias_spread_matrix(FOX_HEAD_DIM)
    query_bias_spread_t = _bias_spread_matrix(FOX_HEAD_DIM + 3).T

    xf = x.reshape(t, d)
    pf = p.reshape(depth, t, p.shape[-1])
    heads_major = lambda a: jnp.transpose(a[:, :, 0, :], (0, 2, 1))
    fox_k = fox_vt = fox_crelt = cblk_t = cend_t = kpm_t = None
    for i in range(depth):
        if i == n_gla:
            wk_aug = _spread_heads(fox_w_kvf[:, :fox_w]).astype(BF16)
            wvt = fox_w_kvf[:, fox_w:2 * fox_w].T.astype(BF16)
            wf = fox_w_kvf[:, 2 * fox_w:].astype(BF16)
            fox_k, fox_vt, fox_crelt, cblk, cend, kpm = _fox_kv(
                xf.reshape(bsz, seq, d), wk_aug, wvt, wf, row(fox_b_f), tri_block, key_bias_spread,
                _head_group_matrix(), jnp.eye(3 * FOX_HEADS, dtype=BF16))
            cblk_t, cend_t, kpm_t = heads_major(cblk), heads_major(cend), heads_major(kpm)
        xf = _ffn(xf, ffn1_w_in[i].astype(BF16), ffn1_w_out[i].astype(BF16), row(ln1_g[i]), row(ln1_b[i]), alpha)
        if i < n_gla:
            qd, ki, ks, v, dec, r = _gla_proj(xf, gla_w_in[i].astype(BF16), gla_w_a2[i].astype(BF16),
                                                 row(gla_b_a[i]), tri_chunk)
            b3 = lambda a: a.reshape(bsz, seq, -1)
            mix = _gla_rec(b3(qd), b3(ki), b3(ks), b3(v), dec.reshape(bsz, seq // GLA_CHUNK, 1, -1), b3(r),
                           row(gla_gn_g[i]), row(gla_gn_b[i])).reshape(t, -1)
            w_o = gla_w_o[i]
        else:
            j = i - n_gla
            wqt = fox_w_in[j][:, :fox_w].T.astype(BF16)
            qt, g, need, need_fixed, shift = _fox_q(xf.reshape(bsz, seq, d), wqt, fox_w_in[j][:, fox_w:].astype(BF16),
                                        fox_crelt, query_bias_spread_t, cblk_t, cend_t, kpm_t)
            flat = lambda a: jnp.transpose(a[:, :, :, 0], (0, 2, 1)).reshape(-1)
            mix = _fox_attn(cblk_t.reshape(-1), flat(need), flat(need_fixed), flat(shift), qt, fox_k, fox_vt,
                            g).reshape(t, -1)
            w_o = fox_w_o[j]
        xf = _ffn(xf, ffn2_w_in[i].astype(BF16), ffn2_w_out[i].astype(BF16), row(ln3_g[i]), row(ln3_b[i]), alpha,
                  mixer=(mix, w_o.astype(BF16), row(ln2_g[i]), row(ln2_b[i])),
                  ple=(pf, i, ple_w_gate[i].astype(BF16), ple_w_proj[i].astype(BF16)))
    return xf.reshape(bsz, seq, d)
```

```python
import functools
import math

import jax
import jax.numpy as jnp
from jax import lax
from jax.experimental import pallas as pl
from jax.experimental.pallas import tpu as pltpu

F32 = jnp.float32
BF16 = jnp.bfloat16

GLA_HEADS = 4
GLA_TAU = 16.0
GLA_CHUNK = 64
FOX_HEADS = 16
FOX_HEAD_DIM = 64
LN_EPS = 1e-5

V7X_VMEM_LIMIT_BYTES = 56 * 1024 * 1024

TOKEN_TILE = 512
FFN_TOKEN_TILE = 512
FOX_BLOCK = 512
FOX_SLOT = 128
FOX_HEADS_PER_STEP = 2
TRI_BLOCK = 256
LOG2E = math.log2(math.e)
EXP2_ZERO_GAP = 136.0
NORM_SLACK = 1.01
FOX_V_ROWS = 80
FOX_ATTN_UNROLL = 4
MAX_FIXED_SHIFT = 50.0


def _params(*sems):
    return pltpu.CompilerParams(dimension_semantics=sems, vmem_limit_bytes=V7X_VMEM_LIMIT_BYTES)


def _resident(shape):
    zeros = (0,) * len(shape)
    return pl.BlockSpec(shape, lambda *_: zeros, pipeline_mode=pl.Buffered(1))


def _dot(a, b):
    return jnp.dot(a, b, preferred_element_type=F32)


def _dot_nt(a, b):
    return lax.dot_general(a, b, (((1,), (1,)), ((), ())), preferred_element_type=F32)


def _split3(a):
    hi = a.astype(BF16)
    r1 = a - hi.astype(F32)
    mid = r1.astype(BF16)
    lo = (r1 - mid.astype(F32)).astype(BF16)
    return hi, mid, lo


def _tri_cumsum(tri, a):
    hi, mid, lo = _split3(a)
    return _dot(tri, hi) + _dot(tri, mid) + _dot(tri, lo)


def _log_sigmoid(z):
    return jnp.minimum(z, 0.0) - jnp.log1p(jnp.exp(-jnp.abs(z)))


def _sigmoid(z):
    return 1.0 / (1.0 + jnp.exp(-z))


def _layer_norm(z, g, b):
    mu = jnp.mean(z, axis=-1, keepdims=True)
    zc = z - mu
    var = jnp.mean(zc * zc, axis=-1, keepdims=True)
    return zc * lax.rsqrt(var + LN_EPS) * g + b


def _ffn_kernel(*refs, alpha, d_ff, after_mixer):
    if after_mixer:
        (x_ref, mix_ref, wo_ref, g2_ref, b2_ref, win_ref, wout_ref, g_ref, b_ref,
         p_ref, wgate_ref, wproj_ref, o_ref) = refs
    else:
        x_ref, win_ref, wout_ref, g_ref, b_ref, o_ref = refs
    tm = x_ref.shape[0]
    halves = (slice(0, tm // 2), slice(tm // 2, tm))
    if after_mixer:
        xs = [_layer_norm(alpha * x_ref[rows, :] + _dot(mix_ref[rows, :], wo_ref[...]), g2_ref[...], b2_ref[...])
              for rows in halves]
    else:
        xs = [x_ref[rows, :] for rows in halves]
    for rows, x in zip(halves, xs):
        xb = x.astype(BF16)
        gate = _dot(xb, win_ref[:, :d_ff])
        up = _dot(xb, win_ref[:, d_ff:])
        h = (gate * _sigmoid(gate) * up).astype(BF16)
        z = alpha * x + 0.5 * _dot(h, wout_ref[...])
        if after_mixer:
            ple_gate = _sigmoid(_dot(xb, wgate_ref[...]))
            z = z + ple_gate * _dot(p_ref[0, rows, :].astype(BF16), wproj_ref[...])
        o_ref[rows, :] = _layer_norm(z, g_ref[...], b_ref[...])


def _ffn(x, w_in, w_out, ln_g, ln_b, alpha, mixer=None, ple=None):
    assert (mixer is None) == (ple is None)
    t, d = x.shape
    d_ff = w_out.shape[0]
    tm = FFN_TOKEN_TILE
    row = lambda i: (i, 0)
    in_specs = [pl.BlockSpec((tm, d), row)]
    args = [x]
    if mixer is not None:
        mix, w_o, ln2_g, ln2_b = mixer
        in_specs += [pl.BlockSpec((tm, mix.shape[1]), row), _resident(w_o.shape), _resident(ln2_g.shape),
                     _resident(ln2_b.shape)]
        args += [mix, w_o, ln2_g, ln2_b]
    in_specs += [_resident(w_in.shape), _resident(w_out.shape), _resident(ln_g.shape), _resident(ln_b.shape)]
    args += [w_in, w_out, ln_g, ln_b]
    if ple is not None:
        p, layer, w_gate, w_proj = ple
        in_specs += [pl.BlockSpec((1, tm, p.shape[2]), lambda i: (layer, i, 0)), _resident(w_gate.shape),
                     _resident(w_proj.shape)]
        args += [p, w_gate, w_proj]
    return pl.pallas_call(
        functools.partial(_ffn_kernel, alpha=alpha, d_ff=d_ff, after_mixer=mixer is not None),
        grid=(t // tm,),
        in_specs=in_specs,
        out_specs=pl.BlockSpec((tm, d), row),
        out_shape=jax.ShapeDtypeStruct((t, d), F32),
        compiler_params=_params("parallel"),
        name="ffn_after_mixer" if mixer is not None else "ffn",
    )(*args)


def _gla_proj_kernel(x_ref, win_ref, wa2_ref, ba_ref, tri_ref,
                     qd_ref, ki_ref, ks_ref, v_ref, dec_ref, r_ref, *, kd, vd):
    tm = x_ref.shape[0]
    xb = x_ref[...].astype(BF16)
    a_lr = _dot(xb, win_ref[:, 2 * kd + 2 * vd:])
    log_a = _log_sigmoid(_dot(a_lr.astype(BF16), wa2_ref[...]) + ba_ref[...]) * (1.0 / GLA_TAU)
    qk = _dot(xb, win_ref[:, :2 * kd])
    q = qk[:, :kd] * ((kd // GLA_HEADS) ** -0.5)
    k = qk[:, kd:]
    r = _dot(xb, win_ref[:, 2 * kd + vd:2 * kd + 2 * vd])
    v = _dot(xb, win_ref[:, 2 * kd:2 * kd + vd])
    tri = tri_ref[...]
    bcum = jnp.concatenate(
        [_tri_cumsum(tri, log_a[s:s + TRI_BLOCK]) for s in range(0, tm, TRI_BLOCK)], axis=0)
    nc = tm // GLA_CHUNK
    b3 = bcum.reshape(nc, GLA_CHUNK, kd)
    b_last = b3[:, GLA_CHUNK - 1:GLA_CHUNK, :]
    qd_ref[...] = (q * jnp.exp(bcum)).astype(BF16)
    ki_ref[...] = (k * jnp.exp(-bcum)).astype(BF16)
    ks_ref[...] = (k.reshape(nc, GLA_CHUNK, kd) * jnp.exp(b_last - b3)).reshape(tm, kd).astype(BF16)
    dec_ref[...] = jnp.exp(b_last)
    v_ref[...] = v.astype(BF16)
    r_ref[...] = r


def _gla_proj(x, w_in, w_a2, b_a, tri):
    t, d = x.shape
    kd = w_a2.shape[1]
    vd = (w_in.shape[1] - 2 * kd - w_a2.shape[0]) // 2
    tm = TOKEN_TILE
    row = lambda i: (i, 0)
    return pl.pallas_call(
        functools.partial(_gla_proj_kernel, kd=kd, vd=vd),
        grid=(t // tm,),
        in_specs=[pl.BlockSpec((tm, d), row), _resident(w_in.shape), _resident(w_a2.shape),
                  _resident(b_a.shape), _resident(tri.shape)],
        out_specs=[pl.BlockSpec((tm, kd), row), pl.BlockSpec((tm, kd), row), pl.BlockSpec((tm, kd), row),
                   pl.BlockSpec((tm, vd), row), pl.BlockSpec((tm // GLA_CHUNK, 1, kd), lambda i: (i, 0, 0)),
                   pl.BlockSpec((tm, vd), row)],
        out_shape=[jax.ShapeDtypeStruct((t, kd), BF16)] * 3 + [
            jax.ShapeDtypeStruct((t, vd), BF16),
            jax.ShapeDtypeStruct((t // GLA_CHUNK, 1, kd), F32),
            jax.ShapeDtypeStruct((t, vd), F32)],
        compiler_params=_params("parallel"),
        name="gla_proj",
    )(x, w_in, w_a2, b_a, tri)


def _gla_rec_kernel(qd_ref, ki_ref, ks_ref, v_ref, dec_ref, r_ref, gng_ref, gnb_ref, o_ref, st_ref):
    tm = qd_ref.shape[1]
    dk = qd_ref.shape[2] // GLA_HEADS
    dv = v_ref.shape[2] // GLA_HEADS

    @pl.when(pl.program_id(1) == 0)
    def _():
        st_ref[...] = jnp.zeros_like(st_ref)

    rows = lax.broadcasted_iota(jnp.int32, (GLA_CHUNK, GLA_CHUNK), 0)
    cols = lax.broadcasted_iota(jnp.int32, (GLA_CHUNK, GLA_CHUNK), 1)
    causal = cols <= rows

    def chunk(c, carry):
        r0 = pl.multiple_of(c * GLA_CHUNK, GLA_CHUNK)
        rs = pl.ds(r0, GLA_CHUNK)
        for h in range(GLA_HEADS):
            ksl = slice(h * dk, (h + 1) * dk)
            vsl = slice(h * dv, (h + 1) * dv)
            qd = qd_ref[0, rs, ksl]
            ki = ki_ref[0, rs, ksl]
            ks = ks_ref[0, rs, ksl]
            vv = v_ref[0, rs, vsl]
            attn = jnp.where(causal, _dot_nt(qd, ki), 0.0).astype(BF16)
            st = st_ref[h]
            o = _dot(attn, vv) + _dot_nt(qd, st.astype(BF16))
            vt = jnp.transpose(vv.astype(F32)).astype(BF16)
            st_ref[h] = st * dec_ref[0, c, :, ksl] + _dot(vt, ks)
            on = _layer_norm(o, gng_ref[:, vsl], gnb_ref[:, vsl])
            r = r_ref[0, rs, vsl]
            o_ref[0, rs, vsl] = (on * (r * _sigmoid(r))).astype(BF16)
        return carry

    lax.fori_loop(0, tm // GLA_CHUNK, chunk, 0, unroll=True)


def _gla_rec(qd, ki, ks, v, dec, r, gn_g, gn_b):
    b, s, kd = qd.shape
    vd = v.shape[2]
    tm = TOKEN_TILE
    blk = lambda bi, i: (bi, i, 0)
    return pl.pallas_call(
        _gla_rec_kernel,
        grid=(b, s // tm),
        in_specs=[pl.BlockSpec((1, tm, kd), blk), pl.BlockSpec((1, tm, kd), blk), pl.BlockSpec((1, tm, kd), blk),
                  pl.BlockSpec((1, tm, vd), blk),
                  pl.BlockSpec((1, tm // GLA_CHUNK, 1, kd), lambda bi, i: (bi, i, 0, 0)),
                  pl.BlockSpec((1, tm, vd), blk), _resident(gn_g.shape), _resident(gn_b.shape)],
        out_specs=pl.BlockSpec((1, tm, vd), blk),
        out_shape=jax.ShapeDtypeStruct((b, s, vd), BF16),
        scratch_shapes=[pltpu.VMEM((GLA_HEADS, vd // GLA_HEADS, kd // GLA_HEADS), F32)],
        compiler_params=_params("arbitrary", "arbitrary"),
        name="gla_rec",
    )(qd, ki, ks, v, dec, r, gn_g, gn_b)


def _fox_kv_kernel(x_ref, wk_ref, wvt_ref, wf_ref, bf_ref, tri_ref, spread_ref, group_ref, eye_ref,
                   k_ref, vt_ref, crelt_ref, cblk_ref, cend_ref, kpm_ref, carry_ref, kmax_ref):
    @pl.when(pl.program_id(1) == 0)
    def _():
        carry_ref[...] = jnp.zeros_like(carry_ref)
        kmax_ref[...] = jnp.zeros_like(kmax_ref)

    tk = x_ref.shape[1]
    xb = x_ref[0].astype(BF16)
    log_f = _log_sigmoid(_dot(xb, wf_ref[...]) + bf_ref[...])
    c_rel = _tri_cumsum(tri_ref[...], log_f) * LOG2E
    terms = jnp.concatenate(_split3(-c_rel), axis=1)
    bias = _dot(terms, spread_ref[...])
    crelt_ref[0, 0] = -_dot_nt(eye_ref[...], terms)
    k = _dot(xb, wk_ref[...])
    slot_lane = lax.broadcasted_iota(jnp.int32, k.shape, 1) % FOX_SLOT
    is_one_lane = (slot_lane >= FOX_HEAD_DIM + 3) & (slot_lane < FOX_HEAD_DIM + 6)
    k_aug = jnp.where(is_one_lane, 1.0, k + bias).astype(BF16)
    for h in range(FOX_HEADS):
        k_ref[0, 0, h] = k_aug[:, h * FOX_SLOT:(h + 1) * FOX_SLOT]
    vt = _dot_nt(wvt_ref[...], xb).astype(BF16).reshape(FOX_HEADS, FOX_HEAD_DIM, tk)
    pad_row = lax.broadcasted_iota(jnp.int32, (FOX_HEADS, FOX_V_ROWS - FOX_HEAD_DIM, tk), 1)
    vt_ref[0, 0] = jnp.concatenate([vt, jnp.where(pad_row == 0, 1.0, 0.0).astype(BF16)], axis=1)
    cblk_ref[0, 0] = carry_ref[...]
    carry_ref[...] = carry_ref[...] + c_rel[tk - 1:tk, :]
    cend_ref[0, 0] = carry_ref[...]
    kr = k.astype(BF16).astype(F32)
    norm2 = _dot((kr * kr).astype(BF16), group_ref[...]) * NORM_SLACK
    kmax_ref[...] = jnp.maximum(kmax_ref[...], jnp.sqrt(jnp.max(norm2, axis=0, keepdims=True)))
    kpm_ref[0, 0] = kmax_ref[...]


def _fox_kv(x, wk_aug, wvt, wf, b_f, tri, spread, group, eye):
    b, s, d = x.shape
    tk = FOX_BLOCK
    nkv = s // tk
    nh = wf.shape[1]
    blk4 = lambda bi, j: (bi, j, 0, 0)
    blk5 = lambda bi, j: (bi, j, 0, 0, 0)
    small = jax.ShapeDtypeStruct((b, nkv, 1, nh), F32)
    return pl.pallas_call(
        _fox_kv_kernel,
        grid=(b, nkv),
        in_specs=[pl.BlockSpec((1, tk, d), lambda bi, j: (bi, j, 0)), _resident(wk_aug.shape),
                  _resident(wvt.shape), _resident(wf.shape), _resident(b_f.shape), _resident(tri.shape),
                  _resident(spread.shape), _resident(group.shape), _resident(eye.shape)],
        out_specs=[pl.BlockSpec((1, 1, nh, tk, FOX_SLOT), blk5),
                   pl.BlockSpec((1, 1, nh, FOX_V_ROWS, tk), blk5),
                   pl.BlockSpec((1, 1, 3 * nh, tk), blk4),
                   pl.BlockSpec((1, 1, 1, nh), blk4), pl.BlockSpec((1, 1, 1, nh), blk4),
                   pl.BlockSpec((1, 1, 1, nh), blk4)],
        out_shape=[jax.ShapeDtypeStruct((b, nkv, nh, tk, FOX_SLOT), BF16),
                   jax.ShapeDtypeStruct((b, nkv, nh, FOX_V_ROWS, tk), BF16),
                   jax.ShapeDtypeStruct((b, nkv, 3 * nh, tk), F32),
                   small, small, small],
        scratch_shapes=[pltpu.VMEM((1, nh), F32), pltpu.VMEM((1, nh), F32)],
        compiler_params=_params("arbitrary", "arbitrary"),
        name="fox_kv",
    )(x, wk_aug, wvt, wf, b_f, tri, spread, group, eye)


def _fox_q_kernel(x_ref, wqt_ref, wg_ref, crelt_ref, spreadt_ref, cblk_ref, cend_ref, kpm_ref,
                  qt_ref, g_ref, need_ref, need_fixed_ref, shift_ref):
    i = pl.program_id(1)
    tq = x_ref.shape[1]
    xb = x_ref[0].astype(BF16)
    q = _dot_nt(wqt_ref[...], xb) * (FOX_HEAD_DIM ** -0.5 * LOG2E)
    q = q.reshape(FOX_HEADS, FOX_HEAD_DIM, tq)
    qt = jnp.concatenate([q, jnp.zeros((FOX_HEADS, FOX_SLOT - FOX_HEAD_DIM, tq), F32)], axis=1)
    qt = qt.reshape(FOX_HEADS * FOX_SLOT, tq)
    c_rows = _dot(spreadt_ref[...], crelt_ref[0, 0].astype(BF16))
    qt3 = (qt + c_rows).reshape(FOX_HEADS, FOX_SLOT, tq)
    slot_row = lax.broadcasted_iota(jnp.int32, qt3.shape, 1)
    is_one_row = (slot_row >= FOX_HEAD_DIM) & (slot_row < FOX_HEAD_DIM + 3)
    qt_ref[0] = jnp.where(is_one_row, 1.0, qt3).astype(BF16)
    g_ref[0] = _dot(xb, wg_ref[...])

    qr = qt.astype(BF16).astype(F32).reshape(FOX_HEADS, FOX_SLOT, tq)
    qn = jnp.sqrt(jnp.max(jnp.sum(qr * qr, axis=1), axis=1, keepdims=True))
    blk = lax.broadcasted_iota(jnp.int32, cblk_ref.shape[1:], 1)
    at_i = blk == i
    c_before = jnp.sum(jnp.where(at_i, cblk_ref[0], 0.0), axis=1, keepdims=True)
    k_norm = jnp.sum(jnp.where(at_i, kpm_ref[0], 0.0), axis=1, keepdims=True)
    qk_bound = NORM_SLACK * qn * k_norm

    def count_blocks(thr):
        needed = (blk < i) & jnp.logical_not(cend_ref[0] > thr)
        return jnp.sum(jnp.where(needed, 1.0, 0.0), axis=1, keepdims=True).astype(jnp.int32)

    need_ref[0, 0] = count_blocks(c_before + EXP2_ZERO_GAP + 2.0 * qk_bound)
    need_fixed_ref[0, 0] = count_blocks(c_before + EXP2_ZERO_GAP)
    shift_ref[0, 0] = qk_bound


def _fox_q(x, wqt, wg, crelt, spreadt, cblk_t, cend_t, kpm_t):
    b, s, d = x.shape
    tm = FOX_BLOCK
    nkv = cblk_t.shape[2]
    per_batch = pl.BlockSpec((1, FOX_HEADS, nkv), lambda bi, i: (bi, 0, 0))
    return pl.pallas_call(
        _fox_q_kernel,
        grid=(b, s // tm),
        in_specs=[pl.BlockSpec((1, tm, d), lambda bi, i: (bi, i, 0)), _resident(wqt.shape),
                  _resident(wg.shape),
                  pl.BlockSpec((1, 1, crelt.shape[2], tm), lambda bi, i: (bi, i, 0, 0)),
                  _resident(spreadt.shape), per_batch, per_batch, per_batch],
        out_specs=[pl.BlockSpec((1, FOX_HEADS, FOX_SLOT, tm), lambda bi, i: (bi, 0, 0, i)),
                   pl.BlockSpec((1, tm, wg.shape[1]), lambda bi, i: (bi, i, 0)),
                   pl.BlockSpec((1, 1, FOX_HEADS, 1), lambda bi, i: (bi, i, 0, 0)),
                   pl.BlockSpec((1, 1, FOX_HEADS, 1), lambda bi, i: (bi, i, 0, 0)),
                   pl.BlockSpec((1, 1, FOX_HEADS, 1), lambda bi, i: (bi, i, 0, 0))],
        out_shape=[jax.ShapeDtypeStruct((b, FOX_HEADS, FOX_SLOT, s), BF16),
                   jax.ShapeDtypeStruct((b, s, wg.shape[1]), F32),
                   jax.ShapeDtypeStruct((b, s // tm, FOX_HEADS, 1), jnp.int32),
                   jax.ShapeDtypeStruct((b, s // tm, FOX_HEADS, 1), jnp.int32),
                   jax.ShapeDtypeStruct((b, s // tm, FOX_HEADS, 1), F32)],
        compiler_params=_params("parallel", "parallel"),
        name="fox_q",
    )(x, wqt, wg, crelt, spreadt, cblk_t, cend_t, kpm_t)


def _fox_attn_kernel(cblk_ref, need_ref, need_fixed_ref, shift_ref, qt_ref, k_ref, vt_ref, g_ref, o_ref,
                     acc_ref, *bufs, nkv):
    bi = pl.program_id(0)
    hp = pl.program_id(1)
    i = pl.program_id(2)
    tq = qt_ref.shape[3]
    tk = k_ref.shape[3]
    nhp = FOX_HEADS_PER_STEP
    assert nhp == 2
    dh = FOX_HEAD_DIM
    unroll = len(bufs)

    def head_base(hh):
        return (bi * FOX_HEADS + hp * nhp + hh) * nkv

    def scores(hh, j):
        return _dot(k_ref[0, j, hh], qt_ref[0, hh])

    key_pos = lax.broadcasted_iota(jnp.int32, (tk, tq), 0)
    qry_pos = lax.broadcasted_iota(jnp.int32, (tk, tq), 1)
    bound = [shift_ref[head_base(hh) + i] for hh in range(nhp)]
    fixed_shift_ok = jnp.maximum(bound[0], bound[1]) <= MAX_FIXED_SHIFT

    @pl.when(fixed_shift_ok)
    def _():
        n0 = need_fixed_ref[head_base(0) + i]
        total = n0 + need_fixed_ref[head_base(1) + i]
        lead = unroll - nhp

        def item(e):
            hh = (e >= n0).astype(jnp.int32)
            j = jnp.maximum(i - 1 - (e - hh * n0), 0)
            base = head_base(hh)
            shift = cblk_ref[base + i] - cblk_ref[base + j] - shift_ref[base + i]
            return hh, j, jnp.where(e < total, shift, -jnp.inf)

        def probabilities(e, p_ref):
            hh, j, shift = item(e)
            p_ref[...] = jnp.exp2(scores(hh, j) + shift).astype(BF16)

        def accumulate(e, p_ref):
            hh, j, _ = item(e)
            acc_ref[hh] = acc_ref[hh] + _dot(vt_ref[0, j, hh], p_ref[...])

        def run_group(first, size):
            for u in range(size):
                probabilities(first + u, bufs[u])
            for u in range(size):
                accumulate(first + u, bufs[u])

        for hh in range(nhp):
            p = jnp.where(key_pos <= qry_pos, jnp.exp2(scores(hh, i) - bound[hh]), 0.0)
            bufs[hh][...] = p.astype(BF16)
        for u in range(lead):
            probabilities(u, bufs[nhp + u])
        for hh in range(nhp):
            acc_ref[hh] = _dot(vt_ref[0, i, hh], bufs[hh][...])
        for u in range(lead):
            accumulate(u, bufs[nhp + u])

        def full_group(t, carry):
            run_group(lead + unroll * t, unroll)
            return carry

        left = jnp.maximum(total - lead, 0)
        n_full = left // unroll
        lax.fori_loop(0, n_full, full_group, 0)
        rest = left - n_full * unroll

        @pl.when(rest > unroll // 2)
        def _():
            run_group(lead + n_full * unroll, unroll)

        @pl.when((rest > 0) & (rest <= unroll // 2))
        def _():
            run_group(lead + n_full * unroll, unroll // 2)

    @pl.when(jnp.logical_not(fixed_shift_ok))
    def _():
        for hh in range(nhp):
            base = head_base(hh)
            s = jnp.where(key_pos <= qry_pos, scores(hh, i), -jnp.inf)
            m = jnp.max(s, axis=0, keepdims=True)
            acc_ref[hh] = _dot(vt_ref[0, i, hh], jnp.exp2(s - m).astype(BF16))

            def step(jj, m, hh=hh, base=base):
                j = i - 1 - jj
                s = scores(hh, j) + (cblk_ref[base + i] - cblk_ref[base + j])
                m_new = jnp.maximum(m, jnp.max(s, axis=0, keepdims=True))
                acc_ref[hh] = (jnp.exp2(m - m_new) * acc_ref[hh]
                               + _dot(vt_ref[0, j, hh], jnp.exp2(s - m_new).astype(BF16)))
                return m_new

            lax.fori_loop(0, need_ref[base + i], step, m)

    o = jnp.concatenate([acc_ref[hh, :dh, :] / acc_ref[hh, dh:dh + 1, :] for hh in range(nhp)], axis=0)
    o_ref[0] = (jnp.transpose(o) * _sigmoid(g_ref[0])).astype(BF16)


def _fox_attn(cblk, need, need_fixed, shift, qt, k, vt, g):
    b, nkv, nh, tk, slot = k.shape
    v_rows = vt.shape[3]
    s = nkv * tk
    tq = FOX_BLOCK
    nhp = FOX_HEADS_PER_STEP
    out_w = nhp * FOX_HEAD_DIM
    smem = pl.BlockSpec(memory_space=pltpu.SMEM)
    return pl.pallas_call(
        functools.partial(_fox_attn_kernel, nkv=nkv),
        grid=(b, nh // nhp, s // tq),
        in_specs=[smem, smem, smem, smem,
                  pl.BlockSpec((1, nhp, slot, tq), lambda bi, hp, i: (bi, hp, 0, i)),
                  pl.BlockSpec((1, nkv, nhp, tk, slot), lambda bi, hp, i: (bi, 0, hp, 0, 0)),
                  pl.BlockSpec((1, nkv, nhp, v_rows, tk), lambda bi, hp, i: (bi, 0, hp, 0, 0)),
                  pl.BlockSpec((1, tq, out_w), lambda bi, hp, i: (bi, i, hp))],
        out_specs=pl.BlockSpec((1, tq, out_w), lambda bi, hp, i: (bi, i, hp)),
        out_shape=jax.ShapeDtypeStruct((b, s, nh * FOX_HEAD_DIM), BF16),
        scratch_shapes=[pltpu.VMEM((nhp, v_rows, tq), F32)] + [pltpu.VMEM((tk, tq), BF16)] * FOX_ATTN_UNROLL,
        compiler_params=_params("arbitrary", "arbitrary", "arbitrary"),
        name="fox_attn",
    )(cblk, need, need_fixed, shift, qt, k, vt, g)


def _block_tri(n, block):
    r = jnp.arange(n)
    return ((r[:, None] >= r[None, :]) & (r[:, None] // block == r[None, :] // block)).astype(BF16)


def _spread_heads(w):
    d = w.shape[0]
    w = w.reshape(d, FOX_HEADS, FOX_HEAD_DIM)
    w = jnp.pad(w, ((0, 0), (0, 0), (0, FOX_SLOT - FOX_HEAD_DIM)))
    return w.reshape(d, FOX_HEADS * FOX_SLOT)


def _head_group_matrix():
    return (jnp.arange(FOX_HEADS * FOX_SLOT)[:, None] // FOX_SLOT == jnp.arange(FOX_HEADS)[None, :]).astype(BF16)


def _bias_spread_matrix(first_lane):
    rows = jnp.arange(3 * FOX_HEADS)
    term, head = rows // FOX_HEADS, rows % FOX_HEADS
    cols = head * FOX_SLOT + first_lane + term
    return (jnp.arange(FOX_HEADS * FOX_SLOT)[None, :] == cols[:, None]).astype(BF16)


def kernel(x, p, ffn1_w_in, ffn1_w_out, ln1_g, ln1_b, gla_w_in, gla_w_a2, gla_b_a, gla_gn_g, gla_gn_b, gla_w_o, fox_w_kvf, fox_b_f, fox_w_in, fox_w_o, ln2_g, ln2_b, ffn2_w_in, ffn2_w_out, ple_w_gate, ple_w_proj, ln3_g, ln3_b):
    bsz, seq, d = x.shape
    depth = ffn1_w_in.shape[0]
    n_gla = gla_w_in.shape[0]
    alpha = (2 * depth) ** 0.25
    t = bsz * seq
    fox_w = FOX_HEADS * FOX_HEAD_DIM
    row = lambda a: a.reshape(1, -1)

    tri_chunk = _block_tri(TRI_BLOCK, GLA_CHUNK)
    tri_block = _block_tri(FOX_BLOCK, FOX_BLOCK)
    key_bias_spread = _bias_spread_matrix(FOX_HEAD_DIM)
    query_bias_spread_t = _bias_spread_matrix(FOX_HEAD_DIM + 3).T

    xf = x.reshape(t, d)
    pf = p.reshape(depth, t, p.shape[-1])
    heads_major = lambda a: jnp.transpose(a[:, :, 0, :], (0, 2, 1))
    fox_k = fox_vt = fox_crelt = cblk_t = cend_t = kpm_t = None
    for i in range(depth):
        if i == n_gla:
            wk_aug = _spread_heads(fox_w_kvf[:, :fox_w]).astype(BF16)
            wvt = fox_w_kvf[:, fox_w:2 * fox_w].T.astype(BF16)
            wf = fox_w_kvf[:, 2 * fox_w:].astype(BF16)
            fox_k, fox_vt, fox_crelt, cblk, cend, kpm = _fox_kv(
                xf.reshape(bsz, seq, d), wk_aug, wvt, wf, row(fox_b_f), tri_block, key_bias_spread,
                _head_group_matrix(), jnp.eye(3 * FOX_HEADS, dtype=BF16))
            cblk_t, cend_t, kpm_t = heads_major(cblk), heads_major(cend), heads_major(kpm)
        xf = _ffn(xf, ffn1_w_in[i].astype(BF16), ffn1_w_out[i].astype(BF16), row(ln1_g[i]), row(ln1_b[i]), alpha)
        if i < n_gla:
            qd, ki, ks, v, dec, r = _gla_proj(xf, gla_w_in[i].astype(BF16), gla_w_a2[i].astype(BF16),
                                                 row(gla_b_a[i]), tri_chunk)
            b3 = lambda a: a.reshape(bsz, seq, -1)
            mix = _gla_rec(b3(qd), b3(ki), b3(ks), b3(v), dec.reshape(bsz, seq // GLA_CHUNK, 1, -1), b3(r),
                           row(gla_gn_g[i]), row(gla_gn_b[i])).reshape(t, -1)
            w_o = gla_w_o[i]
        else:
            j = i - n_gla
            wqt = fox_w_in[j][:, :fox_w].T.astype(BF16)
            qt, g, need, need_fixed, shift = _fox_q(xf.reshape(bsz, seq, d), wqt, fox_w_in[j][:, fox_w:].astype(BF16),
                                        fox_crelt, query_bias_spread_t, cblk_t, cend_t, kpm_t)
            flat = lambda a: jnp.transpose(a[:, :, :, 0], (0, 2, 1)).reshape(-1)
            mix = _fox_attn(cblk_t.reshape(-1), flat(need), flat(need_fixed), flat(shift), qt, fox_k, fox_vt,
                            g).reshape(t, -1)
            w_o = fox_w_o[j]
        xf = _ffn(xf, ffn2_w_in[i].astype(BF16), ffn2_w_out[i].astype(BF16), row(ln3_g[i]), row(ln3_b[i]), alpha,
                  mixer=(mix, w_o.astype(BF16), row(ln2_g[i]), row(ln2_b[i])),
                  ple=(pf, i, ple_w_gate[i].astype(BF16), ple_w_proj[i].astype(BF16)))
    return xf.reshape(bsz, seq, d)
```

```python
import functools
import math

import jax
import jax.numpy as jnp
from jax import lax
from jax.experimental import pallas as pl
from jax.experimental.pallas import tpu as pltpu

F32 = jnp.float32
BF16 = jnp.bfloat16

GLA_HEADS = 4
GLA_TAU = 16.0
GLA_CHUNK = 64
FOX_HEADS = 16
FOX_HEAD_DIM = 64
LN_EPS = 1e-5

V7X_VMEM_LIMIT_BYTES = 56 * 1024 * 1024

TOKEN_TILE = 512
FFN_TOKEN_TILE = 1024
FFN_AFTER_MIXER_TOKEN_TILE = 512
FFN_ROW_PIECE = 256
FOX_BLOCK = 512
FOX_SLOT = 128
FOX_HEADS_PER_STEP = 2
TRI_BLOCK = 256
LOG2E = math.log2(math.e)
EXP2_ZERO_GAP = 136.0
NORM_SLACK = 1.01
FOX_V_ROWS = 80
FOX_ATTN_UNROLL = 4
MAX_FIXED_SHIFT = 50.0


def _params(*sems):
    return pltpu.CompilerParams(dimension_semantics=sems, vmem_limit_bytes=V7X_VMEM_LIMIT_BYTES)


def _resident(shape):
    zeros = (0,) * len(shape)
    return pl.BlockSpec(shape, lambda *_: zeros, pipeline_mode=pl.Buffered(1))


def _dot(a, b):
    return jnp.dot(a, b, preferred_element_type=F32)


def _dot_nt(a, b):
    return lax.dot_general(a, b, (((1,), (1,)), ((), ())), preferred_element_type=F32)


def _split3(a):
    hi = a.astype(BF16)
    r1 = a - hi.astype(F32)
    mid = r1.astype(BF16)
    lo = (r1 - mid.astype(F32)).astype(BF16)
    return hi, mid, lo


def _tri_cumsum(tri, a):
    hi, mid, lo = _split3(a)
    return _dot(tri, hi) + _dot(tri, mid) + _dot(tri, lo)


def _log_sigmoid(z):
    return jnp.minimum(z, 0.0) - jnp.log1p(jnp.exp(-jnp.abs(z)))


def _sigmoid(z):
    return 1.0 / (1.0 + jnp.exp(-z))


def _layer_norm(z, g, b):
    mu = jnp.mean(z, axis=-1, keepdims=True)
    zc = z - mu
    var = jnp.mean(zc * zc, axis=-1, keepdims=True)
    return zc * lax.rsqrt(var + LN_EPS) * g + b


def _ffn_kernel(*refs, alpha, d_ff, after_mixer):
    if after_mixer:
        (x_ref, mix_ref, wo_ref, g2_ref, b2_ref, win_ref, wout_ref, g_ref, b_ref,
         p_ref, wgate_ref, wproj_ref, o_ref) = refs
    else:
        x_ref, win_ref, wout_ref, g_ref, b_ref, o_ref = refs
    tm = x_ref.shape[0]
    halves = [slice(r, r + FFN_ROW_PIECE) for r in range(0, tm, FFN_ROW_PIECE)]
    if after_mixer:
        xs = [_layer_norm(alpha * x_ref[rows, :] + _dot(mix_ref[rows, :], wo_ref[...]), g2_ref[...], b2_ref[...])
              for rows in halves]
    else:
        xs = [x_ref[rows, :] for rows in halves]
    for rows, x in zip(halves, xs):
        xb = x.astype(BF16)
        gate = _dot(xb, win_ref[:, :d_ff])
        up = _dot(xb, win_ref[:, d_ff:])
        h = (gate * _sigmoid(gate) * up).astype(BF16)
        z = alpha * x + 0.5 * _dot(h, wout_ref[...])
        if after_mixer:
            ple_gate = _sigmoid(_dot(xb, wgate_ref[...]))
            z = z + ple_gate * _dot(p_ref[0, rows, :].astype(BF16), wproj_ref[...])
        o_ref[rows, :] = _layer_norm(z, g_ref[...], b_ref[...])


def _ffn(x, w_in, w_out, ln_g, ln_b, alpha, mixer=None, ple=None):
    assert (mixer is None) == (ple is None)
    t, d = x.shape
    d_ff = w_out.shape[0]
    tm = FFN_TOKEN_TILE if mixer is None else FFN_AFTER_MIXER_TOKEN_TILE
    row = lambda i: (i, 0)
    in_specs = [pl.BlockSpec((tm, d), row)]
    args = [x]
    if mixer is not None:
        mix, w_o, ln2_g, ln2_b = mixer
        in_specs += [pl.BlockSpec((tm, mix.shape[1]), row), _resident(w_o.shape), _resident(ln2_g.shape),
                     _resident(ln2_b.shape)]
        args += [mix, w_o, ln2_g, ln2_b]
    in_specs += [_resident(w_in.shape), _resident(w_out.shape), _resident(ln_g.shape), _resident(ln_b.shape)]
    args += [w_in, w_out, ln_g, ln_b]
    if ple is not None:
        p, layer, w_gate, w_proj = ple
        in_specs += [pl.BlockSpec((1, tm, p.shape[2]), lambda i: (layer, i, 0)), _resident(w_gate.shape),
                     _resident(w_proj.shape)]
        args += [p, w_gate, w_proj]
    return pl.pallas_call(
        functools.partial(_ffn_kernel, alpha=alpha, d_ff=d_ff, after_mixer=mixer is not None),
        grid=(t // tm,),
        in_specs=in_specs,
        out_specs=pl.BlockSpec((tm, d), row),
        out_shape=jax.ShapeDtypeStruct((t, d), F32),
        compiler_params=_params("parallel"),
        name="ffn_after_mixer" if mixer is not None else "ffn",
    )(*args)


def _gla_proj_kernel(x_ref, win_ref, wa2_ref, ba_ref, tri_ref,
                     qd_ref, ki_ref, ks_ref, v_ref, dec_ref, r_ref, *, kd, vd):
    tm = x_ref.shape[0]
    xb = x_ref[...].astype(BF16)
    a_lr = _dot(xb, win_ref[:, 2 * kd + 2 * vd:])
    log_a = _log_sigmoid(_dot(a_lr.astype(BF16), wa2_ref[...]) + ba_ref[...]) * (1.0 / GLA_TAU)
    qk = _dot(xb, win_ref[:, :2 * kd])
    q = qk[:, :kd] * ((kd // GLA_HEADS) ** -0.5)
    k = qk[:, kd:]
    r = _dot(xb, win_ref[:, 2 * kd + vd:2 * kd + 2 * vd])
    v = _dot(xb, win_ref[:, 2 * kd:2 * kd + vd])
    tri = tri_ref[...]
    bcum = jnp.concatenate(
        [_tri_cumsum(tri, log_a[s:s + TRI_BLOCK]) for s in range(0, tm, TRI_BLOCK)], axis=0)
    nc = tm // GLA_CHUNK
    b3 = bcum.reshape(nc, GLA_CHUNK, kd)
    b_last = b3[:, GLA_CHUNK - 1:GLA_CHUNK, :]
    qd_ref[...] = (q * jnp.exp(bcum)).astype(BF16)
    ki_ref[...] = (k * jnp.exp(-bcum)).astype(BF16)
    ks_ref[...] = (k.reshape(nc, GLA_CHUNK, kd) * jnp.exp(b_last - b3)).reshape(tm, kd).astype(BF16)
    dec_ref[...] = jnp.exp(b_last)
    v_ref[...] = v.astype(BF16)
    r_ref[...] = r


def _gla_proj(x, w_in, w_a2, b_a, tri):
    t, d = x.shape
    kd = w_a2.shape[1]
    vd = (w_in.shape[1] - 2 * kd - w_a2.shape[0]) // 2
    tm = TOKEN_TILE
    row = lambda i: (i, 0)
    return pl.pallas_call(
        functools.partial(_gla_proj_kernel, kd=kd, vd=vd),
        grid=(t // tm,),
        in_specs=[pl.BlockSpec((tm, d), row), _resident(w_in.shape), _resident(w_a2.shape),
                  _resident(b_a.shape), _resident(tri.shape)],
        out_specs=[pl.BlockSpec((tm, kd), row), pl.BlockSpec((tm, kd), row), pl.BlockSpec((tm, kd), row),
                   pl.BlockSpec((tm, vd), row), pl.BlockSpec((tm // GLA_CHUNK, 1, kd), lambda i: (i, 0, 0)),
                   pl.BlockSpec((tm, vd), row)],
        out_shape=[jax.ShapeDtypeStruct((t, kd), BF16)] * 3 + [
            jax.ShapeDtypeStruct((t, vd), BF16),
            jax.ShapeDtypeStruct((t // GLA_CHUNK, 1, kd), F32),
            jax.ShapeDtypeStruct((t, vd), F32)],
        compiler_params=_params("parallel"),
        name="gla_proj",
    )(x, w_in, w_a2, b_a, tri)


def _gla_rec_kernel(qd_ref, ki_ref, ks_ref, v_ref, dec_ref, r_ref, gng_ref, gnb_ref, o_ref, st_ref):
    tm = qd_ref.shape[1]
    dk = qd_ref.shape[2] // GLA_HEADS
    dv = v_ref.shape[2] // GLA_HEADS

    @pl.when(pl.program_id(1) == 0)
    def _():
        st_ref[...] = jnp.zeros_like(st_ref)

    rows = lax.broadcasted_iota(jnp.int32, (GLA_CHUNK, GLA_CHUNK), 0)
    cols = lax.broadcasted_iota(jnp.int32, (GLA_CHUNK, GLA_CHUNK), 1)
    causal = cols <= rows

    def chunk(c, carry):
        r0 = pl.multiple_of(c * GLA_CHUNK, GLA_CHUNK)
        rs = pl.ds(r0, GLA_CHUNK)
        for h in range(GLA_HEADS):
            ksl = slice(h * dk, (h + 1) * dk)
            vsl = slice(h * dv, (h + 1) * dv)
            qd = qd_ref[0, rs, ksl]
            ki = ki_ref[0, rs, ksl]
            ks = ks_ref[0, rs, ksl]
            vv = v_ref[0, rs, vsl]
            attn = jnp.where(causal, _dot_nt(qd, ki), 0.0).astype(BF16)
            st = st_ref[h]
            o = _dot(attn, vv) + _dot_nt(qd, st.astype(BF16))
            vt = jnp.transpose(vv.astype(F32)).astype(BF16)
            st_ref[h] = st * dec_ref[0, c, :, ksl] + _dot(vt, ks)
            on = _layer_norm(o, gng_ref[:, vsl], gnb_ref[:, vsl])
            r = r_ref[0, rs, vsl]
            o_ref[0, rs, vsl] = (on * (r * _sigmoid(r))).astype(BF16)
        return carry

    lax.fori_loop(0, tm // GLA_CHUNK, chunk, 0, unroll=True)


def _gla_rec(qd, ki, ks, v, dec, r, gn_g, gn_b):
    b, s, kd = qd.shape
    vd = v.shape[2]
    tm = TOKEN_TILE
    blk = lambda bi, i: (bi, i, 0)
    return pl.pallas_call(
        _gla_rec_kernel,
        grid=(b, s // tm),
        in_specs=[pl.BlockSpec((1, tm, kd), blk), pl.BlockSpec((1, tm, kd), blk), pl.BlockSpec((1, tm, kd), blk),
                  pl.BlockSpec((1, tm, vd), blk),
                  pl.BlockSpec((1, tm // GLA_CHUNK, 1, kd), lambda bi, i: (bi, i, 0, 0)),
                  pl.BlockSpec((1, tm, vd), blk), _resident(gn_g.shape), _resident(gn_b.shape)],
        out_specs=pl.BlockSpec((1, tm, vd), blk),
        out_shape=jax.ShapeDtypeStruct((b, s, vd), BF16),
        scratch_shapes=[pltpu.VMEM((GLA_HEADS, vd // GLA_HEADS, kd // GLA_HEADS), F32)],
        compiler_params=_params("arbitrary", "arbitrary"),
        name="gla_rec",
    )(qd, ki, ks, v, dec, r, gn_g, gn_b)


def _fox_kv_kernel(x_ref, wk_ref, wvt_ref, wf_ref, bf_ref, tri_ref, spread_ref, group_ref, eye_ref,
                   k_ref, vt_ref, crelt_ref, cblk_ref, cend_ref, kpm_ref, carry_ref, kmax_ref):
    @pl.when(pl.program_id(1) == 0)
    def _():
        carry_ref[...] = jnp.zeros_like(carry_ref)
        kmax_ref[...] = jnp.zeros_like(kmax_ref)

    tk = x_ref.shape[1]
    xb = x_ref[0].astype(BF16)
    log_f = _log_sigmoid(_dot(xb, wf_ref[...]) + bf_ref[...])
    c_rel = _tri_cumsum(tri_ref[...], log_f) * LOG2E
    terms = jnp.concatenate(_split3(-c_rel), axis=1)
    bias = _dot(terms, spread_ref[...])
    crelt_ref[0, 0] = -_dot_nt(eye_ref[...], terms)
    k = _dot(xb, wk_ref[...])
    slot_lane = lax.broadcasted_iota(jnp.int32, k.shape, 1) % FOX_SLOT
    is_one_lane = (slot_lane >= FOX_HEAD_DIM + 3) & (slot_lane < FOX_HEAD_DIM + 6)
    k_aug = jnp.where(is_one_lane, 1.0, k + bias).astype(BF16)
    for h in range(FOX_HEADS):
        k_ref[0, 0, h] = k_aug[:, h * FOX_SLOT:(h + 1) * FOX_SLOT]
    vt = _dot_nt(wvt_ref[...], xb).astype(BF16).reshape(FOX_HEADS, FOX_HEAD_DIM, tk)
    pad_row = lax.broadcasted_iota(jnp.int32, (FOX_HEADS, FOX_V_ROWS - FOX_HEAD_DIM, tk), 1)
    vt_ref[0, 0] = jnp.concatenate([vt, jnp.where(pad_row == 0, 1.0, 0.0).astype(BF16)], axis=1)
    cblk_ref[0, 0] = carry_ref[...]
    carry_ref[...] = carry_ref[...] + c_rel[tk - 1:tk, :]
    cend_ref[0, 0] = carry_ref[...]
    kr = k.astype(BF16).astype(F32)
    norm2 = _dot((kr * kr).astype(BF16), group_ref[...]) * NORM_SLACK
    kmax_ref[...] = jnp.maximum(kmax_ref[...], jnp.sqrt(jnp.max(norm2, axis=0, keepdims=True)))
    kpm_ref[0, 0] = kmax_ref[...]


def _fox_kv(x, wk_aug, wvt, wf, b_f, tri, spread, group, eye):
    b, s, d = x.shape
    tk = FOX_BLOCK
    nkv = s // tk
    nh = wf.shape[1]
    blk4 = lambda bi, j: (bi, j, 0, 0)
    blk5 = lambda bi, j: (bi, j, 0, 0, 0)
    small = jax.ShapeDtypeStruct((b, nkv, 1, nh), F32)
    return pl.pallas_call(
        _fox_kv_kernel,
        grid=(b, nkv),
        in_specs=[pl.BlockSpec((1, tk, d), lambda bi, j: (bi, j, 0)), _resident(wk_aug.shape),
                  _resident(wvt.shape), _resident(wf.shape), _resident(b_f.shape), _resident(tri.shape),
                  _resident(spread.shape), _resident(group.shape), _resident(eye.shape)],
        out_specs=[pl.BlockSpec((1, 1, nh, tk, FOX_SLOT), blk5),
                   pl.BlockSpec((1, 1, nh, FOX_V_ROWS, tk), blk5),
                   pl.BlockSpec((1, 1, 3 * nh, tk), blk4),
                   pl.BlockSpec((1, 1, 1, nh), blk4), pl.BlockSpec((1, 1, 1, nh), blk4),
                   pl.BlockSpec((1, 1, 1, nh), blk4)],
        out_shape=[jax.ShapeDtypeStruct((b, nkv, nh, tk, FOX_SLOT), BF16),
                   jax.ShapeDtypeStruct((b, nkv, nh, FOX_V_ROWS, tk), BF16),
                   jax.ShapeDtypeStruct((b, nkv, 3 * nh, tk), F32),
                   small, small, small],
        scratch_shapes=[pltpu.VMEM((1, nh), F32), pltpu.VMEM((1, nh), F32)],
        compiler_params=_params("arbitrary", "arbitrary"),
        name="fox_kv",
    )(x, wk_aug, wvt, wf, b_f, tri, spread, group, eye)


def _fox_q_kernel(x_ref, wqt_ref, wg_ref, crelt_ref, spreadt_ref, cblk_ref, cend_ref, kpm_ref,
                  qt_ref, g_ref, need_ref, need_fixed_ref, shift_ref):
    i = pl.program_id(1)
    tq = x_ref.shape[1]
    xb = x_ref[0].astype(BF16)
    q = _dot_nt(wqt_ref[...], xb) * (FOX_HEAD_DIM ** -0.5 * LOG2E)
    q = q.reshape(FOX_HEADS, FOX_HEAD_DIM, tq)
    qt = jnp.concatenate([q, jnp.zeros((FOX_HEADS, FOX_SLOT - FOX_HEAD_DIM, tq), F32)], axis=1)
    qt = qt.reshape(FOX_HEADS * FOX_SLOT, tq)
    c_rows = _dot(spreadt_ref[...], crelt_ref[0, 0].astype(BF16))
    qt3 = (qt + c_rows).reshape(FOX_HEADS, FOX_SLOT, tq)
    slot_row = lax.broadcasted_iota(jnp.int32, qt3.shape, 1)
    is_one_row = (slot_row >= FOX_HEAD_DIM) & (slot_row < FOX_HEAD_DIM + 3)
    qt_ref[0] = jnp.where(is_one_row, 1.0, qt3).astype(BF16)
    g_ref[0] = _dot(xb, wg_ref[...])

    qr = qt.astype(BF16).astype(F32).reshape(FOX_HEADS, FOX_SLOT, tq)
    qn = jnp.sqrt(jnp.max(jnp.sum(qr * qr, axis=1), axis=1, keepdims=True))
    blk = lax.broadcasted_iota(jnp.int32, cblk_ref.shape[1:], 1)
    at_i = blk == i
    c_before = jnp.sum(jnp.where(at_i, cblk_ref[0], 0.0), axis=1, keepdims=True)
    k_norm = jnp.sum(jnp.where(at_i, kpm_ref[0], 0.0), axis=1, keepdims=True)
    qk_bound = NORM_SLACK * qn * k_norm

    def count_blocks(thr):
        needed = (blk < i) & jnp.logical_not(cend_ref[0] > thr)
        return jnp.sum(jnp.where(needed, 1.0, 0.0), axis=1, keepdims=True).astype(jnp.int32)

    need_ref[0, 0] = count_blocks(c_before + EXP2_ZERO_GAP + 2.0 * qk_bound)
    need_fixed_ref[0, 0] = count_blocks(c_before + EXP2_ZERO_GAP)
    shift_ref[0, 0] = qk_bound


def _fox_q(x, wqt, wg, crelt, spreadt, cblk_t, cend_t, kpm_t):
    b, s, d = x.shape
    tm = FOX_BLOCK
    nkv = cblk_t.shape[2]
    per_batch = pl.BlockSpec((1, FOX_HEADS, nkv), lambda bi, i: (bi, 0, 0))
    return pl.pallas_call(
        _fox_q_kernel,
        grid=(b, s // tm),
        in_specs=[pl.BlockSpec((1, tm, d), lambda bi, i: (bi, i, 0)), _resident(wqt.shape),
                  _resident(wg.shape),
                  pl.BlockSpec((1, 1, crelt.shape[2], tm), lambda bi, i: (bi, i, 0, 0)),
                  _resident(spreadt.shape), per_batch, per_batch, per_batch],
        out_specs=[pl.BlockSpec((1, FOX_HEADS, FOX_SLOT, tm), lambda bi, i: (bi, 0, 0, i)),
                   pl.BlockSpec((1, tm, wg.shape[1]), lambda bi, i: (bi, i, 0)),
                   pl.BlockSpec((1, 1, FOX_HEADS, 1), lambda bi, i: (bi, i, 0, 0)),
                   pl.BlockSpec((1, 1, FOX_HEADS, 1), lambda bi, i: (bi, i, 0, 0)),
                   pl.BlockSpec((1, 1, FOX_HEADS, 1), lambda bi, i: (bi, i, 0, 0))],
        out_shape=[jax.ShapeDtypeStruct((b, FOX_HEADS, FOX_SLOT, s), BF16),
                   jax.ShapeDtypeStruct((b, s, wg.shape[1]), F32),
                   jax.ShapeDtypeStruct((b, s // tm, FOX_HEADS, 1), jnp.int32),
                   jax.ShapeDtypeStruct((b, s // tm, FOX_HEADS, 1), jnp.int32),
                   jax.ShapeDtypeStruct((b, s // tm, FOX_HEADS, 1), F32)],
        compiler_params=_params("parallel", "parallel"),
        name="fox_q",
    )(x, wqt, wg, crelt, spreadt, cblk_t, cend_t, kpm_t)


def _fox_attn_kernel(cblk_ref, need_ref, need_fixed_ref, shift_ref, qt_ref, k_ref, vt_ref, g_ref, o_ref,
                     acc_ref, *bufs, nkv):
    bi = pl.program_id(0)
    hp = pl.program_id(1)
    i = pl.program_id(2)
    tq = qt_ref.shape[3]
    tk = k_ref.shape[3]
    nhp = FOX_HEADS_PER_STEP
    assert nhp == 2
    dh = FOX_HEAD_DIM
    unroll = len(bufs)

    def head_base(hh):
        return (bi * FOX_HEADS + hp * nhp + hh) * nkv

    def scores(hh, j):
        return _dot(k_ref[0, j, hh], qt_ref[0, hh])

    key_pos = lax.broadcasted_iota(jnp.int32, (tk, tq), 0)
    qry_pos = lax.broadcasted_iota(jnp.int32, (tk, tq), 1)
    bound = [shift_ref[head_base(hh) + i] for hh in range(nhp)]
    fixed_shift_ok = jnp.maximum(bound[0], bound[1]) <= MAX_FIXED_SHIFT

    @pl.when(fixed_shift_ok)
    def _():
        n0 = need_fixed_ref[head_base(0) + i]
        total = n0 + need_fixed_ref[head_base(1) + i]
        lead = unroll - nhp

        def item(e):
            hh = (e >= n0).astype(jnp.int32)
            j = jnp.maximum(i - 1 - (e - hh * n0), 0)
            base = head_base(hh)
            shift = cblk_ref[base + i] - cblk_ref[base + j] - shift_ref[base + i]
            return hh, j, jnp.where(e < total, shift, -jnp.inf)

        def probabilities(e, p_ref):
            hh, j, shift = item(e)
            p_ref[...] = jnp.exp2(scores(hh, j) + shift).astype(BF16)

        def accumulate(e, p_ref):
            hh, j, _ = item(e)
            acc_ref[hh] = acc_ref[hh] + _dot(vt_ref[0, j, hh], p_ref[...])

        def run_group(first, size):
            for u in range(size):
                probabilities(first + u, bufs[u])
            for u in range(size):
                accumulate(first + u, bufs[u])

        for hh in range(nhp):
            p = jnp.where(key_pos <= qry_pos, jnp.exp2(scores(hh, i) - bound[hh]), 0.0)
            bufs[hh][...] = p.astype(BF16)
        for u in range(lead):
            probabilities(u, bufs[nhp + u])
        for hh in range(nhp):
            acc_ref[hh] = _dot(vt_ref[0, i, hh], bufs[hh][...])
        for u in range(lead):
            accumulate(u, bufs[nhp + u])

        def full_group(t, carry):
            run_group(lead + unroll * t, unroll)
            return carry

        left = jnp.maximum(total - lead, 0)
        n_full = left // unroll
        lax.fori_loop(0, n_full, full_group, 0)
        rest = left - n_full * unroll

        @pl.when(rest > unroll // 2)
        def _():
            run_group(lead + n_full * unroll, unroll)

        @pl.when((rest > 0) & (rest <= unroll // 2))
        def _():
            run_group(lead + n_full * unroll, unroll // 2)

    @pl.when(jnp.logical_not(fixed_shift_ok))
    def _():
        for hh in range(nhp):
            base = head_base(hh)
            s = jnp.where(key_pos <= qry_pos, scores(hh, i), -jnp.inf)
            m = jnp.max(s, axis=0, keepdims=True)
            acc_ref[hh] = _dot(vt_ref[0, i, hh], jnp.exp2(s - m).astype(BF16))

            def step(jj, m, hh=hh, base=base):
                j = i - 1 - jj
                s = scores(hh, j) + (cblk_ref[base + i] - cblk_ref[base + j])
                m_new = jnp.maximum(m, jnp.max(s, axis=0, keepdims=True))
                acc_ref[hh] = (jnp.exp2(m - m_new) * acc_ref[hh]
                               + _dot(vt_ref[0, j, hh], jnp.exp2(s - m_new).astype(BF16)))
                return m_new

            lax.fori_loop(0, need_ref[base + i], step, m)

    o = jnp.concatenate([acc_ref[hh, :dh, :] / acc_ref[hh, dh:dh + 1, :] for hh in range(nhp)], axis=0)
    o_ref[0] = (jnp.transpose(o) * _sigmoid(g_ref[0])).astype(BF16)


def _fox_attn(cblk, need, need_fixed, shift, qt, k, vt, g):
    b, nkv, nh, tk, slot = k.shape
    v_rows = vt.shape[3]
    s = nkv * tk
    tq = FOX_BLOCK
    nhp = FOX_HEADS_PER_STEP
    out_w = nhp * FOX_HEAD_DIM
    smem = pl.BlockSpec(memory_space=pltpu.SMEM)
    return pl.pallas_call(
        functools.partial(_fox_attn_kernel, nkv=nkv),
        grid=(b, nh // nhp, s // tq),
        in_specs=[smem, smem, smem, smem,
                  pl.BlockSpec((1, nhp, slot, tq), lambda bi, hp, i: (bi, hp, 0, i)),
                  pl.BlockSpec((1, nkv, nhp, tk, slot), lambda bi, hp, i: (bi, 0, hp, 0, 0)),
                  pl.BlockSpec((1, nkv, nhp, v_rows, tk), lambda bi, hp, i: (bi, 0, hp, 0, 0)),
                  pl.BlockSpec((1, tq, out_w), lambda bi, hp, i: (bi, i, hp))],
        out_specs=pl.BlockSpec((1, tq, out_w), lambda bi, hp, i: (bi, i, hp)),
        out_shape=jax.ShapeDtypeStruct((b, s, nh * FOX_HEAD_DIM), BF16),
        scratch_shapes=[pltpu.VMEM((nhp, v_rows, tq), F32)] + [pltpu.VMEM((tk, tq), BF16)] * FOX_ATTN_UNROLL,
        compiler_params=_params("arbitrary", "arbitrary", "arbitrary"),
        name="fox_attn",
    )(cblk, need, need_fixed, shift, qt, k, vt, g)


def _block_tri(n, block):
    r = jnp.arange(n)
    return ((r[:, None] >= r[None, :]) & (r[:, None] // block == r[None, :] // block)).astype(BF16)


def _spread_heads(w):
    d = w.shape[0]
    w = w.reshape(d, FOX_HEADS, FOX_HEAD_DIM)
    w = jnp.pad(w, ((0, 0), (0, 0), (0, FOX_SLOT - FOX_HEAD_DIM)))
    return w.reshape(d, FOX_HEADS * FOX_SLOT)


def _head_group_matrix():
    return (jnp.arange(FOX_HEADS * FOX_SLOT)[:, None] // FOX_SLOT == jnp.arange(FOX_HEADS)[None, :]).astype(BF16)


def _bias_spread_matrix(first_lane):
    rows = jnp.arange(3 * FOX_HEADS)
    term, head = rows // FOX_HEADS, rows % FOX_HEADS
    cols = head * FOX_SLOT + first_lane + term
    return (jnp.arange(FOX_HEADS * FOX_SLOT)[None, :] == cols[:, None]).astype(BF16)


def kernel(x, p, ffn1_w_in, ffn1_w_out, ln1_g, ln1_b, gla_w_in, gla_w_a2, gla_b_a, gla_gn_g, gla_gn_b, gla_w_o, fox_w_kvf, fox_b_f, fox_w_in, fox_w_o, ln2_g, ln2_b, ffn2_w_in, ffn2_w_out, ple_w_gate, ple_w_proj, ln3_g, ln3_b):
    bsz, seq, d = x.shape
    depth = ffn1_w_in.shape[0]
    n_gla = gla_w_in.shape[0]
    alpha = (2 * depth) ** 0.25
    t = bsz * seq
    fox_w = FOX_HEADS * FOX_HEAD_DIM
    row = lambda a: a.reshape(1, -1)

    tri_chunk = _block_tri(TRI_BLOCK, GLA_CHUNK)
    tri_block = _block_tri(FOX_BLOCK, FOX_BLOCK)
    key_bias_spread = _bias_spread_matrix(FOX_HEAD_DIM)
    query_bias_spread_t = _bias_spread_matrix(FOX_HEAD_DIM + 3).T

    xf = x.reshape(t, d)
    pf = p.reshape(depth, t, p.shape[-1])
    heads_major = lambda a: jnp.transpose(a[:, :, 0, :], (0, 2, 1))
    fox_k = fox_vt = fox_crelt = cblk_t = cend_t = kpm_t = None
    for i in range(depth):
        if i == n_gla:
            wk_aug = _spread_heads(fox_w_kvf[:, :fox_w]).astype(BF16)
            wvt = fox_w_kvf[:, fox_w:2 * fox_w].T.astype(BF16)
            wf = fox_w_kvf[:, 2 * fox_w:].astype(BF16)
            fox_k, fox_vt, fox_crelt, cblk, cend, kpm = _fox_kv(
                xf.reshape(bsz, seq, d), wk_aug, wvt, wf, row(fox_b_f), tri_block, key_bias_spread,
                _head_group_matrix(), jnp.eye(3 * FOX_HEADS, dtype=BF16))
            cblk_t, cend_t, kpm_t = heads_major(cblk), heads_major(cend), heads_major(kpm)
        xf = _ffn(xf, ffn1_w_in[i].astype(BF16), ffn1_w_out[i].astype(BF16), row(ln1_g[i]), row(ln1_b[i]), alpha)
        if i < n_gla:
            qd, ki, ks, v, dec, r = _gla_proj(xf, gla_w_in[i].astype(BF16), gla_w_a2[i].astype(BF16),
                                                 row(gla_b_a[i]), tri_chunk)
            b3 = lambda a: a.reshape(bsz, seq, -1)
            mix = _gla_rec(b3(qd), b3(ki), b3(ks), b3(v), dec.reshape(bsz, seq // GLA_CHUNK, 1, -1), b3(r),
                           row(gla_gn_g[i]), row(gla_gn_b[i])).reshape(t, -1)
            w_o = gla_w_o[i]
        else:
            j = i - n_gla
            wqt = fox_w_in[j][:, :fox_w].T.astype(BF16)
            qt, g, need, need_fixed, shift = _fox_q(xf.reshape(bsz, seq, d), wqt, fox_w_in[j][:, fox_w:].astype(BF16),
                                        fox_crelt, query_bias_spread_t, cblk_t, cend_t, kpm_t)
            flat = lambda a: jnp.transpose(a[:, :, :, 0], (0, 2, 1)).reshape(-1)
            mix = _fox_attn(cblk_t.reshape(-1), flat(need), flat(need_fixed), flat(shift), qt, fox_k, fox_vt,
                            g).reshape(t, -1)
            w_o = fox_w_o[j]
        xf = _ffn(xf, ffn2_w_in[i].astype(BF16), ffn2_w_out[i].astype(BF16), row(ln3_g[i]), row(ln3_b[i]), alpha,
                  mixer=(mix, w_o.astype(BF16), row(ln2_g[i]), row(ln2_b[i])),
                  ple=(pf, i, ple_w_gate[i].astype(BF16), ple_w_proj[i].astype(BF16)))
    return xf.reshape(bsz, seq, d)
```

```python
import functools
import math

import jax
import jax.numpy as jnp
from jax import lax
from jax.experimental import pallas as pl
from jax.experimental.pallas import tpu as pltpu

F32 = jnp.float32
BF16 = jnp.bfloat16

GLA_HEADS = 4
GLA_TAU = 16.0
GLA_CHUNK = 64
FOX_HEADS = 16
FOX_HEAD_DIM = 64
LN_EPS = 1e-5

V7X_VMEM_LIMIT_BYTES = 56 * 1024 * 1024

TOKEN_TILE = 512
FFN_TOKEN_TILE = 1024
FFN_AFTER_MIXER_TOKEN_TILE = 512
FFN_ROW_PIECE = 256
FOX_BLOCK = 512
FOX_SLOT = 128
FOX_HEADS_PER_STEP = 2
FOX_QUERY_BLOCKS_PER_STEP = 2
TRI_BLOCK = 256
LOG2E = math.log2(math.e)
EXP2_ZERO_GAP = 136.0
NORM_SLACK = 1.01
FOX_V_ROWS = 80
FOX_ATTN_UNROLL = 4
MAX_FIXED_SHIFT = 50.0


def _params(*sems):
    return pltpu.CompilerParams(dimension_semantics=sems, vmem_limit_bytes=V7X_VMEM_LIMIT_BYTES)


def _resident(shape):
    zeros = (0,) * len(shape)
    return pl.BlockSpec(shape, lambda *_: zeros, pipeline_mode=pl.Buffered(1))


def _dot(a, b):
    return jnp.dot(a, b, preferred_element_type=F32)


def _dot_nt(a, b):
    return lax.dot_general(a, b, (((1,), (1,)), ((), ())), preferred_element_type=F32)


def _split3(a):
    hi = a.astype(BF16)
    r1 = a - hi.astype(F32)
    mid = r1.astype(BF16)
    lo = (r1 - mid.astype(F32)).astype(BF16)
    return hi, mid, lo


def _tri_cumsum(tri, a):
    hi, mid, lo = _split3(a)
    return _dot(tri, hi) + _dot(tri, mid) + _dot(tri, lo)


def _log_sigmoid(z):
    return jnp.minimum(z, 0.0) - jnp.log1p(jnp.exp(-jnp.abs(z)))


def _sigmoid(z):
    return 1.0 / (1.0 + jnp.exp(-z))


def _layer_norm(z, g, b):
    mu = jnp.mean(z, axis=-1, keepdims=True)
    zc = z - mu
    var = jnp.mean(zc * zc, axis=-1, keepdims=True)
    return zc * lax.rsqrt(var + LN_EPS) * g + b


def _ffn_kernel(*refs, alpha, d_ff, after_mixer):
    if after_mixer:
        (x_ref, mix_ref, wo_ref, g2_ref, b2_ref, win_ref, wout_ref, g_ref, b_ref,
         p_ref, wgate_ref, wproj_ref, o_ref) = refs
    else:
        x_ref, win_ref, wout_ref, g_ref, b_ref, o_ref = refs
    tm = x_ref.shape[0]
    halves = [slice(r, r + FFN_ROW_PIECE) for r in range(0, tm, FFN_ROW_PIECE)]
    if after_mixer:
        xs = [_layer_norm(alpha * x_ref[rows, :] + _dot(mix_ref[rows, :], wo_ref[...]), g2_ref[...], b2_ref[...])
              for rows in halves]
    else:
        xs = [x_ref[rows, :] for rows in halves]
    for rows, x in zip(halves, xs):
        xb = x.astype(BF16)
        gate = _dot(xb, win_ref[:, :d_ff])
        up = _dot(xb, win_ref[:, d_ff:])
        h = (gate * _sigmoid(gate) * up).astype(BF16)
        z = alpha * x + 0.5 * _dot(h, wout_ref[...])
        if after_mixer:
            ple_gate = _sigmoid(_dot(xb, wgate_ref[...]))
            z = z + ple_gate * _dot(p_ref[0, rows, :].astype(BF16), wproj_ref[...])
        o_ref[rows, :] = _layer_norm(z, g_ref[...], b_ref[...])


def _ffn(x, w_in, w_out, ln_g, ln_b, alpha, mixer=None, ple=None):
    assert (mixer is None) == (ple is None)
    t, d = x.shape
    d_ff = w_out.shape[0]
    tm = FFN_TOKEN_TILE if mixer is None else FFN_AFTER_MIXER_TOKEN_TILE
    row = lambda i: (i, 0)
    in_specs = [pl.BlockSpec((tm, d), row)]
    args = [x]
    if mixer is not None:
        mix, w_o, ln2_g, ln2_b = mixer
        in_specs += [pl.BlockSpec((tm, mix.shape[1]), row), _resident(w_o.shape), _resident(ln2_g.shape),
                     _resident(ln2_b.shape)]
        args += [mix, w_o, ln2_g, ln2_b]
    in_specs += [_resident(w_in.shape), _resident(w_out.shape), _resident(ln_g.shape), _resident(ln_b.shape)]
    args += [w_in, w_out, ln_g, ln_b]
    if ple is not None:
        p, layer, w_gate, w_proj = ple
        in_specs += [pl.BlockSpec((1, tm, p.shape[2]), lambda i: (layer, i, 0)), _resident(w_gate.shape),
                     _resident(w_proj.shape)]
        args += [p, w_gate, w_proj]
    return pl.pallas_call(
        functools.partial(_ffn_kernel, alpha=alpha, d_ff=d_ff, after_mixer=mixer is not None),
        grid=(t // tm,),
        in_specs=in_specs,
        out_specs=pl.BlockSpec((tm, d), row),
        out_shape=jax.ShapeDtypeStruct((t, d), F32),
        compiler_params=_params("parallel"),
        name="ffn_after_mixer" if mixer is not None else "ffn",
    )(*args)


def _gla_proj_kernel(x_ref, win_ref, wa2_ref, ba_ref, tri_ref,
                     qd_ref, ki_ref, ks_ref, v_ref, dec_ref, r_ref, *, kd, vd):
    tm = x_ref.shape[0]
    xb = x_ref[...].astype(BF16)
    a_lr = _dot(xb, win_ref[:, 2 * kd + 2 * vd:])
    log_a = _log_sigmoid(_dot(a_lr.astype(BF16), wa2_ref[...]) + ba_ref[...]) * (1.0 / GLA_TAU)
    qk = _dot(xb, win_ref[:, :2 * kd])
    q = qk[:, :kd] * ((kd // GLA_HEADS) ** -0.5)
    k = qk[:, kd:]
    r = _dot(xb, win_ref[:, 2 * kd + vd:2 * kd + 2 * vd])
    v = _dot(xb, win_ref[:, 2 * kd:2 * kd + vd])
    tri = tri_ref[...]
    bcum = jnp.concatenate(
        [_tri_cumsum(tri, log_a[s:s + TRI_BLOCK]) for s in range(0, tm, TRI_BLOCK)], axis=0)
    nc = tm // GLA_CHUNK
    b3 = bcum.reshape(nc, GLA_CHUNK, kd)
    b_last = b3[:, GLA_CHUNK - 1:GLA_CHUNK, :]
    qd_ref[...] = (q * jnp.exp(bcum)).astype(BF16)
    ki_ref[...] = (k * jnp.exp(-bcum)).astype(BF16)
    ks_ref[...] = (k.reshape(nc, GLA_CHUNK, kd) * jnp.exp(b_last - b3)).reshape(tm, kd).astype(BF16)
    dec_ref[...] = jnp.exp(b_last)
    v_ref[...] = v.astype(BF16)
    r_ref[...] = r


def _gla_proj(x, w_in, w_a2, b_a, tri):
    t, d = x.shape
    kd = w_a2.shape[1]
    vd = (w_in.shape[1] - 2 * kd - w_a2.shape[0]) // 2
    tm = TOKEN_TILE
    row = lambda i: (i, 0)
    return pl.pallas_call(
        functools.partial(_gla_proj_kernel, kd=kd, vd=vd),
        grid=(t // tm,),
        in_specs=[pl.BlockSpec((tm, d), row), _resident(w_in.shape), _resident(w_a2.shape),
                  _resident(b_a.shape), _resident(tri.shape)],
        out_specs=[pl.BlockSpec((tm, kd), row), pl.BlockSpec((tm, kd), row), pl.BlockSpec((tm, kd), row),
                   pl.BlockSpec((tm, vd), row), pl.BlockSpec((tm // GLA_CHUNK, 1, kd), lambda i: (i, 0, 0)),
                   pl.BlockSpec((tm, vd), row)],
        out_shape=[jax.ShapeDtypeStruct((t, kd), BF16)] * 3 + [
            jax.ShapeDtypeStruct((t, vd), BF16),
            jax.ShapeDtypeStruct((t // GLA_CHUNK, 1, kd), F32),
            jax.ShapeDtypeStruct((t, vd), F32)],
        compiler_params=_params("parallel"),
        name="gla_proj",
    )(x, w_in, w_a2, b_a, tri)


def _gla_rec_kernel(qd_ref, ki_ref, ks_ref, v_ref, dec_ref, r_ref, gng_ref, gnb_ref, o_ref, st_ref):
    tm = qd_ref.shape[1]
    dk = qd_ref.shape[2] // GLA_HEADS
    dv = v_ref.shape[2] // GLA_HEADS

    @pl.when(pl.program_id(1) == 0)
    def _():
        st_ref[...] = jnp.zeros_like(st_ref)

    rows = lax.broadcasted_iota(jnp.int32, (GLA_CHUNK, GLA_CHUNK), 0)
    cols = lax.broadcasted_iota(jnp.int32, (GLA_CHUNK, GLA_CHUNK), 1)
    causal = cols <= rows

    def chunk(c, carry):
        r0 = pl.multiple_of(c * GLA_CHUNK, GLA_CHUNK)
        rs = pl.ds(r0, GLA_CHUNK)
        for h in range(GLA_HEADS):
            ksl = slice(h * dk, (h + 1) * dk)
            vsl = slice(h * dv, (h + 1) * dv)
            qd = qd_ref[0, rs, ksl]
            ki = ki_ref[0, rs, ksl]
            ks = ks_ref[0, rs, ksl]
            vv = v_ref[0, rs, vsl]
            attn = jnp.where(causal, _dot_nt(qd, ki), 0.0).astype(BF16)
            st = st_ref[h]
            o = _dot(attn, vv) + _dot_nt(qd, st.astype(BF16))
            vt = jnp.transpose(vv.astype(F32)).astype(BF16)
            st_ref[h] = st * dec_ref[0, c, :, ksl] + _dot(vt, ks)
            on = _layer_norm(o, gng_ref[:, vsl], gnb_ref[:, vsl])
            r = r_ref[0, rs, vsl]
            o_ref[0, rs, vsl] = (on * (r * _sigmoid(r))).astype(BF16)
        return carry

    lax.fori_loop(0, tm // GLA_CHUNK, chunk, 0, unroll=True)


def _gla_rec(qd, ki, ks, v, dec, r, gn_g, gn_b):
    b, s, kd = qd.shape
    vd = v.shape[2]
    tm = TOKEN_TILE
    blk = lambda bi, i: (bi, i, 0)
    return pl.pallas_call(
        _gla_rec_kernel,
        grid=(b, s // tm),
        in_specs=[pl.BlockSpec((1, tm, kd), blk), pl.BlockSpec((1, tm, kd), blk), pl.BlockSpec((1, tm, kd), blk),
                  pl.BlockSpec((1, tm, vd), blk),
                  pl.BlockSpec((1, tm // GLA_CHUNK, 1, kd), lambda bi, i: (bi, i, 0, 0)),
                  pl.BlockSpec((1, tm, vd), blk), _resident(gn_g.shape), _resident(gn_b.shape)],
        out_specs=pl.BlockSpec((1, tm, vd), blk),
        out_shape=jax.ShapeDtypeStruct((b, s, vd), BF16),
        scratch_shapes=[pltpu.VMEM((GLA_HEADS, vd // GLA_HEADS, kd // GLA_HEADS), F32)],
        compiler_params=_params("arbitrary", "arbitrary"),
        name="gla_rec",
    )(qd, ki, ks, v, dec, r, gn_g, gn_b)


def _fox_kv_kernel(x_ref, wk_ref, wvt_ref, wf_ref, bf_ref, tri_ref, spread_ref, group_ref, eye_ref,
                   k_ref, vt_ref, crelt_ref, cblk_ref, cend_ref, kpm_ref, carry_ref, kmax_ref):
    @pl.when(pl.program_id(1) == 0)
    def _():
        carry_ref[...] = jnp.zeros_like(carry_ref)
        kmax_ref[...] = jnp.zeros_like(kmax_ref)

    tk = x_ref.shape[1]
    xb = x_ref[0].astype(BF16)
    log_f = _log_sigmoid(_dot(xb, wf_ref[...]) + bf_ref[...])
    c_rel = _tri_cumsum(tri_ref[...], log_f) * LOG2E
    terms = jnp.concatenate(_split3(-c_rel), axis=1)
    bias = _dot(terms, spread_ref[...])
    crelt_ref[0, 0] = -_dot_nt(eye_ref[...], terms)
    k = _dot(xb, wk_ref[...])
    slot_lane = lax.broadcasted_iota(jnp.int32, k.shape, 1) % FOX_SLOT
    is_one_lane = (slot_lane >= FOX_HEAD_DIM + 3) & (slot_lane < FOX_HEAD_DIM + 6)
    k_aug = jnp.where(is_one_lane, 1.0, k + bias).astype(BF16)
    for h in range(FOX_HEADS):
        k_ref[0, 0, h] = k_aug[:, h * FOX_SLOT:(h + 1) * FOX_SLOT]
    vt = _dot_nt(wvt_ref[...], xb).astype(BF16).reshape(FOX_HEADS, FOX_HEAD_DIM, tk)
    pad_row = lax.broadcasted_iota(jnp.int32, (FOX_HEADS, FOX_V_ROWS - FOX_HEAD_DIM, tk), 1)
    vt_ref[0, 0] = jnp.concatenate([vt, jnp.where(pad_row == 0, 1.0, 0.0).astype(BF16)], axis=1)
    cblk_ref[0, 0] = carry_ref[...]
    carry_ref[...] = carry_ref[...] + c_rel[tk - 1:tk, :]
    cend_ref[0, 0] = carry_ref[...]
    kr = k.astype(BF16).astype(F32)
    norm2 = _dot((kr * kr).astype(BF16), group_ref[...]) * NORM_SLACK
    kmax_ref[...] = jnp.maximum(kmax_ref[...], jnp.sqrt(jnp.max(norm2, axis=0, keepdims=True)))
    kpm_ref[0, 0] = kmax_ref[...]


def _fox_kv(x, wk_aug, wvt, wf, b_f, tri, spread, group, eye):
    b, s, d = x.shape
    tk = FOX_BLOCK
    nkv = s // tk
    nh = wf.shape[1]
    blk4 = lambda bi, j: (bi, j, 0, 0)
    blk5 = lambda bi, j: (bi, j, 0, 0, 0)
    small = jax.ShapeDtypeStruct((b, nkv, 1, nh), F32)
    return pl.pallas_call(
        _fox_kv_kernel,
        grid=(b, nkv),
        in_specs=[pl.BlockSpec((1, tk, d), lambda bi, j: (bi, j, 0)), _resident(wk_aug.shape),
                  _resident(wvt.shape), _resident(wf.shape), _resident(b_f.shape), _resident(tri.shape),
                  _resident(spread.shape), _resident(group.shape), _resident(eye.shape)],
        out_specs=[pl.BlockSpec((1, 1, nh, tk, FOX_SLOT), blk5),
                   pl.BlockSpec((1, 1, nh, FOX_V_ROWS, tk), blk5),
                   pl.BlockSpec((1, 1, 3 * nh, tk), blk4),
                   pl.BlockSpec((1, 1, 1, nh), blk4), pl.BlockSpec((1, 1, 1, nh), blk4),
                   pl.BlockSpec((1, 1, 1, nh), blk4)],
        out_shape=[jax.ShapeDtypeStruct((b, nkv, nh, tk, FOX_SLOT), BF16),
                   jax.ShapeDtypeStruct((b, nkv, nh, FOX_V_ROWS, tk), BF16),
                   jax.ShapeDtypeStruct((b, nkv, 3 * nh, tk), F32),
                   small, small, small],
        scratch_shapes=[pltpu.VMEM((1, nh), F32), pltpu.VMEM((1, nh), F32)],
        compiler_params=_params("arbitrary", "arbitrary"),
        name="fox_kv",
    )(x, wk_aug, wvt, wf, b_f, tri, spread, group, eye)


def _fox_q_kernel(x_ref, wqt_ref, wg_ref, crelt_ref, spreadt_ref, cblk_ref, cend_ref, kpm_ref,
                  qt_ref, g_ref, need_ref, need_fixed_ref, shift_ref):
    i = pl.program_id(1)
    tq = x_ref.shape[1]
    xb = x_ref[0].astype(BF16)
    q = _dot_nt(wqt_ref[...], xb) * (FOX_HEAD_DIM ** -0.5 * LOG2E)
    q = q.reshape(FOX_HEADS, FOX_HEAD_DIM, tq)
    qt = jnp.concatenate([q, jnp.zeros((FOX_HEADS, FOX_SLOT - FOX_HEAD_DIM, tq), F32)], axis=1)
    qt = qt.reshape(FOX_HEADS * FOX_SLOT, tq)
    c_rows = _dot(spreadt_ref[...], crelt_ref[0, 0].astype(BF16))
    qt3 = (qt + c_rows).reshape(FOX_HEADS, FOX_SLOT, tq)
    slot_row = lax.broadcasted_iota(jnp.int32, qt3.shape, 1)
    is_one_row = (slot_row >= FOX_HEAD_DIM) & (slot_row < FOX_HEAD_DIM + 3)
    qt_ref[0] = jnp.where(is_one_row, 1.0, qt3).astype(BF16)
    g_ref[0] = _dot(xb, wg_ref[...])

    qr = qt.astype(BF16).astype(F32).reshape(FOX_HEADS, FOX_SLOT, tq)
    qn = jnp.sqrt(jnp.max(jnp.sum(qr * qr, axis=1), axis=1, keepdims=True))
    blk = lax.broadcasted_iota(jnp.int32, cblk_ref.shape[1:], 1)
    at_i = blk == i
    c_before = jnp.sum(jnp.where(at_i, cblk_ref[0], 0.0), axis=1, keepdims=True)
    k_norm = jnp.sum(jnp.where(at_i, kpm_ref[0], 0.0), axis=1, keepdims=True)
    qk_bound = NORM_SLACK * qn * k_norm

    def count_blocks(thr):
        needed = (blk < i) & jnp.logical_not(cend_ref[0] > thr)
        return jnp.sum(jnp.where(needed, 1.0, 0.0), axis=1, keepdims=True).astype(jnp.int32)

    need_ref[0, 0] = count_blocks(c_before + EXP2_ZERO_GAP + 2.0 * qk_bound)
    need_fixed_ref[0, 0] = count_blocks(c_before + EXP2_ZERO_GAP)
    shift_ref[0, 0] = qk_bound


def _fox_q(x, wqt, wg, crelt, spreadt, cblk_t, cend_t, kpm_t):
    b, s, d = x.shape
    tm = FOX_BLOCK
    nkv = cblk_t.shape[2]
    per_batch = pl.BlockSpec((1, FOX_HEADS, nkv), lambda bi, i: (bi, 0, 0))
    return pl.pallas_call(
        _fox_q_kernel,
        grid=(b, s // tm),
        in_specs=[pl.BlockSpec((1, tm, d), lambda bi, i: (bi, i, 0)), _resident(wqt.shape),
                  _resident(wg.shape),
                  pl.BlockSpec((1, 1, crelt.shape[2], tm), lambda bi, i: (bi, i, 0, 0)),
                  _resident(spreadt.shape), per_batch, per_batch, per_batch],
        out_specs=[pl.BlockSpec((1, FOX_HEADS, FOX_SLOT, tm), lambda bi, i: (bi, 0, 0, i)),
                   pl.BlockSpec((1, tm, wg.shape[1]), lambda bi, i: (bi, i, 0)),
                   pl.BlockSpec((1, 1, FOX_HEADS, 1), lambda bi, i: (bi, i, 0, 0)),
                   pl.BlockSpec((1, 1, FOX_HEADS, 1), lambda bi, i: (bi, i, 0, 0)),
                   pl.BlockSpec((1, 1, FOX_HEADS, 1), lambda bi, i: (bi, i, 0, 0))],
        out_shape=[jax.ShapeDtypeStruct((b, FOX_HEADS, FOX_SLOT, s), BF16),
                   jax.ShapeDtypeStruct((b, s, wg.shape[1]), F32),
                   jax.ShapeDtypeStruct((b, s // tm, FOX_HEADS, 1), jnp.int32),
                   jax.ShapeDtypeStruct((b, s // tm, FOX_HEADS, 1), jnp.int32),
                   jax.ShapeDtypeStruct((b, s // tm, FOX_HEADS, 1), F32)],
        compiler_params=_params("parallel", "parallel"),
        name="fox_q",
    )(x, wqt, wg, crelt, spreadt, cblk_t, cend_t, kpm_t)


def _fox_attn_kernel(cblk_ref, need_ref, need_fixed_ref, shift_ref, qt_ref, k_ref, vt_ref, g_ref, o_ref,
                     acc_ref, *bufs, nkv):
    tk = k_ref.shape[3]
    for sub in range(FOX_QUERY_BLOCKS_PER_STEP):
        _fox_attn_query_block(cblk_ref, need_ref, need_fixed_ref, shift_ref, qt_ref, k_ref, vt_ref, g_ref, o_ref,
                              acc_ref, bufs, nkv=nkv, i=pl.program_id(2) * FOX_QUERY_BLOCKS_PER_STEP + sub,
                              cols=slice(sub * tk, (sub + 1) * tk))


def _fox_attn_query_block(cblk_ref, need_ref, need_fixed_ref, shift_ref, qt_ref, k_ref, vt_ref, g_ref, o_ref,
                          acc_ref, bufs, *, nkv, i, cols):
    bi = pl.program_id(0)
    hp = pl.program_id(1)
    tk = k_ref.shape[3]
    tq = tk
    nhp = FOX_HEADS_PER_STEP
    assert nhp == 2
    dh = FOX_HEAD_DIM
    unroll = len(bufs)

    def head_base(hh):
        return (bi * FOX_HEADS + hp * nhp + hh) * nkv

    def scores(hh, j):
        return _dot(k_ref[0, j, hh], qt_ref[0, hh, :, cols])

    key_pos = lax.broadcasted_iota(jnp.int32, (tk, tq), 0)
    qry_pos = lax.broadcasted_iota(jnp.int32, (tk, tq), 1)
    bound = [shift_ref[head_base(hh) + i] for hh in range(nhp)]
    fixed_shift_ok = jnp.maximum(bound[0], bound[1]) <= MAX_FIXED_SHIFT

    @pl.when(fixed_shift_ok)
    def _():
        n0 = need_fixed_ref[head_base(0) + i]
        total = n0 + need_fixed_ref[head_base(1) + i]
        lead = unroll - nhp

        def item(e):
            hh = (e >= n0).astype(jnp.int32)
            j = jnp.maximum(i - 1 - (e - hh * n0), 0)
            base = head_base(hh)
            shift = cblk_ref[base + i] - cblk_ref[base + j] - shift_ref[base + i]
            return hh, j, jnp.where(e < total, shift, -jnp.inf)

        def probabilities(e, p_ref):
            hh, j, shift = item(e)
            p_ref[...] = jnp.exp2(scores(hh, j) + shift).astype(BF16)

        def accumulate(e, p_ref):
            hh, j, _ = item(e)
            acc_ref[hh] = acc_ref[hh] + _dot(vt_ref[0, j, hh], p_ref[...])

        def run_group(first, size):
            for u in range(size):
                probabilities(first + u, bufs[u])
            for u in range(size):
                accumulate(first + u, bufs[u])

        for hh in range(nhp):
            p = jnp.where(key_pos <= qry_pos, jnp.exp2(scores(hh, i) - bound[hh]), 0.0)
            bufs[hh][...] = p.astype(BF16)
        for u in range(lead):
            probabilities(u, bufs[nhp + u])
        for hh in range(nhp):
            acc_ref[hh] = _dot(vt_ref[0, i, hh], bufs[hh][...])
        for u in range(lead):
            accumulate(u, bufs[nhp + u])

        def full_group(t, carry):
            run_group(lead + unroll * t, unroll)
            return carry

        left = jnp.maximum(total - lead, 0)
        n_full = left // unroll
        lax.fori_loop(0, n_full, full_group, 0)
        rest = left - n_full * unroll

        @pl.when(rest > unroll // 2)
        def _():
            run_group(lead + n_full * unroll, unroll)

        @pl.when((rest > 0) & (rest <= unroll // 2))
        def _():
            run_group(lead + n_full * unroll, unroll // 2)

    @pl.when(jnp.logical_not(fixed_shift_ok))
    def _():
        for hh in range(nhp):
            base = head_base(hh)
            s = jnp.where(key_pos <= qry_pos, scores(hh, i), -jnp.inf)
            m = jnp.max(s, axis=0, keepdims=True)
            acc_ref[hh] = _dot(vt_ref[0, i, hh], jnp.exp2(s - m).astype(BF16))

            def step(jj, m, hh=hh, base=base):
                j = i - 1 - jj
                s = scores(hh, j) + (cblk_ref[base + i] - cblk_ref[base + j])
                m_new = jnp.maximum(m, jnp.max(s, axis=0, keepdims=True))
                acc_ref[hh] = (jnp.exp2(m - m_new) * acc_ref[hh]
                               + _dot(vt_ref[0, j, hh], jnp.exp2(s - m_new).astype(BF16)))
                return m_new

            lax.fori_loop(0, need_ref[base + i], step, m)

    o = jnp.concatenate([acc_ref[hh, :dh, :] / acc_ref[hh, dh:dh + 1, :] for hh in range(nhp)], axis=0)
    o_ref[0, cols, :] = (jnp.transpose(o) * _sigmoid(g_ref[0, cols, :])).astype(BF16)


def _fox_attn(cblk, need, need_fixed, shift, qt, k, vt, g):
    b, nkv, nh, tk, slot = k.shape
    v_rows = vt.shape[3]
    s = nkv * tk
    tq = FOX_BLOCK
    nhp = FOX_HEADS_PER_STEP
    out_w = nhp * FOX_HEAD_DIM
    smem = pl.BlockSpec(memory_space=pltpu.SMEM)
    tstep = tq * FOX_QUERY_BLOCKS_PER_STEP
    return pl.pallas_call(
        functools.partial(_fox_attn_kernel, nkv=nkv),
        grid=(b, nh // nhp, s // tstep),
        in_specs=[smem, smem, smem, smem,
                  pl.BlockSpec((1, nhp, slot, tstep), lambda bi, hp, i: (bi, hp, 0, i)),
                  pl.BlockSpec((1, nkv, nhp, tk, slot), lambda bi, hp, i: (bi, 0, hp, 0, 0)),
                  pl.BlockSpec((1, nkv, nhp, v_rows, tk), lambda bi, hp, i: (bi, 0, hp, 0, 0)),
                  pl.BlockSpec((1, tstep, out_w), lambda bi, hp, i: (bi, i, hp))],
        out_specs=pl.BlockSpec((1, tstep, out_w), lambda bi, hp, i: (bi, i, hp)),
        out_shape=jax.ShapeDtypeStruct((b, s, nh * FOX_HEAD_DIM), BF16),
        scratch_shapes=[pltpu.VMEM((nhp, v_rows, tq), F32)] + [pltpu.VMEM((tk, tq), BF16)] * FOX_ATTN_UNROLL,
        compiler_params=_params("arbitrary", "arbitrary", "arbitrary"),
        name="fox_attn",
    )(cblk, need, need_fixed, shift, qt, k, vt, g)


def _block_tri(n, block):
    r = jnp.arange(n)
    return ((r[:, None] >= r[None, :]) & (r[:, None] // block == r[None, :] // block)).astype(BF16)


def _spread_heads(w):
    d = w.shape[0]
    w = w.reshape(d, FOX_HEADS, FOX_HEAD_DIM)
    w = jnp.pad(w, ((0, 0), (0, 0), (0, FOX_SLOT - FOX_HEAD_DIM)))
    return w.reshape(d, FOX_HEADS * FOX_SLOT)


def _head_group_matrix():
    return (jnp.arange(FOX_HEADS * FOX_SLOT)[:, None] // FOX_SLOT == jnp.arange(FOX_HEADS)[None, :]).astype(BF16)


def _bias_spread_matrix(first_lane):
    rows = jnp.arange(3 * FOX_HEADS)
    term, head = rows // FOX_HEADS, rows % FOX_HEADS
    cols = head * FOX_SLOT + first_lane + term
    return (jnp.arange(FOX_HEADS * FOX_SLOT)[None, :] == cols[:, None]).astype(BF16)


def kernel(x, p, ffn1_w_in, ffn1_w_out, ln1_g, ln1_b, gla_w_in, gla_w_a2, gla_b_a, gla_gn_g, gla_gn_b, gla_w_o, fox_w_kvf, fox_b_f, fox_w_in, fox_w_o, ln2_g, ln2_b, ffn2_w_in, ffn2_w_out, ple_w_gate, ple_w_proj, ln3_g, ln3_b):
    bsz, seq, d = x.shape
    depth = ffn1_w_in.shape[0]
    n_gla = gla_w_in.shape[0]
    alpha = (2 * depth) ** 0.25
    t = bsz * seq
    fox_w = FOX_HEADS * FOX_HEAD_DIM
    row = lambda a: a.reshape(1, -1)

    tri_chunk = _block_tri(TRI_BLOCK, GLA_CHUNK)
    tri_block = _block_tri(FOX_BLOCK, FOX_BLOCK)
    key_bias_spread = _bias_spread_matrix(FOX_HEAD_DIM)
    query_bias_spread_t = _bias_spread_matrix(FOX_HEAD_DIM + 3).T

    xf = x.reshape(t, d)
    pf = p.reshape(depth, t, p.shape[-1])
    heads_major = lambda a: jnp.transpose(a[:, :, 0, :], (0, 2, 1))
    fox_k = fox_vt = fox_crelt = cblk_t = cend_t = kpm_t = None
    for i in range(depth):
        if i == n_gla:
            wk_aug = _spread_heads(fox_w_kvf[:, :fox_w]).astype(BF16)
            wvt = fox_w_kvf[:, fox_w:2 * fox_w].T.astype(BF16)
            wf = fox_w_kvf[:, 2 * fox_w:].astype(BF16)
            fox_k, fox_vt, fox_crelt, cblk, cend, kpm = _fox_kv(
                xf.reshape(bsz, seq, d), wk_aug, wvt, wf, row(fox_b_f), tri_block, key_bias_spread,
                _head_group_matrix(), jnp.eye(3 * FOX_HEADS, dtype=BF16))
            cblk_t, cend_t, kpm_t = heads_major(cblk), heads_major(cend), heads_major(kpm)
        xf = _ffn(xf, ffn1_w_in[i].astype(BF16), ffn1_w_out[i].astype(BF16), row(ln1_g[i]), row(ln1_b[i]), alpha)
        if i < n_gla:
            qd, ki, ks, v, dec, r = _gla_proj(xf, gla_w_in[i].astype(BF16), gla_w_a2[i].astype(BF16),
                                                 row(gla_b_a[i]), tri_chunk)
            b3 = lambda a: a.reshape(bsz, seq, -1)
            mix = _gla_rec(b3(qd), b3(ki), b3(ks), b3(v), dec.reshape(bsz, seq // GLA_CHUNK, 1, -1), b3(r),
                           row(gla_gn_g[i]), row(gla_gn_b[i])).reshape(t, -1)
            w_o = gla_w_o[i]
        else:
            j = i - n_gla
            wqt = fox_w_in[j][:, :fox_w].T.astype(BF16)
            qt, g, need, need_fixed, shift = _fox_q(xf.reshape(bsz, seq, d), wqt, fox_w_in[j][:, fox_w:].astype(BF16),
                                        fox_crelt, query_bias_spread_t, cblk_t, cend_t, kpm_t)
            flat = lambda a: jnp.transpose(a[:, :, :, 0], (0, 2, 1)).reshape(-1)
            mix = _fox_attn(cblk_t.reshape(-1), flat(need), flat(need_fixed), flat(shift), qt, fox_k, fox_vt,
                            g).reshape(t, -1)
            w_o = fox_w_o[j]
        xf = _ffn(xf, ffn2_w_in[i].astype(BF16), ffn2_w_out[i].astype(BF16), row(ln3_g[i]), row(ln3_b[i]), alpha,
                  mixer=(mix, w_o.astype(BF16), row(ln2_g[i]), row(ln2_b[i])),
                  ple=(pf, i, ple_w_gate[i].astype(BF16), ple_w_proj[i].astype(BF16)))
    return xf.reshape(bsz, seq, d)
```

```python
import functools
import math

import jax
import jax.numpy as jnp
from jax import lax
from jax.experimental import pallas as pl
from jax.experimental.pallas import tpu as pltpu

F32 = jnp.float32
BF16 = jnp.bfloat16

GLA_HEADS = 4
GLA_TAU = 16.0
GLA_CHUNK = 64
FOX_HEADS = 16
FOX_HEAD_DIM = 64
LN_EPS = 1e-5

V7X_VMEM_LIMIT_BYTES = 56 * 1024 * 1024

TOKEN_TILE = 512
FFN_TOKEN_TILE = 1024
FFN_AFTER_MIXER_TOKEN_TILE = 512
FFN_ROW_PIECE = 256
FOX_BLOCK = 512
FOX_SLOT = 128
FOX_HEADS_PER_STEP = 2
FOX_QUERY_BLOCKS_PER_STEP = 4
TRI_BLOCK = 256
LOG2E = math.log2(math.e)
EXP2_ZERO_GAP = 136.0
NORM_SLACK = 1.01
FOX_V_ROWS = 80
FOX_ATTN_UNROLL = 4
MAX_FIXED_SHIFT = 50.0


def _params(*sems):
    return pltpu.CompilerParams(dimension_semantics=sems, vmem_limit_bytes=V7X_VMEM_LIMIT_BYTES)


def _resident(shape):
    zeros = (0,) * len(shape)
    return pl.BlockSpec(shape, lambda *_: zeros, pipeline_mode=pl.Buffered(1))


def _dot(a, b):
    return jnp.dot(a, b, preferred_element_type=F32)


def _dot_nt(a, b):
    return lax.dot_general(a, b, (((1,), (1,)), ((), ())), preferred_element_type=F32)


def _split3(a):
    hi = a.astype(BF16)
    r1 = a - hi.astype(F32)
    mid = r1.astype(BF16)
    lo = (r1 - mid.astype(F32)).astype(BF16)
    return hi, mid, lo


def _tri_cumsum(tri, a):
    hi, mid, lo = _split3(a)
    return _dot(tri, hi) + _dot(tri, mid) + _dot(tri, lo)


def _log_sigmoid(z):
    return jnp.minimum(z, 0.0) - jnp.log1p(jnp.exp(-jnp.abs(z)))


def _sigmoid(z):
    return 1.0 / (1.0 + jnp.exp(-z))


def _layer_norm(z, g, b):
    mu = jnp.mean(z, axis=-1, keepdims=True)
    zc = z - mu
    var = jnp.mean(zc * zc, axis=-1, keepdims=True)
    return zc * lax.rsqrt(var + LN_EPS) * g + b


def _ffn_kernel(*refs, alpha, d_ff, after_mixer):
    if after_mixer:
        (x_ref, mix_ref, wo_ref, g2_ref, b2_ref, win_ref, wout_ref, g_ref, b_ref,
         p_ref, wgate_ref, wproj_ref, o_ref) = refs
    else:
        x_ref, win_ref, wout_ref, g_ref, b_ref, o_ref = refs
    tm = x_ref.shape[0]
    halves = [slice(r, r + FFN_ROW_PIECE) for r in range(0, tm, FFN_ROW_PIECE)]
    if after_mixer:
        xs = [_layer_norm(alpha * x_ref[rows, :] + _dot(mix_ref[rows, :], wo_ref[...]), g2_ref[...], b2_ref[...])
              for rows in halves]
    else:
        xs = [x_ref[rows, :] for rows in halves]
    for rows, x in zip(halves, xs):
        xb = x.astype(BF16)
        gate = _dot(xb, win_ref[:, :d_ff])
        up = _dot(xb, win_ref[:, d_ff:])
        h = (gate * _sigmoid(gate) * up).astype(BF16)
        z = alpha * x + 0.5 * _dot(h, wout_ref[...])
        if after_mixer:
            ple_gate = _sigmoid(_dot(xb, wgate_ref[...]))
            z = z + ple_gate * _dot(p_ref[0, rows, :].astype(BF16), wproj_ref[...])
        o_ref[rows, :] = _layer_norm(z, g_ref[...], b_ref[...])


def _ffn(x, w_in, w_out, ln_g, ln_b, alpha, mixer=None, ple=None):
    assert (mixer is None) == (ple is None)
    t, d = x.shape
    d_ff = w_out.shape[0]
    tm = FFN_TOKEN_TILE if mixer is None else FFN_AFTER_MIXER_TOKEN_TILE
    row = lambda i: (i, 0)
    in_specs = [pl.BlockSpec((tm, d), row)]
    args = [x]
    if mixer is not None:
        mix, w_o, ln2_g, ln2_b = mixer
        in_specs += [pl.BlockSpec((tm, mix.shape[1]), row), _resident(w_o.shape), _resident(ln2_g.shape),
                     _resident(ln2_b.shape)]
        args += [mix, w_o, ln2_g, ln2_b]
    in_specs += [_resident(w_in.shape), _resident(w_out.shape), _resident(ln_g.shape), _resident(ln_b.shape)]
    args += [w_in, w_out, ln_g, ln_b]
    if ple is not None:
        p, layer, w_gate, w_proj = ple
        in_specs += [pl.BlockSpec((1, tm, p.shape[2]), lambda i: (layer, i, 0)), _resident(w_gate.shape),
                     _resident(w_proj.shape)]
        args += [p, w_gate, w_proj]
    return pl.pallas_call(
        functools.partial(_ffn_kernel, alpha=alpha, d_ff=d_ff, after_mixer=mixer is not None),
        grid=(t // tm,),
        in_specs=in_specs,
        out_specs=pl.BlockSpec((tm, d), row),
        out_shape=jax.ShapeDtypeStruct((t, d), F32),
        compiler_params=_params("parallel"),
        name="ffn_after_mixer" if mixer is not None else "ffn",
    )(*args)


def _gla_proj_kernel(x_ref, win_ref, wa2_ref, ba_ref, tri_ref,
                     qd_ref, ki_ref, ks_ref, v_ref, dec_ref, r_ref, *, kd, vd):
    tm = x_ref.shape[0]
    xb = x_ref[...].astype(BF16)
    a_lr = _dot(xb, win_ref[:, 2 * kd + 2 * vd:])
    log_a = _log_sigmoid(_dot(a_lr.astype(BF16), wa2_ref[...]) + ba_ref[...]) * (1.0 / GLA_TAU)
    qk = _dot(xb, win_ref[:, :2 * kd])
    q = qk[:, :kd] * ((kd // GLA_HEADS) ** -0.5)
    k = qk[:, kd:]
    r = _dot(xb, win_ref[:, 2 * kd + vd:2 * kd + 2 * vd])
    v = _dot(xb, win_ref[:, 2 * kd:2 * kd + vd])
    tri = tri_ref[...]
    bcum = jnp.concatenate(
        [_tri_cumsum(tri, log_a[s:s + TRI_BLOCK]) for s in range(0, tm, TRI_BLOCK)], axis=0)
    nc = tm // GLA_CHUNK
    b3 = bcum.reshape(nc, GLA_CHUNK, kd)
    b_last = b3[:, GLA_CHUNK - 1:GLA_CHUNK, :]
    qd_ref[...] = (q * jnp.exp(bcum)).astype(BF16)
    ki_ref[...] = (k * jnp.exp(-bcum)).astype(BF16)
    ks_ref[...] = (k.reshape(nc, GLA_CHUNK, kd) * jnp.exp(b_last - b3)).reshape(tm, kd).astype(BF16)
    dec_ref[...] = jnp.exp(b_last)
    v_ref[...] = v.astype(BF16)
    r_ref[...] = r


def _gla_proj(x, w_in, w_a2, b_a, tri):
    t, d = x.shape
    kd = w_a2.shape[1]
    vd = (w_in.shape[1] - 2 * kd - w_a2.shape[0]) // 2
    tm = TOKEN_TILE
    row = lambda i: (i, 0)
    return pl.pallas_call(
        functools.partial(_gla_proj_kernel, kd=kd, vd=vd),
        grid=(t // tm,),
        in_specs=[pl.BlockSpec((tm, d), row), _resident(w_in.shape), _resident(w_a2.shape),
                  _resident(b_a.shape), _resident(tri.shape)],
        out_specs=[pl.BlockSpec((tm, kd), row), pl.BlockSpec((tm, kd), row), pl.BlockSpec((tm, kd), row),
                   pl.BlockSpec((tm, vd), row), pl.BlockSpec((tm // GLA_CHUNK, 1, kd), lambda i: (i, 0, 0)),
                   pl.BlockSpec((tm, vd), row)],
        out_shape=[jax.ShapeDtypeStruct((t, kd), BF16)] * 3 + [
            jax.ShapeDtypeStruct((t, vd), BF16),
            jax.ShapeDtypeStruct((t // GLA_CHUNK, 1, kd), F32),
            jax.ShapeDtypeStruct((t, vd), F32)],
        compiler_params=_params("parallel"),
        name="gla_proj",
    )(x, w_in, w_a2, b_a, tri)


def _gla_rec_kernel(qd_ref, ki_ref, ks_ref, v_ref, dec_ref, r_ref, gng_ref, gnb_ref, o_ref, st_ref):
    tm = qd_ref.shape[1]
    dk = qd_ref.shape[2] // GLA_HEADS
    dv = v_ref.shape[2] // GLA_HEADS

    @pl.when(pl.program_id(1) == 0)
    def _():
        st_ref[...] = jnp.zeros_like(st_ref)

    rows = lax.broadcasted_iota(jnp.int32, (GLA_CHUNK, GLA_CHUNK), 0)
    cols = lax.broadcasted_iota(jnp.int32, (GLA_CHUNK, GLA_CHUNK), 1)
    causal = cols <= rows

    def chunk(c, carry):
        r0 = pl.multiple_of(c * GLA_CHUNK, GLA_CHUNK)
        rs = pl.ds(r0, GLA_CHUNK)
        for h in range(GLA_HEADS):
            ksl = slice(h * dk, (h + 1) * dk)
            vsl = slice(h * dv, (h + 1) * dv)
            qd = qd_ref[0, rs, ksl]
            ki = ki_ref[0, rs, ksl]
            ks = ks_ref[0, rs, ksl]
            vv = v_ref[0, rs, vsl]
            attn = jnp.where(causal, _dot_nt(qd, ki), 0.0).astype(BF16)
            st = st_ref[h]
            o = _dot(attn, vv) + _dot_nt(qd, st.astype(BF16))
            vt = jnp.transpose(vv.astype(F32)).astype(BF16)
            st_ref[h] = st * dec_ref[0, c, :, ksl] + _dot(vt, ks)
            on = _layer_norm(o, gng_ref[:, vsl], gnb_ref[:, vsl])
            r = r_ref[0, rs, vsl]
            o_ref[0, rs, vsl] = (on * (r * _sigmoid(r))).astype(BF16)
        return carry

    lax.fori_loop(0, tm // GLA_CHUNK, chunk, 0, unroll=True)


def _gla_rec(qd, ki, ks, v, dec, r, gn_g, gn_b):
    b, s, kd = qd.shape
    vd = v.shape[2]
    tm = TOKEN_TILE
    blk = lambda bi, i: (bi, i, 0)
    return pl.pallas_call(
        _gla_rec_kernel,
        grid=(b, s // tm),
        in_specs=[pl.BlockSpec((1, tm, kd), blk), pl.BlockSpec((1, tm, kd), blk), pl.BlockSpec((1, tm, kd), blk),
                  pl.BlockSpec((1, tm, vd), blk),
                  pl.BlockSpec((1, tm // GLA_CHUNK, 1, kd), lambda bi, i: (bi, i, 0, 0)),
                  pl.BlockSpec((1, tm, vd), blk), _resident(gn_g.shape), _resident(gn_b.shape)],
        out_specs=pl.BlockSpec((1, tm, vd), blk),
        out_shape=jax.ShapeDtypeStruct((b, s, vd), BF16),
        scratch_shapes=[pltpu.VMEM((GLA_HEADS, vd // GLA_HEADS, kd // GLA_HEADS), F32)],
        compiler_params=_params("arbitrary", "arbitrary"),
        name="gla_rec",
    )(qd, ki, ks, v, dec, r, gn_g, gn_b)


def _fox_kv_kernel(x_ref, wk_ref, wvt_ref, wf_ref, bf_ref, tri_ref, spread_ref, group_ref, eye_ref,
                   k_ref, vt_ref, crelt_ref, cblk_ref, cend_ref, kpm_ref, carry_ref, kmax_ref):
    @pl.when(pl.program_id(1) == 0)
    def _():
        carry_ref[...] = jnp.zeros_like(carry_ref)
        kmax_ref[...] = jnp.zeros_like(kmax_ref)

    tk = x_ref.shape[1]
    xb = x_ref[0].astype(BF16)
    log_f = _log_sigmoid(_dot(xb, wf_ref[...]) + bf_ref[...])
    c_rel = _tri_cumsum(tri_ref[...], log_f) * LOG2E
    terms = jnp.concatenate(_split3(-c_rel), axis=1)
    bias = _dot(terms, spread_ref[...])
    crelt_ref[0, 0] = -_dot_nt(eye_ref[...], terms)
    k = _dot(xb, wk_ref[...])
    slot_lane = lax.broadcasted_iota(jnp.int32, k.shape, 1) % FOX_SLOT
    is_one_lane = (slot_lane >= FOX_HEAD_DIM + 3) & (slot_lane < FOX_HEAD_DIM + 6)
    k_aug = jnp.where(is_one_lane, 1.0, k + bias).astype(BF16)
    for h in range(FOX_HEADS):
        k_ref[0, 0, h] = k_aug[:, h * FOX_SLOT:(h + 1) * FOX_SLOT]
    vt = _dot_nt(wvt_ref[...], xb).astype(BF16).reshape(FOX_HEADS, FOX_HEAD_DIM, tk)
    pad_row = lax.broadcasted_iota(jnp.int32, (FOX_HEADS, FOX_V_ROWS - FOX_HEAD_DIM, tk), 1)
    vt_ref[0, 0] = jnp.concatenate([vt, jnp.where(pad_row == 0, 1.0, 0.0).astype(BF16)], axis=1)
    cblk_ref[0, 0] = carry_ref[...]
    carry_ref[...] = carry_ref[...] + c_rel[tk - 1:tk, :]
    cend_ref[0, 0] = carry_ref[...]
    kr = k.astype(BF16).astype(F32)
    norm2 = _dot((kr * kr).astype(BF16), group_ref[...]) * NORM_SLACK
    kmax_ref[...] = jnp.maximum(kmax_ref[...], jnp.sqrt(jnp.max(norm2, axis=0, keepdims=True)))
    kpm_ref[0, 0] = kmax_ref[...]


def _fox_kv(x, wk_aug, wvt, wf, b_f, tri, spread, group, eye):
    b, s, d = x.shape
    tk = FOX_BLOCK
    nkv = s // tk
    nh = wf.shape[1]
    blk4 = lambda bi, j: (bi, j, 0, 0)
    blk5 = lambda bi, j: (bi, j, 0, 0, 0)
    small = jax.ShapeDtypeStruct((b, nkv, 1, nh), F32)
    return pl.pallas_call(
        _fox_kv_kernel,
        grid=(b, nkv),
        in_specs=[pl.BlockSpec((1, tk, d), lambda bi, j: (bi, j, 0)), _resident(wk_aug.shape),
                  _resident(wvt.shape), _resident(wf.shape), _resident(b_f.shape), _resident(tri.shape),
                  _resident(spread.shape), _resident(group.shape), _resident(eye.shape)],
        out_specs=[pl.BlockSpec((1, 1, nh, tk, FOX_SLOT), blk5),
                   pl.BlockSpec((1, 1, nh, FOX_V_ROWS, tk), blk5),
                   pl.BlockSpec((1, 1, 3 * nh, tk), blk4),
                   pl.BlockSpec((1, 1, 1, nh), blk4), pl.BlockSpec((1, 1, 1, nh), blk4),
                   pl.BlockSpec((1, 1, 1, nh), blk4)],
        out_shape=[jax.ShapeDtypeStruct((b, nkv, nh, tk, FOX_SLOT), BF16),
                   jax.ShapeDtypeStruct((b, nkv, nh, FOX_V_ROWS, tk), BF16),
                   jax.ShapeDtypeStruct((b, nkv, 3 * nh, tk), F32),
                   small, small, small],
        scratch_shapes=[pltpu.VMEM((1, nh), F32), pltpu.VMEM((1, nh), F32)],
        compiler_params=_params("arbitrary", "arbitrary"),
        name="fox_kv",
    )(x, wk_aug, wvt, wf, b_f, tri, spread, group, eye)


def _fox_q_kernel(x_ref, wqt_ref, wg_ref, crelt_ref, spreadt_ref, cblk_ref, cend_ref, kpm_ref,
                  qt_ref, g_ref, need_ref, need_fixed_ref, shift_ref):
    i = pl.program_id(1)
    tq = x_ref.shape[1]
    xb = x_ref[0].astype(BF16)
    q = _dot_nt(wqt_ref[...], xb) * (FOX_HEAD_DIM ** -0.5 * LOG2E)
    q = q.reshape(FOX_HEADS, FOX_HEAD_DIM, tq)
    qt = jnp.concatenate([q, jnp.zeros((FOX_HEADS, FOX_SLOT - FOX_HEAD_DIM, tq), F32)], axis=1)
    qt = qt.reshape(FOX_HEADS * FOX_SLOT, tq)
    c_rows = _dot(spreadt_ref[...], crelt_ref[0, 0].astype(BF16))
    qt3 = (qt + c_rows).reshape(FOX_HEADS, FOX_SLOT, tq)
    slot_row = lax.broadcasted_iota(jnp.int32, qt3.shape, 1)
    is_one_row = (slot_row >= FOX_HEAD_DIM) & (slot_row < FOX_HEAD_DIM + 3)
    qt_ref[0] = jnp.where(is_one_row, 1.0, qt3).astype(BF16)
    g_ref[0] = _dot(xb, wg_ref[...])

    qr = qt.astype(BF16).astype(F32).reshape(FOX_HEADS, FOX_SLOT, tq)
    qn = jnp.sqrt(jnp.max(jnp.sum(qr * qr, axis=1), axis=1, keepdims=True))
    blk = lax.broadcasted_iota(jnp.int32, cblk_ref.shape[1:], 1)
    at_i = blk == i
    c_before = jnp.sum(jnp.where(at_i, cblk_ref[0], 0.0), axis=1, keepdims=True)
    k_norm = jnp.sum(jnp.where(at_i, kpm_ref[0], 0.0), axis=1, keepdims=True)
    qk_bound = NORM_SLACK * qn * k_norm

    def count_blocks(thr):
        needed = (blk < i) & jnp.logical_not(cend_ref[0] > thr)
        return jnp.sum(jnp.where(needed, 1.0, 0.0), axis=1, keepdims=True).astype(jnp.int32)

    need_ref[0, 0] = count_blocks(c_before + EXP2_ZERO_GAP + 2.0 * qk_bound)
    need_fixed_ref[0, 0] = count_blocks(c_before + EXP2_ZERO_GAP)
    shift_ref[0, 0] = qk_bound


def _fox_q(x, wqt, wg, crelt, spreadt, cblk_t, cend_t, kpm_t):
    b, s, d = x.shape
    tm = FOX_BLOCK
    nkv = cblk_t.shape[2]
    per_batch = pl.BlockSpec((1, FOX_HEADS, nkv), lambda bi, i: (bi, 0, 0))
    return pl.pallas_call(
        _fox_q_kernel,
        grid=(b, s // tm),
        in_specs=[pl.BlockSpec((1, tm, d), lambda bi, i: (bi, i, 0)), _resident(wqt.shape),
                  _resident(wg.shape),
                  pl.BlockSpec((1, 1, crelt.shape[2], tm), lambda bi, i: (bi, i, 0, 0)),
                  _resident(spreadt.shape), per_batch, per_batch, per_batch],
        out_specs=[pl.BlockSpec((1, FOX_HEADS, FOX_SLOT, tm), lambda bi, i: (bi, 0, 0, i)),
                   pl.BlockSpec((1, tm, wg.shape[1]), lambda bi, i: (bi, i, 0)),
                   pl.BlockSpec((1, 1, FOX_HEADS, 1), lambda bi, i: (bi, i, 0, 0)),
                   pl.BlockSpec((1, 1, FOX_HEADS, 1), lambda bi, i: (bi, i, 0, 0)),
                   pl.BlockSpec((1, 1, FOX_HEADS, 1), lambda bi, i: (bi, i, 0, 0))],
        out_shape=[jax.ShapeDtypeStruct((b, FOX_HEADS, FOX_SLOT, s), BF16),
                   jax.ShapeDtypeStruct((b, s, wg.shape[1]), F32),
                   jax.ShapeDtypeStruct((b, s // tm, FOX_HEADS, 1), jnp.int32),
                   jax.ShapeDtypeStruct((b, s // tm, FOX_HEADS, 1), jnp.int32),
                   jax.ShapeDtypeStruct((b, s // tm, FOX_HEADS, 1), F32)],
        compiler_params=_params("parallel", "parallel"),
        name="fox_q",
    )(x, wqt, wg, crelt, spreadt, cblk_t, cend_t, kpm_t)


def _fox_attn_kernel(cblk_ref, need_ref, need_fixed_ref, shift_ref, qt_ref, k_ref, vt_ref, g_ref, o_ref,
                     acc_ref, *bufs, nkv):
    tk = k_ref.shape[3]
    for sub in range(FOX_QUERY_BLOCKS_PER_STEP):
        _fox_attn_query_block(cblk_ref, need_ref, need_fixed_ref, shift_ref, qt_ref, k_ref, vt_ref, g_ref, o_ref,
                              acc_ref, bufs, nkv=nkv, i=pl.program_id(2) * FOX_QUERY_BLOCKS_PER_STEP + sub,
                              cols=slice(sub * tk, (sub + 1) * tk))


def _fox_attn_query_block(cblk_ref, need_ref, need_fixed_ref, shift_ref, qt_ref, k_ref, vt_ref, g_ref, o_ref,
                          acc_ref, bufs, *, nkv, i, cols):
    bi = pl.program_id(0)
    hp = pl.program_id(1)
    tk = k_ref.shape[3]
    tq = tk
    nhp = FOX_HEADS_PER_STEP
    assert nhp == 2
    dh = FOX_HEAD_DIM
    unroll = len(bufs)

    def head_base(hh):
        return (bi * FOX_HEADS + hp * nhp + hh) * nkv

    def scores(hh, j):
        return _dot(k_ref[0, j, hh], qt_ref[0, hh, :, cols])

    key_pos = lax.broadcasted_iota(jnp.int32, (tk, tq), 0)
    qry_pos = lax.broadcasted_iota(jnp.int32, (tk, tq), 1)
    bound = [shift_ref[head_base(hh) + i] for hh in range(nhp)]
    fixed_shift_ok = jnp.maximum(bound[0], bound[1]) <= MAX_FIXED_SHIFT

    @pl.when(fixed_shift_ok)
    def _():
        n0 = need_fixed_ref[head_base(0) + i]
        total = n0 + need_fixed_ref[head_base(1) + i]
        lead = unroll - nhp

        def item(e):
            hh = (e >= n0).astype(jnp.int32)
            j = jnp.maximum(i - 1 - (e - hh * n0), 0)
            base = head_base(hh)
            shift = cblk_ref[base + i] - cblk_ref[base + j] - shift_ref[base + i]
            return hh, j, jnp.where(e < total, shift, -jnp.inf)

        def probabilities(e, p_ref):
            hh, j, shift = item(e)
            p_ref[...] = jnp.exp2(scores(hh, j) + shift).astype(BF16)

        def accumulate(e, p_ref):
            hh, j, _ = item(e)
            acc_ref[hh] = acc_ref[hh] + _dot(vt_ref[0, j, hh], p_ref[...])

        def run_group(first, size):
            for u in range(size):
                probabilities(first + u, bufs[u])
            for u in range(size):
                accumulate(first + u, bufs[u])

        for hh in range(nhp):
            p = jnp.where(key_pos <= qry_pos, jnp.exp2(scores(hh, i) - bound[hh]), 0.0)
            bufs[hh][...] = p.astype(BF16)
        for u in range(lead):
            probabilities(u, bufs[nhp + u])
        for hh in range(nhp):
            acc_ref[hh] = _dot(vt_ref[0, i, hh], bufs[hh][...])
        for u in range(lead):
            accumulate(u, bufs[nhp + u])

        def full_group(t, carry):
            run_group(lead + unroll * t, unroll)
            return carry

        left = jnp.maximum(total - lead, 0)
        n_full = left // unroll
        lax.fori_loop(0, n_full, full_group, 0)
        rest = left - n_full * unroll

        @pl.when(rest > unroll // 2)
        def _():
            run_group(lead + n_full * unroll, unroll)

        @pl.when((rest > 0) & (rest <= unroll // 2))
        def _():
            run_group(lead + n_full * unroll, unroll // 2)

    @pl.when(jnp.logical_not(fixed_shift_ok))
    def _():
        for hh in range(nhp):
            base = head_base(hh)
            s = jnp.where(key_pos <= qry_pos, scores(hh, i), -jnp.inf)
            m = jnp.max(s, axis=0, keepdims=True)
            acc_ref[hh] = _dot(vt_ref[0, i, hh], jnp.exp2(s - m).astype(BF16))

            def step(jj, m, hh=hh, base=base):
                j = i - 1 - jj
                s = scores(hh, j) + (cblk_ref[base + i] - cblk_ref[base + j])
                m_new = jnp.maximum(m, jnp.max(s, axis=0, keepdims=True))
                acc_ref[hh] = (jnp.exp2(m - m_new) * acc_ref[hh]
                               + _dot(vt_ref[0, j, hh], jnp.exp2(s - m_new).astype(BF16)))
                return m_new

            lax.fori_loop(0, need_ref[base + i], step, m)

    o = jnp.concatenate([acc_ref[hh, :dh, :] / acc_ref[hh, dh:dh + 1, :] for hh in range(nhp)], axis=0)
    o_ref[0, cols, :] = (jnp.transpose(o) * _sigmoid(g_ref[0, cols, :])).astype(BF16)


def _fox_attn(cblk, need, need_fixed, shift, qt, k, vt, g):
    b, nkv, nh, tk, slot = k.shape
    v_rows = vt.shape[3]
    s = nkv * tk
    tq = FOX_BLOCK
    nhp = FOX_HEADS_PER_STEP
    out_w = nhp * FOX_HEAD_DIM
    smem = pl.BlockSpec(memory_space=pltpu.SMEM)
    tstep = tq * FOX_QUERY_BLOCKS_PER_STEP
    return pl.pallas_call(
        functools.partial(_fox_attn_kernel, nkv=nkv),
        grid=(b, nh // nhp, s // tstep),
        in_specs=[smem, smem, smem, smem,
                  pl.BlockSpec((1, nhp, slot, tstep), lambda bi, hp, i: (bi, hp, 0, i)),
                  pl.BlockSpec((1, nkv, nhp, tk, slot), lambda bi, hp, i: (bi, 0, hp, 0, 0)),
                  pl.BlockSpec((1, nkv, nhp, v_rows, tk), lambda bi, hp, i: (bi, 0, hp, 0, 0)),
                  pl.BlockSpec((1, tstep, out_w), lambda bi, hp, i: (bi, i, hp))],
        out_specs=pl.BlockSpec((1, tstep, out_w), lambda bi, hp, i: (bi, i, hp)),
        out_shape=jax.ShapeDtypeStruct((b, s, nh * FOX_HEAD_DIM), BF16),
        scratch_shapes=[pltpu.VMEM((nhp, v_rows, tq), F32)] + [pltpu.VMEM((tk, tq), BF16)] * FOX_ATTN_UNROLL,
        compiler_params=_params("arbitrary", "arbitrary", "arbitrary"),
        name="fox_attn",
    )(cblk, need, need_fixed, shift, qt, k, vt, g)


def _block_tri(n, block):
    r = jnp.arange(n)
    return ((r[:, None] >= r[None, :]) & (r[:, None] // block == r[None, :] // block)).astype(BF16)


def _spread_heads(w):
    d = w.shape[0]
    w = w.reshape(d, FOX_HEADS, FOX_HEAD_DIM)
    w = jnp.pad(w, ((0, 0), (0, 0), (0, FOX_SLOT - FOX_HEAD_DIM)))
    return w.reshape(d, FOX_HEADS * FOX_SLOT)


def _head_group_matrix():
    return (jnp.arange(FOX_HEADS * FOX_SLOT)[:, None] // FOX_SLOT == jnp.arange(FOX_HEADS)[None, :]).astype(BF16)


def _bias_spread_matrix(first_lane):
    rows = jnp.arange(3 * FOX_HEADS)
    term, head = rows // FOX_HEADS, rows % FOX_HEADS
    cols = head * FOX_SLOT + first_lane + term
    return (jnp.arange(FOX_HEADS * FOX_SLOT)[None, :] == cols[:, None]).astype(BF16)


def kernel(x, p, ffn1_w_in, ffn1_w_out, ln1_g, ln1_b, gla_w_in, gla_w_a2, gla_b_a, gla_gn_g, gla_gn_b, gla_w_o, fox_w_kvf, fox_b_f, fox_w_in, fox_w_o, ln2_g, ln2_b, ffn2_w_in, ffn2_w_out, ple_w_gate, ple_w_proj, ln3_g, ln3_b):
    bsz, seq, d = x.shape
    depth = ffn1_w_in.shape[0]
    n_gla = gla_w_in.shape[0]
    alpha = (2 * depth) ** 0.25
    t = bsz * seq
    fox_w = FOX_HEADS * FOX_HEAD_DIM
    row = lambda a: a.reshape(1, -1)

    tri_chunk = _block_tri(TRI_BLOCK, GLA_CHUNK)
    tri_block = _block_tri(FOX_BLOCK, FOX_BLOCK)
    key_bias_spread = _bias_spread_matrix(FOX_HEAD_DIM)
    query_bias_spread_t = _bias_spread_matrix(FOX_HEAD_DIM + 3).T

    xf = x.reshape(t, d)
    pf = p.reshape(depth, t, p.shape[-1])
    heads_major = lambda a: jnp.transpose(a[:, :, 0, :], (0, 2, 1))
    fox_k = fox_vt = fox_crelt = cblk_t = cend_t = kpm_t = None
    for i in range(depth):
        if i == n_gla:
            wk_aug = _spread_heads(fox_w_kvf[:, :fox_w]).astype(BF16)
            wvt = fox_w_kvf[:, fox_w:2 * fox_w].T.astype(BF16)
            wf = fox_w_kvf[:, 2 * fox_w:].astype(BF16)
            fox_k, fox_vt, fox_crelt, cblk, cend, kpm = _fox_kv(
                xf.reshape(bsz, seq, d), wk_aug, wvt, wf, row(fox_b_f), tri_block, key_bias_spread,
                _head_group_matrix(), jnp.eye(3 * FOX_HEADS, dtype=BF16))
            cblk_t, cend_t, kpm_t = heads_major(cblk), heads_major(cend), heads_major(kpm)
        xf = _ffn(xf, ffn1_w_in[i].astype(BF16), ffn1_w_out[i].astype(BF16), row(ln1_g[i]), row(ln1_b[i]), alpha)
        if i < n_gla:
            qd, ki, ks, v, dec, r = _gla_proj(xf, gla_w_in[i].astype(BF16), gla_w_a2[i].astype(BF16),
                                                 row(gla_b_a[i]), tri_chunk)
            b3 = lambda a: a.reshape(bsz, seq, -1)
            mix = _gla_rec(b3(qd), b3(ki), b3(ks), b3(v), dec.reshape(bsz, seq // GLA_CHUNK, 1, -1), b3(r),
                           row(gla_gn_g[i]), row(gla_gn_b[i])).reshape(t, -1)
            w_o = gla_w_o[i]
        else:
            j = i - n_gla
            wqt = fox_w_in[j][:, :fox_w].T.astype(BF16)
            qt, g, need, need_fixed, shift = _fox_q(xf.reshape(bsz, seq, d), wqt, fox_w_in[j][:, fox_w:].astype(BF16),
                                        fox_crelt, query_bias_spread_t, cblk_t, cend_t, kpm_t)
            flat = lambda a: jnp.transpose(a[:, :, :, 0], (0, 2, 1)).reshape(-1)
            mix = _fox_attn(cblk_t.reshape(-1), flat(need), flat(need_fixed), flat(shift), qt, fox_k, fox_vt,
                            g).reshape(t, -1)
            w_o = fox_w_o[j]
        xf = _ffn(xf, ffn2_w_in[i].astype(BF16), ffn2_w_out[i].astype(BF16), row(ln3_g[i]), row(ln3_b[i]), alpha,
                  mixer=(mix, w_o.astype(BF16), row(ln2_g[i]), row(ln2_b[i])),
                  ple=(pf, i, ple_w_gate[i].astype(BF16), ple_w_proj[i].astype(BF16)))
    return xf.reshape(bsz, seq, d)
```
